```python
import math
import jax, jax.numpy as jnp
from jax import lax
import numpy as np

D_MODEL = 2048
BATCH = 32
SEQ = 256
DEPTH = 4
DEC_BATCH = 8
DEC_SEQ = 1024
PAST_LEN = 512

GRID_W = 64
N_MIXERS = 3
N_HYENA = (DEPTH + 2) // 3
N_ATTN = (DEPTH + 1) // 3
N_SSM = DEPTH // 3
D_FF = 4 * D_MODEL
N_MOD = 6
NORM_EPS = 1e-6
HY_PE_BANDS = 16
HY_PE_DIM = 2 * HY_PE_BANDS
HY_PE_MIN_PERIOD = 2.0
HY_PE_MAX_PERIOD = 4096.0
HY_FILTER_W = 64
HY_SHORT = 3
N_HEADS = 8
HEAD_DIM = D_MODEL // N_HEADS // 2
V_DIM = 2 * HEAD_DIM
ROPE_BASE = 10000.0
Q_BLOCK = 128
S5_GROUP = 16
S5_GROUPS = D_MODEL // S5_GROUP
S5_STATE = 64

kernel_name = 'hybrid_diffusion_prefix_step'

F32 = jnp.float32


def rms_norm(x, g):
    xf = x.astype(F32)
    y = xf * lax.rsqrt(jnp.mean(xf * xf, axis=-1, keepdims=True) + NORM_EPS)
    return (y * g.astype(F32)).astype(x.dtype)


def adaln_params(cond, w, b):
    m = (jax.nn.silu(cond) @ w + b).reshape(cond.shape[0], N_MOD, D_MODEL)
    return [m[:, j, None, :] for j in range(N_MOD)]


def pre_mod(x, g, shift, scale):
    return rms_norm(x, g) * (1 + scale) + shift


def post_add(x, o, g, gate):
    return x + gate * rms_norm(o, g)


def sq_relu_mlp(h, w1, w2):
    return jnp.square(jax.nn.relu(h @ w1)) @ w2


def short_conv(x, w, b):
    xp = jnp.pad(x, ((0, 0), (1, 1), (0, 0)))
    return xp[:, :-2] * w[0] + xp[:, 1:-1] * w[1] + xp[:, 2:] * w[2] + b


def hyena_filter(L, w1, b1, fr1, w2, b2, fr2, w3, log_alpha):
    t = jnp.arange(L, dtype=F32)
    periods = HY_PE_MIN_PERIOD * (HY_PE_MAX_PERIOD / HY_PE_MIN_PERIOD) ** (
        jnp.arange(HY_PE_BANDS, dtype=F32) / (HY_PE_BANDS - 1))
    ang = t[:, None] * (2.0 * math.pi / periods)[None]
    pe = jnp.concatenate([jnp.sin(ang), jnp.cos(ang)], axis=-1)
    h = jnp.sin(fr1.astype(F32) * (pe @ w1.astype(F32) + b1.astype(F32)))
    h = jnp.sin(fr2.astype(F32) * (h @ w2.astype(F32) + b2.astype(F32)))
    h = (h @ w3.astype(F32)).reshape(L, 2, D_MODEL)
    h = h * jnp.exp(-jnp.exp(log_alpha.astype(F32))[None, None] * t[:, None, None])
    k = jnp.concatenate([h[:, 0], jnp.zeros((1, D_MODEL), F32), h[:0:-1, 1]], axis=0)
    return k / (jnp.sum(jnp.abs(k), axis=0, keepdims=True) + 1e-6)


def long_conv(u, k):
    L = u.shape[1]
    U = jnp.fft.rfft(u.astype(F32), n=2 * L, axis=1)
    K = jnp.fft.rfft(k, axis=0)
    return jnp.fft.irfft(U * K[None], n=2 * L, axis=1)[:, :L]


def hyena_mixer(h, w_in, b_in, w_sh, b_sh, filt, skip, w_out, b_out):
    L = h.shape[1]
    z = short_conv(h @ w_in + b_in, w_sh, b_sh)
    x0, x1, v = jnp.split(z, 3, axis=-1)
    v = v * x1
    y = long_conv(v, hyena_filter(L, *filt)).astype(h.dtype) + v * skip
    return (x0 * y) @ w_out + b_out


def axial_rope(L):
    rows = L // GRID_W
    row = jnp.repeat(jnp.arange(rows), GRID_W).astype(F32)
    col = jnp.tile(jnp.arange(GRID_W), rows).astype(F32)
    half = HEAD_DIM // 2
    inv = ROPE_BASE ** (-jnp.arange(0, half, 2, dtype=F32) / half)
    ar = row[:, None] * inv
    ac = col[:, None] * inv
    ang = jnp.concatenate([ar, ar, ac, ac], axis=-1)
    return jnp.cos(ang), jnp.sin(ang)


def rotate_half_axial(x):
    a, b, c, d = jnp.split(x, 4, axis=-1)
    return jnp.concatenate([-b, a, -d, c], axis=-1)


def apply_rope(x, cos, sin):
    cos = cos[None, :, None, None]
    sin = sin[None, :, None, None]
    xf = x.astype(F32)
    return (xf * cos + rotate_half_axial(xf) * sin).astype(x.dtype)


def diff_qkv(h, w_qkv):
    B, L, _ = h.shape
    q, k, v = jnp.split(h @ w_qkv, 3, axis=-1)
    return (q.reshape(B, L, N_HEADS, 2, HEAD_DIM), k.reshape(B, L, N_HEADS, 2, HEAD_DIM),
            v.reshape(B, L, N_HEADS, V_DIM))


def diff_lambda(lam_p, lam_init):
    lp = lam_p.astype(F32)
    return jnp.exp(jnp.sum(lp[0] * lp[1])) - jnp.exp(jnp.sum(lp[2] * lp[3])) + lam_init


def diff_attend(q, k, v, lam):
    B, S = q.shape[:2]
    nb = S // Q_BLOCK
    qb = jnp.moveaxis(q.reshape(B, nb, Q_BLOCK, N_HEADS, 2, HEAD_DIM), 1, 0)
    kf = k.astype(F32)
    vf = v.astype(F32)
    scale = HEAD_DIM ** -0.5

    def block(qblk):
        s = jnp.einsum('bqhcd,bkhcd->bhcqk', qblk.astype(F32), kf) * scale
        p = jax.nn.softmax(s, axis=-1)
        w = p[:, :, 0] - lam * p[:, :, 1]
        return jnp.einsum('bhqk,bkhe->bqhe', w, vf)

    o = lax.map(block, qb)
    return jnp.moveaxis(o, 0, 1).reshape(B, S, N_HEADS, V_DIM)


def diff_output(o, g_sub, lam_init, w_o, dtype):
    B, S = o.shape[:2]
    o = rms_norm(o, g_sub) * (1.0 - lam_init)
    return o.reshape(B, S, D_MODEL).astype(dtype) @ w_o


def _lin_combine(e1, e2):
    a1, b1 = e1
    a2, b2 = e2
    return a1 * a2, a2 * b1 + b2


def s5_discretize(lam_re, lam_im, log_dt, b_re, b_im):
    lam = lax.complex(jnp.minimum(lam_re.astype(F32), -1e-4), lam_im.astype(F32))
    lam_dt = lam * jnp.exp(log_dt.astype(F32))[:, None]
    lam_bar = jnp.exp(lam_dt)
    b = lax.complex(b_re.astype(F32), b_im.astype(F32))
    b_bar = ((lam_bar - 1.0) / lam)[..., None] * b
    return lam_dt, lam_bar, b_bar


def s5_scan(u, lam_dt, lam_bar, b_bar, s0, reverse):
    L = u.shape[1]
    bu = jnp.einsum('blgh,gph->blgp', u.astype(jnp.complex64), b_bar)
    if reverse:
        bu = jnp.flip(bu, axis=1)
    a = jnp.broadcast_to(lam_bar, (1, L) + lam_bar.shape)
    _, xs = lax.associative_scan(_lin_combine, (a, bu), axis=1)
    steps = jnp.arange(1, L + 1, dtype=F32)
    xs = xs + jnp.exp(lam_dt[None] * steps[:, None, None])[None] * s0[:, None]
    final = xs[:, -1]
    if reverse:
        xs = jnp.flip(xs, axis=1)
    return xs, final


def s5_mixer(h, s0, lam_re, lam_im, log_dt, b_re, b_im, c_re, c_im, d_skip, w_glu, b_glu):
    B, L, _ = h.shape
    u = h.astype(F32).reshape(B, L, S5_GROUPS, S5_GROUP)
    y = d_skip.astype(F32).reshape(S5_GROUPS, S5_GROUP) * u
    finals = []
    for di in range(2):
        lam_dt, lam_bar, b_bar = s5_discretize(lam_re[di], lam_im[di], log_dt[di], b_re[di], b_im[di])
        xs, fin = s5_scan(u, lam_dt, lam_bar, b_bar, s0[:, di], di == 1)
        cm = lax.complex(c_re[di].astype(F32), c_im[di].astype(F32))
        y = y + jnp.einsum('blgp,ghp->blgh', xs, cm).real
        finals.append(fin)
    y = jax.nn.gelu(y.reshape(B, L, D_MODEL)).astype(h.dtype)
    a, g = jnp.split(y @ w_glu + b_glu, 2, axis=-1)
    return a * jax.nn.sigmoid(g), jnp.stack(finals, axis=1)


def setup_inputs(seed: int = 0) -> dict:
    key = jax.random.key(seed)
    ks = iter(jax.random.split(key, 64))

    def nrm(shape, s=1.0):
        return s * jax.random.normal(next(ks), shape, F32)

    D = D_MODEL
    G, P, H = S5_GROUPS, S5_STATE, S5_GROUP
    n = jnp.arange(P, dtype=F32)
    lengths = jnp.geomspace(8.0, 1024.0, D)
    return {
        'x_prompt': nrm((BATCH, SEQ, D)),
        'x_sample': nrm((DEC_BATCH, DEC_SEQ, D)),
        'cache_attn_k': nrm((DEC_BATCH, N_ATTN, PAST_LEN, N_HEADS, 2, HEAD_DIM)),
        'cache_attn_v': nrm((DEC_BATCH, N_ATTN, PAST_LEN, N_HEADS, V_DIM)),
        'state_s5_re': nrm((DEC_BATCH, N_SSM, 2, G, P), 0.1),
        'state_s5_im': nrm((DEC_BATCH, N_SSM, 2, G, P), 0.1),
        'c': nrm((DEC_BATCH, D)),
        'c_ctx': nrm((D,)),
        'w_mod': nrm((DEPTH, D, N_MOD * D), 0.5 * D ** -0.5),
        'b_mod': nrm((DEPTH, N_MOD * D), 0.02),
        'g_norm': 1.0 + nrm((DEPTH, 4, D), 0.02),
        'w_mlp_in': nrm((DEPTH, D, D_FF), D ** -0.5),
        'w_mlp_out': nrm((DEPTH, D_FF, D), D_FF ** -0.5),
        'hy_w_in': nrm((N_HYENA, D, 3 * D), D ** -0.5),
        'hy_b_in': nrm((N_HYENA, 3 * D), 0.02),
        'hy_w_short': nrm((N_HYENA, HY_SHORT, 3 * D), HY_SHORT ** -0.5),
        'hy_b_short': nrm((N_HYENA, 3 * D), 0.02),
        'hy_f_w1': nrm((N_HYENA, HY_PE_DIM, HY_FILTER_W), HY_PE_DIM ** -0.5),
        'hy_f_b1': nrm((N_HYENA, HY_FILTER_W), 0.1),
        'hy_f_freq1': 1.0 + nrm((N_HYENA, HY_FILTER_W), 0.1),
        'hy_f_w2': nrm((N_HYENA, HY_FILTER_W, HY_FILTER_W), HY_FILTER_W ** -0.5),
        'hy_f_b2': nrm((N_HYENA, HY_FILTER_W), 0.1),
        'hy_f_freq2': 1.0 + nrm((N_HYENA, HY_FILTER_W), 0.1),
        'hy_f_w3': nrm((N_HYENA, HY_FILTER_W, 2 * D), HY_FILTER_W ** -0.5),
        'hy_log_alpha': jnp.log(math.log(100.0) / lengths)[None] + nrm((N_HYENA, D), 0.01),
        'hy_skip': nrm((N_HYENA, D)),
        'hy_w_out': nrm((N_HYENA, D, D), D ** -0.5),
        'hy_b_out': nrm((N_HYENA, D), 0.02),
        'at_w_qkv': nrm((N_ATTN, D, 3 * D), D ** -0.5),
        'at_lam': nrm((N_ATTN, 4, HEAD_DIM), 0.1),
        'at_g_sub': 1.0 + nrm((N_ATTN, V_DIM), 0.02),
        'at_w_o': nrm((N_ATTN, D, D), D ** -0.5),
        's5_lam_re': -0.5 + nrm((N_SSM, 2, G, P), 0.01),
        's5_lam_im': math.pi * n + nrm((N_SSM, 2, G, P), 0.01),
        's5_log_dt': jax.random.uniform(next(ks), (N_SSM, 2, G), F32, math.log(1e-3), math.log(1e-1)),
        's5_b_re': nrm((N_SSM, 2, G, P, H), (2 * H) ** -0.5),
        's5_b_im': nrm((N_SSM, 2, G, P, H), (2 * H) ** -0.5),
        's5_c_re': nrm((N_SSM, 2, G, H, P), P ** -0.5),
        's5_c_im': nrm((N_SSM, 2, G, H, P), P ** -0.5),
        's5_d': nrm((N_SSM, D)),
        's5_w_glu': nrm((N_SSM, D, 2 * D), D ** -0.5),
        's5_b_glu': nrm((N_SSM, 2 * D), 0.02),
    }


def reference(x_prompt, x_sample, cache_attn_k, cache_attn_v, state_s5_re, state_s5_im, c, c_ctx,
              w_mod, b_mod, g_norm, w_mlp_in, w_mlp_out,
              hy_w_in, hy_b_in, hy_w_short, hy_b_short, hy_f_w1, hy_f_b1, hy_f_freq1,
              hy_f_w2, hy_f_b2, hy_f_freq2, hy_f_w3, hy_log_alpha, hy_skip, hy_w_out, hy_b_out,
              at_w_qkv, at_lam, at_g_sub, at_w_o,
              s5_lam_re, s5_lam_im, s5_log_dt, s5_b_re, s5_b_im, s5_c_re, s5_c_im, s5_d,
              s5_w_glu, s5_b_glu):
    xc, xl = x_prompt, x_sample
    cos, sin = axial_rope(x_sample.shape[1])
    new_k, new_v, new_s = [], [], []
    for i in range(DEPTH):
        kind, j = i % N_MIXERS, i // N_MIXERS
        mc = adaln_params(c_ctx[None], w_mod[i], b_mod[i])
        ml = adaln_params(c, w_mod[i], b_mod[i])
        g = g_norm[i]
        hc = pre_mod(xc, g[0], mc[0], mc[1])
        hl = pre_mod(xl, g[0], ml[0], ml[1])
        if kind == 0:
            filt = (hy_f_w1[j], hy_f_b1[j], hy_f_freq1[j], hy_f_w2[j], hy_f_b2[j], hy_f_freq2[j],
                    hy_f_w3[j], hy_log_alpha[j])
            hp = (hy_w_in[j], hy_b_in[j], hy_w_short[j], hy_b_short[j], filt, hy_skip[j],
                  hy_w_out[j], hy_b_out[j])
            oc = hyena_mixer(hc, *hp)
            ol = hyena_mixer(hl, *hp)
        elif kind == 1:
            lam_init = 0.8 - 0.6 * math.exp(-0.3 * i)
            lam = diff_lambda(at_lam[j], lam_init)
            qc, kc, vc = diff_qkv(hc, at_w_qkv[j])
            new_k.append(kc)
            new_v.append(vc)
            oc = diff_output(diff_attend(qc, kc, vc, lam), at_g_sub[j], lam_init, at_w_o[j], xc.dtype)
            ql, kl, vl = diff_qkv(hl, at_w_qkv[j])
            ql = apply_rope(ql, cos, sin)
            kl = apply_rope(kl, cos, sin)
            k_all = jnp.concatenate([cache_attn_k[:, j].astype(kl.dtype), kl], axis=1)
            v_all = jnp.concatenate([cache_attn_v[:, j].astype(vl.dtype), vl], axis=1)
            ol = diff_output(diff_attend(ql, k_all, v_all, lam), at_g_sub[j], lam_init, at_w_o[j], xl.dtype)
        else:
            sp = (s5_lam_re[j], s5_lam_im[j], s5_log_dt[j], s5_b_re[j], s5_b_im[j], s5_c_re[j],
                  s5_c_im[j], s5_d[j], s5_w_glu[j], s5_b_glu[j])
            s0c = jnp.zeros((xc.shape[0], 2, S5_GROUPS, S5_STATE), jnp.complex64)
            oc, fin = s5_mixer(hc, s0c, *sp)
            new_s.append(fin)
            s0l = lax.complex(state_s5_re[:, j].astype(F32), state_s5_im[:, j].astype(F32))
            ol, _ = s5_mixer(hl, s0l, *sp)
        xc = post_add(xc, oc, g[1], mc[2])
        xl = post_add(xl, ol, g[1], ml[2])
        xc = post_add(xc, sq_relu_mlp(pre_mod(xc, g[2], mc[3], mc[4]), w_mlp_in[i], w_mlp_out[i]), g[3], mc[5])
        xl = post_add(xl, sq_relu_mlp(pre_mod(xl, g[2], ml[3], ml[4]), w_mlp_in[i], w_mlp_out[i]), g[3], ml[5])
    new_cache_attn_k = jnp.stack(new_k, axis=1)
    new_cache_attn_v = jnp.stack(new_v, axis=1)
    s_all = jnp.stack(new_s, axis=1)
    return (xc, xl, new_cache_attn_k, new_cache_attn_v, s_all.real, s_all.imag)
```

```python
import functools
import math

import numpy as np
import jax
import jax.numpy as jnp
from jax import lax
from jax.experimental import pallas as pl
from jax.experimental.pallas import tpu as pltpu

F32 = jnp.float32
BF16 = jnp.bfloat16
HIGHEST = lax.Precision.HIGHEST

NORM_EPS = 1e-6
N_MOD = 6
N_HEADS = 8
GRID_W = 64
ROPE_BASE = 10000.0
HY_PE_BANDS = 16
HY_PE_MIN_PERIOD = 2.0
HY_PE_MAX_PERIOD = 4096.0
S5_GROUP = 16
S5_CHUNK = 16
MOD_ROWS = 16

VMEM_LIMIT = 56 * 1024 * 1024


def _cparams(sem):
    return pltpu.CompilerParams(dimension_semantics=sem, vmem_limit_bytes=VMEM_LIMIT)


def _rms(x):
    return lax.rsqrt(jnp.mean(x * x, axis=-1, keepdims=True) + NORM_EPS)


def _mod_kernel(c_ref, w_ref, b_ref, o_ref):
    c = c_ref[...]
    s = c * jax.nn.sigmoid(c)
    o_ref[...] = jnp.dot(s, w_ref[...], precision=HIGHEST, preferred_element_type=F32) + b_ref[...]


def _modulation(cond, w_mod, b_mod):
    depth, d, n = w_mod.shape
    tn = 1024
    return pl.pallas_call(
        _mod_kernel,
        grid=(depth, n // tn),
        in_specs=[
            pl.BlockSpec((MOD_ROWS, d), lambda l, j: (0, 0)),
            pl.BlockSpec((None, d, tn), lambda l, j: (l, 0, j)),
            pl.BlockSpec((None, 1, tn), lambda l, j: (l, 0, j)),
        ],
        out_specs=pl.BlockSpec((None, MOD_ROWS, tn), lambda l, j: (l, 0, j)),
        out_shape=jax.ShapeDtypeStruct((depth, MOD_ROWS, n), F32),
        compiler_params=_cparams(("parallel", "parallel")),
        name="adaln_mod",
    )(cond, w_mod, b_mod.reshape(depth, 1, n))


def _mod_spec(d, which, rowfn):
    return pl.BlockSpec((None, None, 1, d), lambda i, *_: (rowfn(i), which, 0, 0))


def _row_fn(tm, seq_len, latent):
    if latent:
        assert seq_len % tm == 0, "a latent row tile must sit inside one sequence"
        return lambda i: 1 + (i * tm) // seq_len
    return lambda i: 0


def _premod(x, g, sh, sc):
    return (x * _rms(x) * g) * (1.0 + sc) + sh


def _premod_mm3_kernel(x_ref, g_ref, sh_ref, sc_ref, wa_ref, wb_ref, wc_ref, ba_ref, bb_ref, bc_ref,
                       oa_ref, ob_ref, oc_ref, h_scr):
    @pl.when(pl.program_id(1) == 0)
    def _():
        h_scr[...] = _premod(x_ref[...], g_ref[...], sh_ref[...], sc_ref[...]).astype(BF16)

    h = h_scr[...]
    oa_ref[...] = jnp.dot(h, wa_ref[...], preferred_element_type=F32) + ba_ref[...]
    ob_ref[...] = jnp.dot(h, wb_ref[...], preferred_element_type=F32) + bb_ref[...]
    oc_ref[...] = jnp.dot(h, wc_ref[...], preferred_element_type=F32) + bc_ref[...]


def _premod_mm3(x, m_l, g, w, b, *, seq_len, latent, shift_idx, scale_idx):
    t, d = x.shape
    tm = min(512, t)
    tn = 512
    nj = d // tn
    rowfn = _row_fn(tm, seq_len, latent)
    wspec = lambda o: pl.BlockSpec((d, tn), lambda i, j: (0, o * nj + j))
    bspec = lambda o: pl.BlockSpec((1, tn), lambda i, j: (0, o * nj + j))
    ospec = pl.BlockSpec((tm, tn), lambda i, j: (i, j))
    osh = jax.ShapeDtypeStruct((t, d), F32)
    return pl.pallas_call(
        _premod_mm3_kernel,
        grid=(t // tm, nj),
        in_specs=[
            pl.BlockSpec((tm, d), lambda i, j: (i, 0)),
            pl.BlockSpec((1, d), lambda i, j: (0, 0)),
            _mod_spec(d, shift_idx, rowfn),
            _mod_spec(d, scale_idx, rowfn),
            wspec(0), wspec(1), wspec(2), bspec(0), bspec(1), bspec(2),
        ],
        out_specs=[ospec, ospec, ospec],
        out_shape=[osh, osh, osh],
        scratch_shapes=[pltpu.VMEM((tm, d), BF16)],
        compiler_params=_cparams(("parallel", "arbitrary")),
        name="premod_mm3",
    )(x, g.reshape(1, d), m_l, m_l, w, w, w, b, b, b)


def _premod_kernel(x_ref, g_ref, sh_ref, sc_ref, o_ref):
    o_ref[...] = _premod(x_ref[...], g_ref[...], sh_ref[...], sc_ref[...])


def _premod_only(x, m_l, g, *, seq_len, latent):
    t, d = x.shape
    tm = min(512, t)
    rowfn = _row_fn(tm, seq_len, latent)
    return pl.pallas_call(
        _premod_kernel,
        grid=(t // tm,),
        in_specs=[
            pl.BlockSpec((tm, d), lambda i: (i, 0)),
            pl.BlockSpec((1, d), lambda i: (0, 0)),
            _mod_spec(d, 0, rowfn),
            _mod_spec(d, 1, rowfn),
        ],
        out_specs=pl.BlockSpec((tm, d), lambda i: (i, 0)),
        out_shape=jax.ShapeDtypeStruct((t, d), F32),
        compiler_params=_cparams(("parallel",)),
        name="premod",
    )(x, g.reshape(1, d), m_l, m_l)


def _post_add(x, o, g, gate):
    return x + gate * (o * _rms(o) * g)


def _mlp_kernel(x_ref, g2_ref, sh_ref, sc_ref, gate_ref, g3_ref, w1_ref, w2_ref, o_ref, h_scr, acc):
    f = pl.program_id(1)

    @pl.when(f == 0)
    def _():
        h_scr[...] = _premod(x_ref[...], g2_ref[...], sh_ref[...], sc_ref[...]).astype(BF16)
        acc[...] = jnp.zeros_like(acc)

    a = jnp.dot(h_scr[...], w1_ref[...], preferred_element_type=F32)
    a = jnp.square(jnp.maximum(a, 0.0)).astype(BF16)
    acc[...] += jnp.dot(a, w2_ref[...], preferred_element_type=F32)

    @pl.when(f == pl.num_programs(1) - 1)
    def _():
        o_ref[...] = _post_add(x_ref[...], acc[...], g3_ref[...], gate_ref[...])


def _mlp(x, m_l, g2, g3, w1, w2, *, seq_len, latent):
    t, d = x.shape
    dff = w1.shape[1]
    tm = min(512, t)
    tf = 1024
    rowfn = _row_fn(tm, seq_len, latent)
    return pl.pallas_call(
        _mlp_kernel,
        grid=(t // tm, dff // tf),
        in_specs=[
            pl.BlockSpec((tm, d), lambda i, f: (i, 0)),
            pl.BlockSpec((1, d), lambda i, f: (0, 0)),
            _mod_spec(d, 3, rowfn),
            _mod_spec(d, 4, rowfn),
            _mod_spec(d, 5, rowfn),
            pl.BlockSpec((1, d), lambda i, f: (0, 0)),
            pl.BlockSpec((d, tf), lambda i, f: (0, f)),
            pl.BlockSpec((tf, d), lambda i, f: (f, 0)),
        ],
        out_specs=pl.BlockSpec((tm, d), lambda i, f: (i, 0)),
        out_shape=jax.ShapeDtypeStruct((t, d), F32),
        scratch_shapes=[pltpu.VMEM((tm, d), BF16), pltpu.VMEM((tm, d), F32)],
        compiler_params=_cparams(("parallel", "arbitrary")),
        name="mlp",
    )(x, g2.reshape(1, d), m_l, m_l, m_l, g3.reshape(1, d), w1, w2)


def _mm_postadd_kernel(a_ref, w_ref, b_ref, x_ref, gate_ref, g_ref, o_ref):
    o = jnp.dot(a_ref[...], w_ref[...], preferred_element_type=F32) + b_ref[...]
    o_ref[...] = _post_add(x_ref[...], o, g_ref[...], gate_ref[...])


def _mm_postadd(a, w, b, x, m_l, g1, *, seq_len, latent):
    t, d = x.shape
    tm = min(512, t)
    rowfn = _row_fn(tm, seq_len, latent)
    return pl.pallas_call(
        _mm_postadd_kernel,
        grid=(t // tm,),
        in_specs=[
            pl.BlockSpec((tm, d), lambda i: (i, 0)),
            pl.BlockSpec((d, d), lambda i: (0, 0)),
            pl.BlockSpec((1, d), lambda i: (0, 0)),
            pl.BlockSpec((tm, d), lambda i: (i, 0)),
            _mod_spec(d, 2, rowfn),
            pl.BlockSpec((1, d), lambda i: (0, 0)),
        ],
        out_specs=pl.BlockSpec((tm, d), lambda i: (i, 0)),
        out_shape=jax.ShapeDtypeStruct((t, d), F32),
        compiler_params=_cparams(("parallel",)),
        name="mm_postadd",
    )(a, w, b.reshape(1, d), x, m_l, g1.reshape(1, d))


def _dft_tables(L):
    idx = np.arange(L, dtype=np.int64)
    ang = np.pi * ((idx[:, None] * idx[None, :]) % (2 * L)).astype(np.float64) / L
    c = np.cos(ang)
    s = np.sin(ang)
    alt = np.where(idx % 2 == 0, 1.0, -1.0)
    sp = s.copy()
    sp[0, :] = alt
    wf = np.full((L,), 2.0)
    wf[0] = 1.0
    ci = c * wf[None, :] / (2 * L)
    sip = -s * 2.0 / (2 * L)
    sip[:, 0] = alt / (2 * L)
    f32 = lambda a: jnp.asarray(a.astype(np.float32))
    return f32(c), f32(s), f32(sp), f32(ci), f32(sip)


def _hy_pe(L):
    t = np.arange(L, dtype=np.float64)
    periods = HY_PE_MIN_PERIOD * (HY_PE_MAX_PERIOD / HY_PE_MIN_PERIOD) ** (
        np.arange(HY_PE_BANDS, dtype=np.float64) / (HY_PE_BANDS - 1))
    ang = t[:, None] * (2.0 * math.pi / periods)[None]
    return jnp.asarray(np.concatenate([np.sin(ang), np.cos(ang)], axis=-1).astype(np.float32))


def _hy_filter_kernel(pe_ref, w1_ref, b1_ref, fr1_ref, w2_ref, b2_ref, fr2_ref, w3f_ref, w3b_ref,
                      la_ref, c_ref, s_ref, kre_ref, kim_ref):
    L, dc = kre_ref.shape
    h = jnp.sin(fr1_ref[...] * (jnp.dot(pe_ref[...], w1_ref[...], precision=HIGHEST,
                                        preferred_element_type=F32) + b1_ref[...]))
    h = jnp.sin(fr2_ref[...] * (jnp.dot(h, w2_ref[...], precision=HIGHEST,
                                        preferred_element_type=F32) + b2_ref[...]))
    row = lax.broadcasted_iota(jnp.int32, (L, dc), 0)
    dec = jnp.exp(-jnp.exp(la_ref[...]) * row.astype(F32))
    kf = jnp.dot(h, w3f_ref[...], precision=HIGHEST, preferred_element_type=F32) * dec
    kb = jnp.dot(h, w3b_ref[...], precision=HIGHEST, preferred_element_type=F32) * dec
    kb = jnp.where(row == 0, 0.0, kb)
    norm = jnp.sum(jnp.abs(kf) + jnp.abs(kb), axis=0, keepdims=True) + 1e-6
    inv = 1.0 / norm
    ks = (kf + kb) * inv
    kd = (kb - kf) * inv
    kre = jnp.dot(c_ref[...], ks, precision=HIGHEST, preferred_element_type=F32)
    kim = jnp.dot(s_ref[...], kd, precision=HIGHEST, preferred_element_type=F32)
    alt = jnp.where(row % 2 == 0, 1.0, -1.0)
    nyq = jnp.sum(alt * ks, axis=0, keepdims=True)
    kre_ref[...] = kre
    kim_ref[...] = jnp.where(row == 0, nyq, kim)


def _hy_filter(L, w1, b1, fr1, w2, b2, fr2, w3, log_alpha, c_mat, s_mat):
    d = log_alpha.shape[-1]
    fw = w1.shape[1]
    dc = 512
    nd = d // dc
    full = lambda a: pl.BlockSpec(a.shape, lambda j: (0,) * a.ndim)
    pe = _hy_pe(L)
    b1, fr1, b2, fr2 = (a.reshape(1, fw) for a in (b1, fr1, b2, fr2))
    osh = jax.ShapeDtypeStruct((L, d), F32)
    return pl.pallas_call(
        _hy_filter_kernel,
        grid=(nd,),
        in_specs=[full(pe), full(w1), full(b1), full(fr1), full(w2), full(b2), full(fr2),
                  pl.BlockSpec((fw, dc), lambda j: (0, j)),
                  pl.BlockSpec((fw, dc), lambda j: (0, nd + j)),
                  pl.BlockSpec((1, dc), lambda j: (0, j)),
                  full(c_mat), full(s_mat)],
        out_specs=[pl.BlockSpec((L, dc), lambda j: (0, j))] * 2,
        out_shape=[osh, osh],
        compiler_params=_cparams(("parallel",)),
        name="hyena_filter",
    )(pe, w1, b1, fr1, w2, b2, fr2, w3, w3, log_alpha.reshape(1, d), c_mat, s_mat)


def _hy_conv_kernel(x0_ref, x1_ref, v_ref, w0_ref, w1_ref, wv_ref, b0_ref, b1_ref, bv_ref,
                    kre_ref, kim_ref, skip_ref, c_ref, sp_ref, ci_ref, sip_ref, o_ref):
    L, dc = o_ref.shape
    row = lax.broadcasted_iota(jnp.int32, (L, dc), 0)

    def short_conv(z_ref, w_ref, b_ref):
        z = z_ref[...]
        zm = jnp.where(row == 0, 0.0, pltpu.roll(z, 1, 0))
        zp = jnp.where(row == L - 1, 0.0, pltpu.roll(z, L - 1, 0))
        return zm * w_ref[0:1, :] + z * w_ref[1:2, :] + zp * w_ref[2:3, :] + b_ref[...]

    x0 = short_conv(x0_ref, w0_ref, b0_ref)
    x1 = short_conv(x1_ref, w1_ref, b1_ref)
    v = short_conv(v_ref, wv_ref, bv_ref) * x1
    vb = v.astype(BF16)
    a = jnp.dot(c_ref[...], vb, preferred_element_type=F32)
    bm = jnp.dot(sp_ref[...], vb, preferred_element_type=F32)
    kre = kre_ref[...]
    kim = kim_ref[...]
    bk = bm * kim
    first = row == 0
    yre = a * kre + jnp.where(first, 0.0, bk)
    yim = jnp.where(first, bk, a * kim - bm * kre)
    y = (jnp.dot(ci_ref[...], yre.astype(BF16), preferred_element_type=F32)
         + jnp.dot(sip_ref[...], yim.astype(BF16), preferred_element_type=F32))
    o_ref[...] = (x0 * (y + v * skip_ref[...])).astype(BF16)


def _hy_conv(zx0, zx1, zv, w_sh, b_sh, kre, kim, skip, tables_bf16, *, seq_len):
    t, d = zx0.shape
    L = seq_len
    nb = t // L
    dc = 256
    nd = d // dc
    zspec = pl.BlockSpec((L, dc), lambda j, b: (b, j))
    wspec = lambda o: pl.BlockSpec((3, dc), lambda j, b: (0, o * nd + j))
    bspec = lambda o: pl.BlockSpec((1, dc), lambda j, b: (0, o * nd + j))
    kspec = pl.BlockSpec((L, dc), lambda j, b: (0, j))
    mspec = pl.BlockSpec((L, L), lambda j, b: (0, 0))
    b_sh = b_sh.reshape(1, 3 * d)
    return pl.pallas_call(
        _hy_conv_kernel,
        grid=(nd, nb),
        in_specs=[zspec, zspec, zspec, wspec(0), wspec(1), wspec(2), bspec(0), bspec(1), bspec(2),
                  kspec, kspec, pl.BlockSpec((1, dc), lambda j, b: (0, j)),
                  mspec, mspec, mspec, mspec],
        out_specs=pl.BlockSpec((L, dc), lambda j, b: (b, j)),
        out_shape=jax.ShapeDtypeStruct((t, d), BF16),
        compiler_params=_cparams(("parallel", "arbitrary")),
        name="hyena_conv",
    )(zx0, zx1, zv, w_sh, w_sh, w_sh, b_sh, b_sh, b_sh, kre, kim, skip.reshape(1, d), *tables_bf16)


def _rope_tables(L, head_dim):
    rows = L // GRID_W
    row = np.repeat(np.arange(rows), GRID_W).astype(np.float64)
    col = np.tile(np.arange(GRID_W), rows).astype(np.float64)
    half = head_dim // 2
    inv = ROPE_BASE ** (-np.arange(0, half, 2, dtype=np.float64) / half)
    ar = row[:, None] * inv
    ac = col[:, None] * inv
    ang = np.concatenate([ar, ar, ac, ac], axis=-1)
    return (jnp.asarray(np.cos(ang).astype(np.float32)), jnp.asarray(np.sin(ang).astype(np.float32)))


def _rope(x, cos, sin):
    hd = x.shape[-1]
    q = hd // 4
    lane = lax.broadcasted_iota(jnp.int32, x.shape, 1)
    rot = jnp.where((lane % (2 * q)) < q, -pltpu.roll(x, hd - q, 1), pltpu.roll(x, q, 1))
    return x * cos + rot * sin


def _diff_lambda(lam_ref, lam_init):
    lp = lam_ref[...]
    s01 = jnp.sum(lp[0:1, :] * lp[1:2, :], axis=-1, keepdims=True)
    s23 = jnp.sum(lp[2:3, :] * lp[3:4, :], axis=-1, keepdims=True)
    return jnp.exp(s01) - jnp.exp(s23) + lam_init


def _diff_attend_head(q2, k_segs, v_segs, lam, g_sub, out_scale, hd):
    scale = hd ** -0.5
    probs = []
    for comp in range(2):
        qc = (q2[comp] * scale).astype(BF16)
        ss = [lax.dot_general(qc, ks[comp].astype(BF16), (((1,), (1,)), ((), ())),
                              preferred_element_type=F32) for ks in k_segs]
        m = ss[0].max(axis=-1, keepdims=True)
        for s in ss[1:]:
            m = jnp.maximum(m, s.max(axis=-1, keepdims=True))
        es = [jnp.exp(s - m) for s in ss]
        den = es[0].sum(axis=-1, keepdims=True)
        for e in es[1:]:
            den = den + e.sum(axis=-1, keepdims=True)
        inv = 1.0 / den
        probs.append([e * inv for e in es])
    o = None
    for seg, v in enumerate(v_segs):
        w = (probs[0][seg] - lam * probs[1][seg]).astype(BF16)
        pv = jnp.dot(w, v.astype(BF16), preferred_element_type=F32)
        o = pv if o is None else o + pv
    return o * _rms(o) * (g_sub * out_scale)


def _attn_ctx_kernel(q_ref, k_ref, v_ref, lam_ref, g_ref, o_ref, *, lam_init, hd):
    lam = _diff_lambda(lam_ref, lam_init)
    vd = 2 * hd
    for h in range(N_HEADS):
        c0 = h * vd
        q2 = [q_ref[:, c0:c0 + hd], q_ref[:, c0 + hd:c0 + vd]]
        k2 = (k_ref[:, c0:c0 + hd], k_ref[:, c0 + hd:c0 + vd])
        o = _diff_attend_head(q2, [k2], [v_ref[:, c0:c0 + vd]], lam, g_ref[...], 1.0 - lam_init, hd)
        o_ref[:, c0:c0 + vd] = o.astype(BF16)


def _attn_ctx(q, k, v, at_lam, g_sub, *, seq_len, lam_init):
    t, d = q.shape
    hd = d // N_HEADS // 2
    spec = pl.BlockSpec((seq_len, d), lambda b: (b, 0))
    return pl.pallas_call(
        functools.partial(_attn_ctx_kernel, lam_init=lam_init, hd=hd),
        grid=(t // seq_len,),
        in_specs=[spec, spec, spec,
                  pl.BlockSpec(at_lam.shape, lambda b: (0, 0)),
                  pl.BlockSpec((1, 2 * hd), lambda b: (0, 0))],
        out_specs=spec,
        out_shape=jax.ShapeDtypeStruct((t, d), BF16),
        compiler_params=_cparams(("parallel",)),
        name="attn_ctx",
    )(q, k, v, at_lam, g_sub.reshape(1, 2 * hd))


def _attn_lat_kernel(q_ref, k_ref, v_ref, ck_ref, cv_ref, cq_ref, sq_ref, ckk_ref, skk_ref,
                     lam_ref, g_ref, o_ref, *, lam_init, hd):
    lam = _diff_lambda(lam_ref, lam_init)
    cq, sq = cq_ref[...], sq_ref[...]
    ckk, skk = ckk_ref[...], skk_ref[...]
    q2 = [_rope(q_ref[:, 0:hd], cq, sq), _rope(q_ref[:, hd:2 * hd], cq, sq)]
    kl = (_rope(k_ref[:, 0:hd], ckk, skk), _rope(k_ref[:, hd:2 * hd], ckk, skk))
    kc = (ck_ref[:, 0:hd], ck_ref[:, hd:2 * hd])
    o = _diff_attend_head(q2, [kc, kl], [cv_ref[...], v_ref[...]], lam, g_ref[...], 1.0 - lam_init, hd)
    o_ref[...] = o.astype(BF16)


def _attn_lat(q, k, v, cache_k, cache_v, at_lam, g_sub, *, seq_len, lam_init):
    t, d = q.shape
    hd = d // N_HEADS // 2
    vd = 2 * hd
    nb = t // seq_len
    past = cache_k.shape[1]
    qb = min(256, seq_len)
    nq = seq_len // qb
    cos, sin = _rope_tables(seq_len, hd)
    return pl.pallas_call(
        functools.partial(_attn_lat_kernel, lam_init=lam_init, hd=hd),
        grid=(nb, N_HEADS, nq),
        in_specs=[
            pl.BlockSpec((qb, vd), lambda b, h, i: (b * nq + i, h)),
            pl.BlockSpec((seq_len, vd), lambda b, h, i: (b, h)),
            pl.BlockSpec((seq_len, vd), lambda b, h, i: (b, h)),
            pl.BlockSpec((None, past, vd), lambda b, h, i: (b, 0, h)),
            pl.BlockSpec((None, past, vd), lambda b, h, i: (b, 0, h)),
            pl.BlockSpec((qb, hd), lambda b, h, i: (i, 0)),
            pl.BlockSpec((qb, hd), lambda b, h, i: (i, 0)),
            pl.BlockSpec((seq_len, hd), lambda b, h, i: (0, 0)),
            pl.BlockSpec((seq_len, hd), lambda b, h, i: (0, 0)),
            pl.BlockSpec(at_lam.shape, lambda b, h, i: (0, 0)),
            pl.BlockSpec((1, vd), lambda b, h, i: (0, 0)),
        ],
        out_specs=pl.BlockSpec((qb, vd), lambda b, h, i: (b * nq + i, h)),
        out_shape=jax.ShapeDtypeStruct((t, d), BF16),
        compiler_params=_cparams(("parallel", "parallel", "parallel")),
        name="attn_lat",
    )(q, k, v, cache_k, cache_v, cos, sin, cos, sin, at_lam, g_sub.reshape(1, vd))


def _s5_prep_kernel(lre_r, lim_r, lre_c, lim_c, ldt_ref, btre_ref, btim_ref, ctre_ref, ctim_ref,
                    wst_ref, t_ref, wout_ref, lam_ref):
    Q = S5_CHUNK
    H = S5_GROUP
    P2 = lre_r.shape[-1]
    P = P2 // 2
    N = Q * H
    lane2 = lax.broadcasted_iota(jnp.int32, (1, P2), 1)
    is_re = lane2 < P
    nrow = lax.broadcasted_iota(jnp.int32, (Q, P2), 0).astype(F32)
    tlane = (lax.broadcasted_iota(jnp.int32, (P, N), 1) // H)
    lane_n = lax.broadcasted_iota(jnp.int32, (2 * P, N), 1)

    t_acc = [None] * Q
    for di in range(2):
        re = jnp.minimum(lre_r[di], -1e-4)
        im = lim_r[di]
        dt = jnp.exp(ldt_ref[di])
        re_dt, im_dt = re * dt, im * dt
        mag = jnp.exp(re_dt)
        lb_re, lb_im = mag * jnp.cos(im_dt), mag * jnp.sin(im_dt)
        den = re * re + im * im
        q_re = ((lb_re - 1.0) * re + lb_im * im) / den
        q_im = (lb_im * re - (lb_re - 1.0) * im) / den
        bt_re, bt_im = btre_ref[di], btim_ref[di]
        bb_re = q_re * bt_re - q_im * bt_im
        bb_im = q_re * bt_im + q_im * bt_re
        n = (Q - 1.0) - nrow if di == 0 else nrow
        pmag = jnp.exp(n * re_dt)
        pw_re, pw_im = pmag * jnp.cos(n * im_dt), pmag * jnp.sin(n * im_dt)
        for s in range(Q):
            pr, pi = pw_re[s:s + 1, :], pw_im[s:s + 1, :]
            slab = jnp.where(is_re, bb_re * pr - bb_im * pi, bb_re * pi + bb_im * pr)
            wst_ref[s * H:(s + 1) * H, di * P2:(di + 1) * P2] = slab.astype(BF16)
        m16 = jnp.exp(Q * re_dt)
        lam_ref[2 * di:2 * di + 1, :] = m16 * jnp.cos(Q * im_dt)
        lam_ref[2 * di + 1:2 * di + 2, :] = jnp.where(is_re, -1.0, 1.0) * m16 * jnp.sin(Q * im_dt)

        rec = jnp.minimum(lre_c[di], -1e-4)
        imc = lim_c[di]
        rec_dt, imc_dt = rec * dt, imc * dt
        nt = (tlane if di == 0 else (Q - 1) - tlane).astype(F32)
        gmag = jnp.exp(nt * rec_dt)
        g_pre, g_pim = gmag * jnp.cos(nt * imc_dt), gmag * jnp.sin(nt * imc_dt)
        c_re, c_im = ctre_ref[di], ctim_ref[di]
        gx_re = c_re * g_pre - c_im * g_pim
        gx_im = c_re * g_pim + c_im * g_pre
        m1 = jnp.exp(rec_dt)
        l1_re, l1_im = m1 * jnp.cos(imc_dt), m1 * jnp.sin(imc_dt)
        wo_re = gx_re * l1_re - gx_im * l1_im
        wo_im = gx_re * l1_im + gx_im * l1_re
        wout_ref[di * P2:di * P2 + P, :] = wo_re.astype(BF16)
        wout_ref[di * P2 + P:(di + 1) * P2, :] = (-wo_im).astype(BF16)
        gx = jnp.concatenate([gx_re, gx_im], axis=0)
        lhs = jnp.where(is_re, bb_re, -bb_im)
        for s in range(Q):
            if di == 0:
                shift, keep = s * H, lane_n >= s * H
            else:
                shift, keep = ((s + 1) * H) % N, lane_n < (s + 1) * H
            g_s = gx if shift == 0 else pltpu.roll(gx, shift, 1)
            g_s = jnp.where(keep, g_s, 0.0)
            part = jnp.dot(lhs, g_s, precision=HIGHEST, preferred_element_type=F32)
            t_acc[s] = part if t_acc[s] is None else t_acc[s] + part
    for s in range(Q):
        t_ref[s * H:(s + 1) * H, :] = t_acc[s].astype(BF16)


def _s5_prep(lam_re, lam_im, log_dt, b_re, b_im, c_re, c_im):
    _, G, P = lam_re.shape
    H = S5_GROUP
    N = S5_CHUNK * H
    P2 = 2 * P
    dbl = lambda a: jnp.concatenate([a, a], axis=-1)
    lre_r = dbl(lam_re).reshape(2, G, 1, P2)
    lim_r = dbl(lam_im).reshape(2, G, 1, P2)
    lre_c = lam_re.reshape(2, G, P, 1)
    lim_c = lam_im.reshape(2, G, P, 1)
    ldt = log_dt.reshape(2, G, 1, 1)
    bt_re = dbl(jnp.swapaxes(b_re, -1, -2))
    bt_im = dbl(jnp.swapaxes(b_im, -1, -2))
    ct_re = jnp.tile(jnp.swapaxes(c_re, -1, -2), (1, 1, 1, S5_CHUNK))
    ct_im = jnp.tile(jnp.swapaxes(c_im, -1, -2), (1, 1, 1, S5_CHUNK))
    gspec = lambda a: pl.BlockSpec((2, None) + a.shape[2:], lambda g: (0, g, 0, 0))
    ins = (lre_r, lim_r, lre_c, lim_c, ldt, bt_re, bt_im, ct_re, ct_im)
    mspec = pl.BlockSpec((None, N, N), lambda g: (g, 0, 0))
    msh = jax.ShapeDtypeStruct((G, N, N), BF16)
    return pl.pallas_call(
        _s5_prep_kernel,
        grid=(G,),
        in_specs=[gspec(a) for a in ins],
        out_specs=[mspec, mspec, mspec, pl.BlockSpec((None, 4, P2), lambda g: (g, 0, 0))],
        out_shape=[msh, msh, msh, jax.ShapeDtypeStruct((G, 4, P2), F32)],
        compiler_params=_cparams(("parallel",)),
        name="s5_prep",
    )(*ins)


def _s5_core_kernel(u_ref, wst_ref, t_ref, wout_ref, lam_ref, s0_ref, y_ref, fin_ref, loc, sin,
                    *, nb, nc):
    u = u_ref[...]
    P2 = lam_ref.shape[-1]
    loc[...] = jnp.dot(u, wst_ref[...], preferred_element_type=F32)
    af, bf = lam_ref[0:1, :], lam_ref[1:2, :]
    ab, bb = lam_ref[2:3, :], lam_ref[3:4, :]

    def body(k, carry):
        fw, bw = carry
        rf = pl.multiple_of(k * nb, nb)
        rb = pl.multiple_of((nc - 1 - k) * nb, nb)
        sin[pl.ds(rf, nb), 0:P2] = fw
        sin[pl.ds(rb, nb), P2:2 * P2] = bw
        fw = af * fw + bf * pltpu.roll(fw, P2 // 2, 1) + loc[pl.ds(rf, nb), 0:P2]
        bw = ab * bw + bb * pltpu.roll(bw, P2 // 2, 1) + loc[pl.ds(rb, nb), P2:2 * P2]
        return fw, bw

    fw, bw = lax.fori_loop(0, nc, body, (s0_ref[:, 0:P2], s0_ref[:, P2:2 * P2]))
    fin_ref[:, 0:P2] = fw
    fin_ref[:, P2:2 * P2] = bw
    y_ref[...] = (jnp.dot(u, t_ref[...], preferred_element_type=F32)
                  + jnp.dot(sin[...].astype(BF16), wout_ref[...], preferred_element_type=F32))


def _s5_core(h, s0, wst, tmat, wout, lam, *, seq_len):
    t, d = h.shape
    H, Q = S5_GROUP, S5_CHUNK
    G = d // H
    nb = t // seq_len
    nc = seq_len // Q
    R = nb * nc
    N = Q * H
    P4 = s0.shape[-1]
    u = h.reshape(nb, nc, Q, G, H).transpose(3, 1, 0, 2, 4).reshape(G, R, N).astype(BF16)
    mspec = pl.BlockSpec((None, N, N), lambda g: (g, 0, 0))
    y, fin = pl.pallas_call(
        functools.partial(_s5_core_kernel, nb=nb, nc=nc),
        grid=(G,),
        in_specs=[pl.BlockSpec((None, R, N), lambda g: (g, 0, 0)), mspec, mspec, mspec,
                  pl.BlockSpec((None, 4, P4 // 2), lambda g: (g, 0, 0)),
                  pl.BlockSpec((None, nb, P4), lambda g: (g, 0, 0))],
        out_specs=[pl.BlockSpec((None, R, N), lambda g: (g, 0, 0)),
                   pl.BlockSpec((None, nb, P4), lambda g: (g, 0, 0))],
        out_shape=[jax.ShapeDtypeStruct((G, R, N), F32), jax.ShapeDtypeStruct((G, nb, P4), F32)],
        scratch_shapes=[pltpu.VMEM((R, N), F32), pltpu.VMEM((R, P4), F32)],
        compiler_params=_cparams(("parallel",)),
        name="s5_core",
    )(u, wst, tmat, wout, lam, s0)
    y = y.reshape(G, nc, nb, Q, H).transpose(2, 1, 3, 0, 4).reshape(t, d)
    return y, fin


def _glu_postadd_kernel(h_ref, y_ref, d_ref, wa_ref, wg_ref, ba_ref, bg_ref, x_ref, gate_ref, g_ref,
                        o_ref, acc_a, acc_g):
    k = pl.program_id(1)

    @pl.when(k == 0)
    def _():
        acc_a[...] = jnp.zeros_like(acc_a)
        acc_g[...] = jnp.zeros_like(acc_g)

    u = jax.nn.gelu(d_ref[...] * h_ref[...] + y_ref[...]).astype(BF16)
    acc_a[...] += jnp.dot(u, wa_ref[...], preferred_element_type=F32)
    acc_g[...] += jnp.dot(u, wg_ref[...], preferred_element_type=F32)

    @pl.when(k == pl.num_programs(1) - 1)
    def _():
        o = (acc_a[...] + ba_ref[...]) * jax.nn.sigmoid(acc_g[...] + bg_ref[...])
        o_ref[...] = _post_add(x_ref[...], o, g_ref[...], gate_ref[...])


def _glu_postadd(h, y, dskip, w_glu, b_glu, x, m_l, g1, *, seq_len, latent):
    t, d = x.shape
    tm = min(512, t)
    tk = 512
    rowfn = _row_fn(tm, seq_len, latent)
    b_glu = b_glu.reshape(1, 2 * d)
    return pl.pallas_call(
        _glu_postadd_kernel,
        grid=(t // tm, d // tk),
        in_specs=[
            pl.BlockSpec((tm, tk), lambda i, k: (i, k)),
            pl.BlockSpec((tm, tk), lambda i, k: (i, k)),
            pl.BlockSpec((1, tk), lambda i, k: (0, k)),
            pl.BlockSpec((tk, d), lambda i, k: (k, 0)),
            pl.BlockSpec((tk, d), lambda i, k: (k, 1)),
            pl.BlockSpec((1, d), lambda i, k: (0, 0)),
            pl.BlockSpec((1, d), lambda i, k: (0, 1)),
            pl.BlockSpec((tm, d), lambda i, k: (i, 0)),
            _mod_spec(d, 2, rowfn),
            pl.BlockSpec((1, d), lambda i, k: (0, 0)),
        ],
        out_specs=pl.BlockSpec((tm, d), lambda i, k: (i, 0)),
        out_shape=jax.ShapeDtypeStruct((t, d), F32),
        scratch_shapes=[pltpu.VMEM((tm, d), F32), pltpu.VMEM((tm, d), F32)],
        compiler_params=_cparams(("parallel", "arbitrary")),
        name="glu_postadd",
    )(h, y, dskip.reshape(1, d), w_glu, w_glu, b_glu, b_glu, x, m_l, g1.reshape(1, d))


def kernel(x_prompt, x_sample, cache_attn_k, cache_attn_v, state_s5_re, state_s5_im, c, c_ctx, w_mod, b_mod, g_norm, w_mlp_in, w_mlp_out, hy_w_in, hy_b_in, hy_w_short, hy_b_short, hy_f_w1, hy_f_b1, hy_f_freq1, hy_f_w2, hy_f_b2, hy_f_freq2, hy_f_w3, hy_log_alpha, hy_skip, hy_w_out, hy_b_out, at_w_qkv, at_lam, at_g_sub, at_w_o, s5_lam_re, s5_lam_im, s5_log_dt, s5_b_re, s5_b_im, s5_c_re, s5_c_im, s5_d, s5_w_glu, s5_b_glu):
    bc, lc, d = x_prompt.shape
    bl, ll, _ = x_sample.shape
    depth = w_mod.shape[0]
    assert 1 + bl <= MOD_ROWS
    hd = d // N_HEADS // 2
    G = d // S5_GROUP
    P = s5_lam_re.shape[-1]

    cond = jnp.concatenate([c_ctx[None], c, jnp.zeros((MOD_ROWS - 1 - bl, d), F32)], axis=0)
    mod = _modulation(cond, w_mod, b_mod).reshape(depth, MOD_ROWS, N_MOD, 1, d)

    xc = x_prompt.reshape(bc * lc, d)
    xl = x_sample.reshape(bl * ll, d)
    streams = [dict(seq_len=lc, latent=False), dict(seq_len=ll, latent=True)]
    xs = [xc, xl]

    tables = {}
    for L in {lc, ll}:
        cm, sm, sp, ci, sip = _dft_tables(L)
        tables[L] = (cm, sm, tuple(a.astype(BF16) for a in (cm, sp, ci, sip)))

    new_k = new_v = None
    fin_ctx = None
    for i in range(depth):
        kind, j = i % 3, i // 3
        m_l = mod[i]
        g = g_norm[i]
        if kind == 0:
            w_in = hy_w_in[j].astype(BF16)
            w_out = hy_w_out[j].astype(BF16)
            filt = {}
            for L in {lc, ll}:
                filt[L] = _hy_filter(L, hy_f_w1[j], hy_f_b1[j], hy_f_freq1[j], hy_f_w2[j], hy_f_b2[j],
                                     hy_f_freq2[j], hy_f_w3[j], hy_log_alpha[j], tables[L][0], tables[L][1])
            for si, st in enumerate(streams):
                L = st["seq_len"]
                z0, z1, zv = _premod_mm3(xs[si], m_l, g[0], w_in, hy_b_in[j].reshape(1, 3 * d),
                                         shift_idx=0, scale_idx=1, **st)
                a = _hy_conv(z0, z1, zv, hy_w_short[j], hy_b_short[j], filt[L][0], filt[L][1],
                             hy_skip[j], tables[L][2], seq_len=L)
                xs[si] = _mm_postadd(a, w_out, hy_b_out[j], xs[si], m_l, g[1], **st)
        elif kind == 1:
            lam_init = 0.8 - 0.6 * math.exp(-0.3 * i)
            w_qkv = at_w_qkv[j].astype(BF16)
            w_o = at_w_o[j].astype(BF16)
            zero_b = jnp.zeros((1, 3 * d), F32)
            for si, st in enumerate(streams):
                L = st["seq_len"]
                q, k, v = _premod_mm3(xs[si], m_l, g[0], w_qkv, zero_b, shift_idx=0, scale_idx=1, **st)
                if not st["latent"]:
                    new_k = k.reshape(bc, 1, lc, N_HEADS, 2, hd)
                    new_v = v.reshape(bc, 1, lc, N_HEADS, 2 * hd)
                    a = _attn_ctx(q, k, v, at_lam[j], at_g_sub[j], seq_len=L, lam_init=lam_init)
                else:
                    ck = cache_attn_k[:, j].reshape(bl, -1, d)
                    cv = cache_attn_v[:, j].reshape(bl, -1, d)
                    a = _attn_lat(q, k, v, ck, cv, at_lam[j], at_g_sub[j], seq_len=L, lam_init=lam_init)
                xs[si] = _mm_postadd(a, w_o, jnp.zeros((d,), F32), xs[si], m_l, g[1], **st)
        else:
            wst, tmat, wout, lam16 = _s5_prep(s5_lam_re[j], s5_lam_im[j], s5_log_dt[j], s5_b_re[j],
                                              s5_b_im[j], s5_c_re[j], s5_c_im[j])
            w_glu = s5_w_glu[j].astype(BF16)
            for si, st in enumerate(streams):
                L = st["seq_len"]
                nb = xs[si].shape[0] // L
                if st["latent"]:
                    sre, sim = state_s5_re[:, j], state_s5_im[:, j]
                    s0 = jnp.stack([sre, sim], axis=2).transpose(3, 0, 1, 2, 4).reshape(G, nb, 4 * P)
                else:
                    s0 = jnp.zeros((G, nb, 4 * P), F32)
                h = _premod_only(xs[si], m_l, g[0], **st)
                y, fin = _s5_core(h, s0, wst, tmat, wout, lam16, seq_len=L)
                if not st["latent"]:
                    fin_ctx = fin.reshape(G, nb, 2, 2, P).transpose(1, 2, 3, 0, 4)
                xs[si] = _glu_postadd(h, y, s5_d[j], w_glu, s5_b_glu[j], xs[si], m_l, g[1], **st)
        w1 = w_mlp_in[i].astype(BF16)
        w2 = w_mlp_out[i].astype(BF16)
        for si, st in enumerate(streams):
            xs[si] = _mlp(xs[si], m_l, g[2], g[3], w1, w2, **st)

    new_s_re = fin_ctx[:, None, :, 0]
    new_s_im = fin_ctx[:, None, :, 1]
    return (xs[0].reshape(bc, lc, d), xs[1].reshape(bl, ll, d), new_k, new_v, new_s_re, new_s_im)
```

```python
import functools
import math

import numpy as np
import jax
import jax.numpy as jnp
from jax import lax
from jax.experimental import pallas as pl
from jax.experimental.pallas import tpu as pltpu

F32 = jnp.float32
BF16 = jnp.bfloat16
HIGHEST = lax.Precision.HIGHEST

NORM_EPS = 1e-6
N_MOD = 6
N_HEADS = 8
GRID_W = 64
ROPE_BASE = 10000.0
HY_PE_BANDS = 16
HY_PE_MIN_PERIOD = 2.0
HY_PE_MAX_PERIOD = 4096.0
S5_GROUP = 16
S5_CHUNK = 16
LANES = 128
S5_GROUPS_PER_STEP = LANES // S5_GROUP
MOD_ROWS = 16

VMEM_LIMIT = 56 * 1024 * 1024


def _cparams(sem):
    return pltpu.CompilerParams(dimension_semantics=sem, vmem_limit_bytes=VMEM_LIMIT)


def _rms(x):
    return lax.rsqrt(jnp.mean(x * x, axis=-1, keepdims=True) + NORM_EPS)


def _mod_kernel(c_ref, w_ref, b_ref, o_ref):
    c = c_ref[...]
    s = c * jax.nn.sigmoid(c)
    o_ref[...] = jnp.dot(s, w_ref[...], precision=HIGHEST, preferred_element_type=F32) + b_ref[...]


def _modulation(cond, w_mod, b_mod):
    depth, d, n = w_mod.shape
    tn = 1024
    return pl.pallas_call(
        _mod_kernel,
        grid=(depth, n // tn),
        in_specs=[
            pl.BlockSpec((MOD_ROWS, d), lambda l, j: (0, 0)),
            pl.BlockSpec((None, d, tn), lambda l, j: (l, 0, j)),
            pl.BlockSpec((None, 1, tn), lambda l, j: (l, 0, j)),
        ],
        out_specs=pl.BlockSpec((None, MOD_ROWS, tn), lambda l, j: (l, 0, j)),
        out_shape=jax.ShapeDtypeStruct((depth, MOD_ROWS, n), F32),
        compiler_params=_cparams(("parallel", "parallel")),
        name="adaln_mod",
    )(cond, w_mod, b_mod.reshape(depth, 1, n))


def _mod_spec(d, which, rowfn):
    return pl.BlockSpec((None, None, 1, d), lambda i, *_: (rowfn(i), which, 0, 0))


def _row_fn(tm, seq_len, latent):
    if latent:
        assert seq_len % tm == 0, "a latent row tile must sit inside one sequence"
        return lambda i: 1 + (i * tm) // seq_len
    return lambda i: 0


def _premod(x, g, sh, sc):
    return (x * _rms(x) * g) * (1.0 + sc) + sh


def _premod_mm3_kernel(x_ref, g_ref, sh_ref, sc_ref, wa_ref, wb_ref, wc_ref, ba_ref, bb_ref, bc_ref,
                       oa_ref, ob_ref, oc_ref, h_scr):
    @pl.when(pl.program_id(1) == 0)
    def _():
        h_scr[...] = _premod(x_ref[...], g_ref[...], sh_ref[...], sc_ref[...]).astype(BF16)

    h = h_scr[...]
    oa_ref[...] = jnp.dot(h, wa_ref[...], preferred_element_type=F32) + ba_ref[...]
    ob_ref[...] = jnp.dot(h, wb_ref[...], preferred_element_type=F32) + bb_ref[...]
    oc_ref[...] = jnp.dot(h, wc_ref[...], preferred_element_type=F32) + bc_ref[...]


def _premod_mm3(x, m_l, g, w, layer, b, *, seq_len, latent):
    t, d = x.shape
    tm = min(1024, seq_len if latent else t)
    tn = 512
    nj = d // tn
    rowfn = _row_fn(tm, seq_len, latent)
    wspec = lambda o: pl.BlockSpec((None, d, tn), lambda i, j: (layer, 0, o * nj + j))
    bspec = lambda o: pl.BlockSpec((1, tn), lambda i, j: (0, o * nj + j))
    ospec = pl.BlockSpec((tm, tn), lambda i, j: (i, j))
    osh = jax.ShapeDtypeStruct((t, d), F32)
    return pl.pallas_call(
        _premod_mm3_kernel,
        grid=(t // tm, nj),
        in_specs=[
            pl.BlockSpec((tm, d), lambda i, j: (i, 0)),
            pl.BlockSpec((1, d), lambda i, j: (0, 0)),
            _mod_spec(d, 0, rowfn),
            _mod_spec(d, 1, rowfn),
            wspec(0), wspec(1), wspec(2), bspec(0), bspec(1), bspec(2),
        ],
        out_specs=[ospec, ospec, ospec],
        out_shape=[osh, osh, osh],
        scratch_shapes=[pltpu.VMEM((tm, d), BF16)],
        compiler_params=_cparams(("parallel", "arbitrary")),
        name="premod_mm3",
    )(x, g.reshape(1, d), m_l, m_l, w, w, w, b, b, b)


def _premod_kernel(x_ref, g_ref, sh_ref, sc_ref, o_ref):
    o_ref[...] = _premod(x_ref[...], g_ref[...], sh_ref[...], sc_ref[...])


def _premod_time_major(x, m_l, g, *, seq_len, latent):
    t, d = x.shape
    nb = t // seq_len
    tm = min(512, seq_len)
    per_seq = seq_len // tm
    rowfn = _row_fn(tm, seq_len, latent)
    return pl.pallas_call(
        _premod_kernel,
        grid=(t // tm,),
        in_specs=[
            pl.BlockSpec((tm, d), lambda i: (i, 0)),
            pl.BlockSpec((1, d), lambda i: (0, 0)),
            _mod_spec(d, 0, rowfn),
            _mod_spec(d, 1, rowfn),
        ],
        out_specs=pl.BlockSpec((tm, d), lambda i: (i % per_seq, i // per_seq)),
        out_shape=jax.ShapeDtypeStruct((seq_len, nb * d), F32),
        compiler_params=_cparams(("parallel",)),
        name="premod",
    )(x, g.reshape(1, d), m_l, m_l)


def _post_add(x, o, g, gate):
    return x + gate * (o * _rms(o) * g)


def _mlp_kernel(x_ref, g2_ref, sh_ref, sc_ref, gate_ref, g3_ref, w1_ref, w2_ref, o_ref, h_scr, acc):
    f = pl.program_id(1)

    @pl.when(f == 0)
    def _():
        h_scr[...] = _premod(x_ref[...], g2_ref[...], sh_ref[...], sc_ref[...]).astype(BF16)
        acc[...] = jnp.zeros_like(acc)

    a = jnp.dot(h_scr[...], w1_ref[...], preferred_element_type=F32)
    a = jnp.square(jnp.maximum(a, 0.0)).astype(BF16)
    acc[...] += jnp.dot(a, w2_ref[...], preferred_element_type=F32)

    @pl.when(f == pl.num_programs(1) - 1)
    def _():
        o_ref[...] = _post_add(x_ref[...], acc[...], g3_ref[...], gate_ref[...])


def _mlp(x, m_l, g2, g3, w1, w2, layer, *, seq_len, latent):
    t, d = x.shape
    dff = w1.shape[-1]
    tm = min(512, t)
    tf = 1024
    rowfn = _row_fn(tm, seq_len, latent)
    return pl.pallas_call(
        _mlp_kernel,
        grid=(t // tm, dff // tf),
        in_specs=[
            pl.BlockSpec((tm, d), lambda i, f: (i, 0)),
            pl.BlockSpec((1, d), lambda i, f: (0, 0)),
            _mod_spec(d, 3, rowfn),
            _mod_spec(d, 4, rowfn),
            _mod_spec(d, 5, rowfn),
            pl.BlockSpec((1, d), lambda i, f: (0, 0)),
            pl.BlockSpec((None, d, tf), lambda i, f: (layer, 0, f)),
            pl.BlockSpec((None, tf, d), lambda i, f: (layer, f, 0)),
        ],
        out_specs=pl.BlockSpec((tm, d), lambda i, f: (i, 0)),
        out_shape=jax.ShapeDtypeStruct((t, d), F32),
        scratch_shapes=[pltpu.VMEM((tm, d), BF16), pltpu.VMEM((tm, d), F32)],
        compiler_params=_cparams(("parallel", "arbitrary")),
        name="mlp",
    )(x, g2.reshape(1, d), m_l, m_l, m_l, g3.reshape(1, d), w1, w2)


def _mm_postadd_kernel(a_ref, w_ref, b_ref, x_ref, gate_ref, g_ref, o_ref):
    o = jnp.dot(a_ref[...], w_ref[...], preferred_element_type=F32) + b_ref[...]
    o_ref[...] = _post_add(x_ref[...], o, g_ref[...], gate_ref[...])


def _mm_postadd(a, w, layer, b, x, m_l, g1, *, seq_len, latent):
    t, d = x.shape
    tm = min(512, t)
    rowfn = _row_fn(tm, seq_len, latent)
    return pl.pallas_call(
        _mm_postadd_kernel,
        grid=(t // tm,),
        in_specs=[
            pl.BlockSpec((tm, d), lambda i: (i, 0)),
            pl.BlockSpec((None, d, d), lambda i: (layer, 0, 0)),
            pl.BlockSpec((1, d), lambda i: (0, 0)),
            pl.BlockSpec((tm, d), lambda i: (i, 0)),
            _mod_spec(d, 2, rowfn),
            pl.BlockSpec((1, d), lambda i: (0, 0)),
        ],
        out_specs=pl.BlockSpec((tm, d), lambda i: (i, 0)),
        out_shape=jax.ShapeDtypeStruct((t, d), F32),
        compiler_params=_cparams(("parallel",)),
        name="mm_postadd",
    )(a, w, b.reshape(1, d), x, m_l, g1.reshape(1, d))


def _dft_tables(L):
    idx = np.arange(L, dtype=np.int64)
    ang = np.pi * ((idx[:, None] * idx[None, :]) % (2 * L)).astype(np.float64) / L
    c = np.cos(ang)
    s = np.sin(ang)
    alt = np.where(idx % 2 == 0, 1.0, -1.0)
    sp = s.copy()
    sp[0, :] = alt
    wf = np.full((L,), 2.0)
    wf[0] = 1.0
    ci = c * wf[None, :] / (2 * L)
    sip = -s * 2.0 / (2 * L)
    sip[:, 0] = alt / (2 * L)
    f32 = lambda a: jnp.asarray(a.astype(np.float32))
    return f32(c), f32(s), f32(sp), f32(ci), f32(sip)


def _hy_pe(L):
    t = np.arange(L, dtype=np.float64)
    periods = HY_PE_MIN_PERIOD * (HY_PE_MAX_PERIOD / HY_PE_MIN_PERIOD) ** (
        np.arange(HY_PE_BANDS, dtype=np.float64) / (HY_PE_BANDS - 1))
    ang = t[:, None] * (2.0 * math.pi / periods)[None]
    return jnp.asarray(np.concatenate([np.sin(ang), np.cos(ang)], axis=-1).astype(np.float32))


def _hy_filter_kernel(pe_ref, w1_ref, b1_ref, fr1_ref, w2_ref, b2_ref, fr2_ref, w3f_ref, w3b_ref,
                      la_ref, c_ref, s_ref, kre_ref, kim_ref):
    L, dc = kre_ref.shape
    h = jnp.sin(fr1_ref[...] * (jnp.dot(pe_ref[...], w1_ref[...], precision=HIGHEST,
                                        preferred_element_type=F32) + b1_ref[...]))
    h = jnp.sin(fr2_ref[...] * (jnp.dot(h, w2_ref[...], precision=HIGHEST,
                                        preferred_element_type=F32) + b2_ref[...]))
    row = lax.broadcasted_iota(jnp.int32, (L, dc), 0)
    dec = jnp.exp(-jnp.exp(la_ref[...]) * row.astype(F32))
    kf = jnp.dot(h, w3f_ref[...], precision=HIGHEST, preferred_element_type=F32) * dec
    kb = jnp.dot(h, w3b_ref[...], precision=HIGHEST, preferred_element_type=F32) * dec
    kb = jnp.where(row == 0, 0.0, kb)
    norm = jnp.sum(jnp.abs(kf) + jnp.abs(kb), axis=0, keepdims=True) + 1e-6
    inv = 1.0 / norm
    ks = (kf + kb) * inv
    kd = (kb - kf) * inv
    kre = jnp.dot(c_ref[...], ks, precision=HIGHEST, preferred_element_type=F32)
    kim = jnp.dot(s_ref[...], kd, precision=HIGHEST, preferred_element_type=F32)
    alt = jnp.where(row % 2 == 0, 1.0, -1.0)
    nyq = jnp.sum(alt * ks, axis=0, keepdims=True)
    kre_ref[...] = kre
    kim_ref[...] = jnp.where(row == 0, nyq, kim)


def _hy_filter(L, w1, b1, fr1, w2, b2, fr2, w3, log_alpha, c_mat, s_mat):
    d = log_alpha.shape[-1]
    fw = w1.shape[1]
    dc = 512
    nd = d // dc
    full = lambda a: pl.BlockSpec(a.shape, lambda j: (0,) * a.ndim)
    pe = _hy_pe(L)
    b1, fr1, b2, fr2 = (a.reshape(1, fw) for a in (b1, fr1, b2, fr2))
    osh = jax.ShapeDtypeStruct((L, d), F32)
    return pl.pallas_call(
        _hy_filter_kernel,
        grid=(nd,),
        in_specs=[full(pe), full(w1), full(b1), full(fr1), full(w2), full(b2), full(fr2),
                  pl.BlockSpec((fw, dc), lambda j: (0, j)),
                  pl.BlockSpec((fw, dc), lambda j: (0, nd + j)),
                  pl.BlockSpec((1, dc), lambda j: (0, j)),
                  full(c_mat), full(s_mat)],
        out_specs=[pl.BlockSpec((L, dc), lambda j: (0, j))] * 2,
        out_shape=[osh, osh],
        compiler_params=_cparams(("parallel",)),
        name="hyena_filter",
    )(pe, w1, b1, fr1, w2, b2, fr2, w3, w3, log_alpha.reshape(1, d), c_mat, s_mat)


def _hy_conv_kernel(x0_ref, x1_ref, v_ref, w0_ref, w1_ref, wv_ref, b0_ref, b1_ref, bv_ref,
                    kre_ref, kim_ref, skip_ref, c_ref, sp_ref, ci_ref, sip_ref, o_ref):
    L, dc = o_ref.shape
    row = lax.broadcasted_iota(jnp.int32, (L, dc), 0)

    def short_conv(z_ref, w_ref, b_ref):
        z = z_ref[...]
        zm = jnp.where(row == 0, 0.0, pltpu.roll(z, 1, 0))
        zp = jnp.where(row == L - 1, 0.0, pltpu.roll(z, L - 1, 0))
        return zm * w_ref[0:1, :] + z * w_ref[1:2, :] + zp * w_ref[2:3, :] + b_ref[...]

    x0 = short_conv(x0_ref, w0_ref, b0_ref)
    x1 = short_conv(x1_ref, w1_ref, b1_ref)
    v = short_conv(v_ref, wv_ref, bv_ref) * x1
    vb = v.astype(BF16)
    a = jnp.dot(c_ref[...], vb, preferred_element_type=F32)
    bm = jnp.dot(sp_ref[...], vb, preferred_element_type=F32)
    kre = kre_ref[...]
    kim = kim_ref[...]
    bk = bm * kim
    first = row == 0
    yre = a * kre + jnp.where(first, 0.0, bk)
    yim = jnp.where(first, bk, a * kim - bm * kre)
    y = (jnp.dot(ci_ref[...], yre.astype(BF16), preferred_element_type=F32)
         + jnp.dot(sip_ref[...], yim.astype(BF16), preferred_element_type=F32))
    o_ref[...] = (x0 * (y + v * skip_ref[...])).astype(BF16)


def _hy_conv(zx0, zx1, zv, w_sh, b_sh, kre, kim, skip, tables_bf16, *, seq_len):
    t, d = zx0.shape
    L = seq_len
    nb = t // L
    dc = 512 if L <= 512 else 256
    nd = d // dc
    zspec = pl.BlockSpec((L, dc), lambda j, b: (b, j))
    wspec = lambda o: pl.BlockSpec((3, dc), lambda j, b: (0, o * nd + j))
    bspec = lambda o: pl.BlockSpec((1, dc), lambda j, b: (0, o * nd + j))
    kspec = pl.BlockSpec((L, dc), lambda j, b: (0, j))
    mspec = pl.BlockSpec((L, L), lambda j, b: (0, 0))
    b_sh = b_sh.reshape(1, 3 * d)
    return pl.pallas_call(
        _hy_conv_kernel,
        grid=(nd, nb),
        in_specs=[zspec, zspec, zspec, wspec(0), wspec(1), wspec(2), bspec(0), bspec(1), bspec(2),
                  kspec, kspec, pl.BlockSpec((1, dc), lambda j, b: (0, j)),
                  mspec, mspec, mspec, mspec],
        out_specs=pl.BlockSpec((L, dc), lambda j, b: (b, j)),
        out_shape=jax.ShapeDtypeStruct((t, d), BF16),
        compiler_params=_cparams(("parallel", "arbitrary")),
        name="hyena_conv",
    )(zx0, zx1, zv, w_sh, w_sh, w_sh, b_sh, b_sh, b_sh, kre, kim, skip.reshape(1, d), *tables_bf16)


def _rope_tables(L, head_dim):
    rows = L // GRID_W
    row = np.repeat(np.arange(rows), GRID_W).astype(np.float64)
    col = np.tile(np.arange(GRID_W), rows).astype(np.float64)
    half = head_dim // 2
    inv = ROPE_BASE ** (-np.arange(0, half, 2, dtype=np.float64) / half)
    ar = row[:, None] * inv
    ac = col[:, None] * inv
    ang = np.concatenate([ar, ar, ac, ac], axis=-1)
    return (jnp.asarray(np.cos(ang).astype(np.float32)), jnp.asarray(np.sin(ang).astype(np.float32)))


def _rope(x, cos, sin):
    hd = x.shape[-1]
    q = hd // 4
    lane = lax.broadcasted_iota(jnp.int32, x.shape, 1)
    rot = jnp.where((lane % (2 * q)) < q, -pltpu.roll(x, hd - q, 1), pltpu.roll(x, q, 1))
    return x * cos + rot * sin


def _diff_lambda(lam_ref, lam_init):
    lp = lam_ref[...]
    s01 = jnp.sum(lp[0:1, :] * lp[1:2, :], axis=-1, keepdims=True)
    s23 = jnp.sum(lp[2:3, :] * lp[3:4, :], axis=-1, keepdims=True)
    return jnp.exp(s01) - jnp.exp(s23) + lam_init


def _diff_attend_head(q2, k_segs, v_segs, lam, g_sub, out_scale, hd):
    scale = hd ** -0.5
    outs = []
    for comp in range(2):
        qc = (q2[comp] * scale).astype(BF16)
        ss = [lax.dot_general(qc, ks[comp], (((1,), (1,)), ((), ())), preferred_element_type=F32)
              for ks in k_segs]
        m = ss[0].max(axis=-1, keepdims=True)
        for s in ss[1:]:
            m = jnp.maximum(m, s.max(axis=-1, keepdims=True))
        den = None
        pv = None
        for s, v in zip(ss, v_segs):
            e = jnp.exp(s - m)
            dsum = e.sum(axis=-1, keepdims=True)
            den = dsum if den is None else den + dsum
            part = jnp.dot(e.astype(BF16), v, preferred_element_type=F32)
            pv = part if pv is None else pv + part
        outs.append(pv * (1.0 / den))
    o = outs[0] - lam * outs[1]
    return o * _rms(o) * (g_sub * out_scale)


def _attn_ctx_kernel(q_ref, k_ref, v_ref, lam_ref, g_ref, o_ref, *, lam_init, hd):
    lam = _diff_lambda(lam_ref, lam_init)
    vd = 2 * hd
    for h in range(N_HEADS):
        c0 = h * vd
        q2 = [q_ref[:, c0:c0 + hd], q_ref[:, c0 + hd:c0 + vd]]
        k2 = (k_ref[:, c0:c0 + hd].astype(BF16), k_ref[:, c0 + hd:c0 + vd].astype(BF16))
        v = v_ref[:, c0:c0 + vd].astype(BF16)
        o = _diff_attend_head(q2, [k2], [v], lam, g_ref[...], 1.0 - lam_init, hd)
        o_ref[:, c0:c0 + vd] = o.astype(BF16)


def _attn_ctx(q, k, v, at_lam, g_sub, *, seq_len, lam_init):
    t, d = q.shape
    hd = d // N_HEADS // 2
    spec = pl.BlockSpec((seq_len, d), lambda b: (b, 0))
    return pl.pallas_call(
        functools.partial(_attn_ctx_kernel, lam_init=lam_init, hd=hd),
        grid=(t // seq_len,),
        in_specs=[spec, spec, spec,
                  pl.BlockSpec(at_lam.shape, lambda b: (0, 0)),
                  pl.BlockSpec((1, 2 * hd), lambda b: (0, 0))],
        out_specs=spec,
        out_shape=jax.ShapeDtypeStruct((t, d), BF16),
        compiler_params=_cparams(("parallel",)),
        name="attn_ctx",
    )(q, k, v, at_lam, g_sub.reshape(1, 2 * hd))


def _attn_lat_kernel(q_ref, k_ref, v_ref, ck_ref, cv_ref, cq_ref, sq_ref, ckk_ref, skk_ref,
                     lam_ref, g_ref, o_ref, kl_scr, kc_scr, vl_scr, vc_scr, *, lam_init, hd, sub):
    @pl.when(pl.program_id(2) == 0)
    def _():
        ckk, skk = ckk_ref[...], skk_ref[...]
        kl_scr[:, 0:hd] = _rope(k_ref[:, 0:hd], ckk, skk).astype(BF16)
        kl_scr[:, hd:2 * hd] = _rope(k_ref[:, hd:2 * hd], ckk, skk).astype(BF16)
        kc_scr[...] = ck_ref[...].astype(BF16)
        vl_scr[...] = v_ref[...].astype(BF16)
        vc_scr[...] = cv_ref[...].astype(BF16)

    lam = _diff_lambda(lam_ref, lam_init)
    kl = (kl_scr[:, 0:hd], kl_scr[:, hd:2 * hd])
    kc = (kc_scr[:, 0:hd], kc_scr[:, hd:2 * hd])
    for r0 in range(0, q_ref.shape[0], sub):
        rows = slice(r0, r0 + sub)
        cq, sq = cq_ref[rows, :], sq_ref[rows, :]
        q2 = [_rope(q_ref[rows, 0:hd], cq, sq), _rope(q_ref[rows, hd:2 * hd], cq, sq)]
        o = _diff_attend_head(q2, [kc, kl], [vc_scr[...], vl_scr[...]], lam, g_ref[...],
                              1.0 - lam_init, hd)
        o_ref[rows, :] = o.astype(BF16)


def _attn_lat(q, k, v, cache_k, cache_v, at_lam, g_sub, *, seq_len, lam_init):
    t, d = q.shape
    hd = d // N_HEADS // 2
    vd = 2 * hd
    nb = t // seq_len
    past = cache_k.shape[1]
    qb = min(512, seq_len)
    nq = seq_len // qb
    cos, sin = _rope_tables(seq_len, hd)
    return pl.pallas_call(
        functools.partial(_attn_lat_kernel, lam_init=lam_init, hd=hd, sub=min(256, qb)),
        grid=(nb, N_HEADS, nq),
        in_specs=[
            pl.BlockSpec((qb, vd), lambda b, h, i: (b * nq + i, h)),
            pl.BlockSpec((seq_len, vd), lambda b, h, i: (b, h)),
            pl.BlockSpec((seq_len, vd), lambda b, h, i: (b, h)),
            pl.BlockSpec((None, past, vd), lambda b, h, i: (b, 0, h)),
            pl.BlockSpec((None, past, vd), lambda b, h, i: (b, 0, h)),
            pl.BlockSpec((qb, hd), lambda b, h, i: (i, 0)),
            pl.BlockSpec((qb, hd), lambda b, h, i: (i, 0)),
            pl.BlockSpec((seq_len, hd), lambda b, h, i: (0, 0)),
            pl.BlockSpec((seq_len, hd), lambda b, h, i: (0, 0)),
            pl.BlockSpec(at_lam.shape, lambda b, h, i: (0, 0)),
            pl.BlockSpec((1, vd), lambda b, h, i: (0, 0)),
        ],
        out_specs=pl.BlockSpec((qb, vd), lambda b, h, i: (b * nq + i, h)),
        out_shape=jax.ShapeDtypeStruct((t, d), BF16),
        scratch_shapes=[pltpu.VMEM((seq_len, vd), BF16), pltpu.VMEM((past, vd), BF16),
                        pltpu.VMEM((seq_len, vd), BF16), pltpu.VMEM((past, vd), BF16)],
        compiler_params=_cparams(("parallel", "parallel", "arbitrary")),
        name="attn_lat",
    )(q, k, v, cache_k, cache_v, cos, sin, cos, sin, at_lam, g_sub.reshape(1, vd))


def _s5_prep_kernel(lre_r, lim_r, ldt_r, lre_c, lim_c, ldt_c, btre_ref, btim_ref, ctre_ref, ctim_ref,
                    wst_ref, t_ref, wout_ref, lam_ref):
    Q = S5_CHUNK
    H = S5_GROUP
    P2 = lre_r.shape[-1]
    P = P2 // 2
    N = Q * H

    def cexp(n, re_dt, im_dt):
        mag = jnp.exp(n * re_dt)
        return mag * jnp.cos(n * im_dt), mag * jnp.sin(n * im_dt)

    re = jnp.minimum(lre_r[...], -1e-4)
    im = lim_r[...]
    dt = jnp.exp(ldt_r[...])
    re_dt, im_dt = re * dt, im * dt
    lb_re, lb_im = cexp(1.0, re_dt, im_dt)
    den = re * re + im * im
    q_re = ((lb_re - 1.0) * re + lb_im * im) / den
    q_im = (lb_im * re - (lb_re - 1.0) * im) / den
    bt_re, bt_im = btre_ref[...], btim_ref[...]
    bb_re = q_re * bt_re - q_im * bt_im
    bb_im = q_re * bt_im + q_im * bt_re
    fwd_lane = lax.broadcasted_iota(jnp.int32, (Q, P2), 1) < P
    srow = lax.broadcasted_iota(jnp.int32, (Q, P2), 0)
    n = jnp.where(fwd_lane, (Q - 1) - srow, srow).astype(F32)
    pw_re, pw_im = cexp(n, re_dt, im_dt)
    for s in range(Q):
        pr, pi = pw_re[s:s + 1, :], pw_im[s:s + 1, :]
        wst_ref[s * H:(s + 1) * H, 0:P2] = (bb_re * pr - bb_im * pi).astype(BF16)
        wst_ref[s * H:(s + 1) * H, P2:2 * P2] = (bb_re * pi + bb_im * pr).astype(BF16)
    lq_re, lq_im = cexp(float(Q), re_dt, im_dt)
    lam_ref[0:1, :] = lq_re
    lam_ref[1:2, :] = lq_im

    rec = jnp.minimum(lre_c[...], -1e-4)
    imc = lim_c[...]
    dtc = jnp.exp(ldt_c[...])
    rec_dt, imc_dt = rec * dtc, imc * dtc
    l1_re, l1_im = cexp(1.0, rec_dt, imc_dt)
    tlane = lax.broadcasted_iota(jnp.int32, (P2, N), 1) // H
    fwd_row = lax.broadcasted_iota(jnp.int32, (P2, N), 0) < P
    nt = jnp.where(fwd_row, tlane, (Q - 1) - tlane)
    g_re = jnp.ones((P2, N), F32)
    g_im = jnp.zeros((P2, N), F32)
    b_re, b_im = l1_re, l1_im
    bit = 1
    while bit < Q:
        use = (nt & bit) != 0
        f_re = jnp.where(use, b_re, 1.0)
        f_im = jnp.where(use, b_im, 0.0)
        g_re, g_im = g_re * f_re - g_im * f_im, g_re * f_im + g_im * f_re
        b_re, b_im = b_re * b_re - b_im * b_im, 2.0 * b_re * b_im
        bit *= 2
    c_re, c_im = ctre_ref[...], ctim_ref[...]
    gx_re = c_re * g_re - c_im * g_im
    gx_im = c_re * g_im + c_im * g_re
    wout_ref[0:P2, :] = (gx_re * l1_re - gx_im * l1_im).astype(BF16)
    wout_ref[P2:2 * P2, :] = (-(gx_re * l1_im + gx_im * l1_re)).astype(BF16)

    gx = jnp.concatenate([gx_re, gx_im], axis=0)
    is_f = lax.broadcasted_iota(jnp.int32, (H, P2), 1) < P
    zero = jnp.zeros((H, P2), F32)
    lhs_f = jnp.concatenate([jnp.where(is_f, bb_re, zero), jnp.where(is_f, -bb_im, zero)], axis=1)
    lhs_b = jnp.concatenate([jnp.where(is_f, zero, bb_re), jnp.where(is_f, zero, -bb_im)], axis=1)
    m_f = jnp.dot(lhs_f, gx, precision=HIGHEST, preferred_element_type=F32)
    m_b = jnp.dot(lhs_b, gx, precision=HIGHEST, preferred_element_type=F32)
    lane_n = lax.broadcasted_iota(jnp.int32, (H, N), 1)
    for s in range(Q):
        tf = m_f if s == 0 else pltpu.roll(m_f, s * H, 1)
        tb = m_b if s == Q - 1 else pltpu.roll(m_b, (s + 1) * H, 1)
        slab = jnp.where(lane_n >= s * H, tf, 0.0) + jnp.where(lane_n < (s + 1) * H, tb, 0.0)
        t_ref[s * H:(s + 1) * H, :] = slab.astype(BF16)


def _s5_prep(lam_re, lam_im, log_dt, b_re, b_im, c_re, c_im):
    _, G, P = lam_re.shape
    H = S5_GROUP
    N = S5_CHUNK * H
    P2 = 2 * P
    fb_lanes = lambda a: jnp.concatenate([a[0], a[1]], axis=-1)
    ldt = jnp.broadcast_to(log_dt[..., None], (2, G, P))
    rows = [fb_lanes(a).reshape(G, 1, P2) for a in (lam_re, lam_im, ldt)]
    cols = [fb_lanes(a).reshape(G, P2, 1) for a in (lam_re, lam_im, ldt)]
    bt = [fb_lanes(jnp.swapaxes(a, -1, -2)) for a in (b_re, b_im)]
    ct = [jnp.tile(jnp.concatenate([jnp.swapaxes(a[0], -1, -2), jnp.swapaxes(a[1], -1, -2)], axis=1),
                   (1, 1, S5_CHUNK)) for a in (c_re, c_im)]
    ins = rows + cols + bt + ct
    gspec = lambda a: pl.BlockSpec((None,) + a.shape[1:], lambda g: (g, 0, 0))
    mspec = pl.BlockSpec((None, N, N), lambda g: (g, 0, 0))
    msh = jax.ShapeDtypeStruct((G, N, N), BF16)
    return pl.pallas_call(
        _s5_prep_kernel,
        grid=(G,),
        in_specs=[gspec(a) for a in ins],
        out_specs=[mspec, mspec, mspec, pl.BlockSpec((None, 2, P2), lambda g: (g, 0, 0))],
        out_shape=[msh, msh, msh, jax.ShapeDtypeStruct((G, 2, P2), F32)],
        compiler_params=_cparams(("parallel",)),
        name="s5_prep",
    )(*ins)


def _s5_core_kernel(h_ref, wst_ref, t_ref, wout_ref, lam_ref, s0_ref, y_ref, fin_ref,
                    u_scr, loc, sa, sb, *, nb, nc, gl):
    Q, H, GB = S5_CHUNK, S5_GROUP, S5_GROUPS_PER_STEP
    R = nb * nc
    P2 = lam_ref.shape[-1]
    lane_blk = lax.broadcasted_iota(jnp.int32, (R, LANES), 1) // H

    def gather_blocks(pieces, src_blk):
        acc = None
        for b, piece in enumerate(pieces):
            shift = ((b - src_blk) % GB) * H
            r = piece if shift == 0 else pltpu.roll(piece, shift, 1)
            acc = r if acc is None else jnp.where(lane_blk == b, r, acc)
        return acc

    slabs = [h_ref[:, s, :, :].reshape(R, LANES) for s in range(Q)]
    for g in range(GB):
        u = jnp.concatenate([gather_blocks(slabs[0:GB], g), gather_blocks(slabs[GB:Q], g)],
                            axis=-1).astype(BF16)
        u_scr[g] = u
        loc[g] = jnp.dot(u, wst_ref[g], preferred_element_type=F32)

    fwd_half = lax.broadcasted_iota(jnp.int32, (nb, P2), 1) < P2 // 2
    for g0 in range(0, GB, gl):
        def body(k, carry, g0=g0):
            rf = pl.multiple_of(k * nb, nb)
            rb = pl.multiple_of((nc - 1 - k) * nb, nb)
            out = []
            for gi in range(gl):
                g = g0 + gi
                xr, xi = carry[2 * gi], carry[2 * gi + 1]
                sa[g, pl.ds(rf, nb), 0:P2] = xr
                sa[g, pl.ds(rf, nb), P2:2 * P2] = xi
                sb[g, pl.ds(rb, nb), 0:P2] = xr
                sb[g, pl.ds(rb, nb), P2:2 * P2] = xi
                lr = jnp.where(fwd_half, loc[g, pl.ds(rf, nb), 0:P2], loc[g, pl.ds(rb, nb), 0:P2])
                li = jnp.where(fwd_half, loc[g, pl.ds(rf, nb), P2:2 * P2],
                               loc[g, pl.ds(rb, nb), P2:2 * P2])
                ar, ai = lam_ref[g, 0:1, :], lam_ref[g, 1:2, :]
                out += [ar * xr - ai * xi + lr, ar * xi + ai * xr + li]
            return tuple(out)

        init = []
        for gi in range(gl):
            init += [s0_ref[g0 + gi, :, 0:P2], s0_ref[g0 + gi, :, P2:2 * P2]]
        fin = lax.fori_loop(0, nc, body, tuple(init))
        for gi in range(gl):
            fin_ref[g0 + gi, :, 0:P2] = fin[2 * gi]
            fin_ref[g0 + gi, :, P2:2 * P2] = fin[2 * gi + 1]

    fsel = (lax.broadcasted_iota(jnp.int32, (R, 2 * P2), 1) % P2) < P2 // 2
    for g in range(GB):
        s_in = jnp.where(fsel, sa[g], sb[g]).astype(BF16)
        loc[g] = (jnp.dot(u_scr[g], t_ref[g], preferred_element_type=F32)
                  + jnp.dot(s_in, wout_ref[g], preferred_element_type=F32))
    for half in range(Q // GB):
        ys = [loc[g, :, half * LANES:(half + 1) * LANES] for g in range(GB)]
        for tl in range(GB):
            y_ref[:, half * GB + tl, :, :] = gather_blocks(ys, tl).reshape(nc, nb, LANES)


def _s5_core(h_tm, s0, wst, tmat, wout, lam, *, nb, seq_len):
    d = h_tm.shape[1] // nb
    H, Q, GB = S5_GROUP, S5_CHUNK, S5_GROUPS_PER_STEP
    G = d // H
    nc = seq_len // Q
    R = nb * nc
    N = Q * H
    P4 = s0.shape[-1]
    gl = max(1, min(GB, (8 * GB) // nb))
    h4 = h_tm.reshape(nc, Q, nb, d)
    hspec = pl.BlockSpec((nc, Q, nb, LANES), lambda gb: (0, 0, 0, gb))
    mspec = pl.BlockSpec((GB, N, N), lambda gb: (gb, 0, 0))
    sspec = pl.BlockSpec((GB, nb, P4), lambda gb: (gb, 0, 0))
    y, fin = pl.pallas_call(
        functools.partial(_s5_core_kernel, nb=nb, nc=nc, gl=gl),
        grid=(G // GB,),
        in_specs=[hspec, mspec, mspec, mspec,
                  pl.BlockSpec((GB, 2, P4 // 2), lambda gb: (gb, 0, 0)), sspec],
        out_specs=[hspec, sspec],
        out_shape=[jax.ShapeDtypeStruct((nc, Q, nb, d), F32), jax.ShapeDtypeStruct((G, nb, P4), F32)],
        scratch_shapes=[pltpu.VMEM((GB, R, N), BF16), pltpu.VMEM((GB, R, N), F32),
                        pltpu.VMEM((GB, R, P4), F32), pltpu.VMEM((GB, R, P4), F32)],
        compiler_params=_cparams(("parallel",)),
        name="s5_core",
    )(h4, wst, tmat, wout, lam, s0)
    return y.reshape(seq_len, nb * d), fin


def _glu_postadd_kernel(h_ref, y_ref, d_ref, wa_ref, wg_ref, ba_ref, bg_ref, x_ref, gate_ref, g_ref,
                        o_ref, acc_a, acc_g):
    k = pl.program_id(1)

    @pl.when(k == 0)
    def _():
        acc_a[...] = jnp.zeros_like(acc_a)
        acc_g[...] = jnp.zeros_like(acc_g)

    u = jax.nn.gelu(d_ref[...] * h_ref[...] + y_ref[...]).astype(BF16)
    acc_a[...] += jnp.dot(u, wa_ref[...], preferred_element_type=F32)
    acc_g[...] += jnp.dot(u, wg_ref[...], preferred_element_type=F32)

    @pl.when(k == pl.num_programs(1) - 1)
    def _():
        o = (acc_a[...] + ba_ref[...]) * jax.nn.sigmoid(acc_g[...] + bg_ref[...])
        o_ref[...] = _post_add(x_ref[...], o, g_ref[...], gate_ref[...])


def _glu_postadd(h_tm, y_tm, dskip, w_glu, layer, b_glu, x, m_l, g1, *, seq_len, latent):
    t, d = x.shape
    tm = min(512, seq_len)
    per_seq = seq_len // tm
    tk = 512
    nk = d // tk
    rowfn = _row_fn(tm, seq_len, latent)
    b_glu = b_glu.reshape(1, 2 * d)
    tspec = pl.BlockSpec((tm, tk), lambda i, k: (i % per_seq, (i // per_seq) * nk + k))
    return pl.pallas_call(
        _glu_postadd_kernel,
        grid=(t // tm, nk),
        in_specs=[
            tspec, tspec,
            pl.BlockSpec((1, tk), lambda i, k: (0, k)),
            pl.BlockSpec((None, tk, d), lambda i, k: (layer, k, 0)),
            pl.BlockSpec((None, tk, d), lambda i, k: (layer, k, 1)),
            pl.BlockSpec((1, d), lambda i, k: (0, 0)),
            pl.BlockSpec((1, d), lambda i, k: (0, 1)),
            pl.BlockSpec((tm, d), lambda i, k: (i, 0)),
            _mod_spec(d, 2, rowfn),
            pl.BlockSpec((1, d), lambda i, k: (0, 0)),
        ],
        out_specs=pl.BlockSpec((tm, d), lambda i, k: (i, 0)),
        out_shape=jax.ShapeDtypeStruct((t, d), F32),
        scratch_shapes=[pltpu.VMEM((tm, d), F32), pltpu.VMEM((tm, d), F32)],
        compiler_params=_cparams(("parallel", "arbitrary")),
        name="glu_postadd",
    )(h_tm, y_tm, dskip.reshape(1, d), w_glu, w_glu, b_glu, b_glu, x, m_l, g1.reshape(1, d))


def kernel(x_prompt, x_sample, cache_attn_k, cache_attn_v, state_s5_re, state_s5_im, c, c_ctx, w_mod, b_mod, g_norm, w_mlp_in, w_mlp_out, hy_w_in, hy_b_in, hy_w_short, hy_b_short, hy_f_w1, hy_f_b1, hy_f_freq1, hy_f_w2, hy_f_b2, hy_f_freq2, hy_f_w3, hy_log_alpha, hy_skip, hy_w_out, hy_b_out, at_w_qkv, at_lam, at_g_sub, at_w_o, s5_lam_re, s5_lam_im, s5_log_dt, s5_b_re, s5_b_im, s5_c_re, s5_c_im, s5_d, s5_w_glu, s5_b_glu):
    bc, lc, d = x_prompt.shape
    bl, ll, _ = x_sample.shape
    depth = w_mod.shape[0]
    assert 1 + bl <= MOD_ROWS
    assert cache_attn_k.shape[1] == 1 and state_s5_re.shape[1] == 1, "one attention and one S5 layer"
    hd = d // N_HEADS // 2
    G = d // S5_GROUP
    P = s5_lam_re.shape[-1]

    cond = jnp.concatenate([c_ctx[None], c, jnp.zeros((MOD_ROWS - 1 - bl, d), F32)], axis=0)
    mod = _modulation(cond, w_mod, b_mod).reshape(depth, MOD_ROWS, N_MOD, 1, d)

    streams = [dict(seq_len=lc, latent=False), dict(seq_len=ll, latent=True)]
    xs = [x_prompt.reshape(bc * lc, d), x_sample.reshape(bl * ll, d)]

    tables = {}
    for L in {lc, ll}:
        cm, sm, sp, ci, sip = _dft_tables(L)
        tables[L] = (cm, sm, tuple(a.astype(BF16) for a in (cm, sp, ci, sip)))

    w_mlp_in_b, w_mlp_out_b = w_mlp_in.astype(BF16), w_mlp_out.astype(BF16)
    hy_w_in_b, hy_w_out_b = hy_w_in.astype(BF16), hy_w_out.astype(BF16)
    at_w_qkv_b, at_w_o_b = at_w_qkv.astype(BF16), at_w_o.astype(BF16)
    s5_w_glu_b = s5_w_glu.astype(BF16)

    new_k = new_v = None
    fin_ctx = None
    for i in range(depth):
        kind, j = i % 3, i // 3
        m_l = mod[i]
        g = g_norm[i]
        if kind == 0:
            filt = {}
            for L in {lc, ll}:
                filt[L] = _hy_filter(L, hy_f_w1[j], hy_f_b1[j], hy_f_freq1[j], hy_f_w2[j], hy_f_b2[j],
                                     hy_f_freq2[j], hy_f_w3[j], hy_log_alpha[j], tables[L][0], tables[L][1])
            for si, st in enumerate(streams):
                L = st["seq_len"]
                z0, z1, zv = _premod_mm3(xs[si], m_l, g[0], hy_w_in_b, j, hy_b_in[j].reshape(1, 3 * d), **st)
                a = _hy_conv(z0, z1, zv, hy_w_short[j], hy_b_short[j], filt[L][0], filt[L][1],
                             hy_skip[j], tables[L][2], seq_len=L)
                xs[si] = _mm_postadd(a, hy_w_out_b, j, hy_b_out[j], xs[si], m_l, g[1], **st)
        elif kind == 1:
            lam_init = 0.8 - 0.6 * math.exp(-0.3 * i)
            zero_b = jnp.zeros((1, 3 * d), F32)
            for si, st in enumerate(streams):
                L = st["seq_len"]
                q, k, v = _premod_mm3(xs[si], m_l, g[0], at_w_qkv_b, j, zero_b, **st)
                if not st["latent"]:
                    new_k = k.reshape(bc, 1, lc, N_HEADS, 2, hd)
                    new_v = v.reshape(bc, 1, lc, N_HEADS, 2 * hd)
                    a = _attn_ctx(q, k, v, at_lam[j], at_g_sub[j], seq_len=L, lam_init=lam_init)
                else:
                    ck = cache_attn_k[:, j].reshape(bl, -1, d)
                    cv = cache_attn_v[:, j].reshape(bl, -1, d)
                    a = _attn_lat(q, k, v, ck, cv, at_lam[j], at_g_sub[j], seq_len=L, lam_init=lam_init)
                xs[si] = _mm_postadd(a, at_w_o_b, j, jnp.zeros((d,), F32), xs[si], m_l, g[1], **st)
        else:
            wst, tmat, wout, lam_q = _s5_prep(s5_lam_re[j], s5_lam_im[j], s5_log_dt[j], s5_b_re[j],
                                              s5_b_im[j], s5_c_re[j], s5_c_im[j])
            for si, st in enumerate(streams):
                L = st["seq_len"]
                nb = xs[si].shape[0] // L
                if st["latent"]:
                    sre, sim = state_s5_re[:, j], state_s5_im[:, j]
                    s0 = jnp.concatenate([sre[:, 0], sre[:, 1], sim[:, 0], sim[:, 1]], axis=-1)
                    s0 = s0.transpose(1, 0, 2)
                else:
                    s0 = jnp.zeros((G, nb, 4 * P), F32)
                h_tm = _premod_time_major(xs[si], m_l, g[0], **st)
                y_tm, fin = _s5_core(h_tm, s0, wst, tmat, wout, lam_q, nb=nb, seq_len=L)
                if not st["latent"]:
                    fin_ctx = fin.reshape(G, nb, 2, 2, P).transpose(1, 2, 3, 0, 4)
                xs[si] = _glu_postadd(h_tm, y_tm, s5_d[j], s5_w_glu_b, j, s5_b_glu[j], xs[si], m_l, g[1], **st)
        for si, st in enumerate(streams):
            xs[si] = _mlp(xs[si], m_l, g[2], g[3], w_mlp_in_b, w_mlp_out_b, i, **st)

    new_s_re = fin_ctx[:, 0][:, None]
    new_s_im = fin_ctx[:, 1][:, None]
    return (xs[0].reshape(bc, lc, d), xs[1].reshape(bl, ll, d), new_k, new_v, new_s_re, new_s_im)
```

```python
import functools
import math

import numpy as np
import jax
import jax.numpy as jnp
from jax import lax
from jax.experimental import pallas as pl
from jax.experimental.pallas import tpu as pltpu

F32 = jnp.float32
BF16 = jnp.bfloat16
HIGHEST = lax.Precision.HIGHEST

NORM_EPS = 1e-6
N_MOD = 6
N_HEADS = 8
GRID_W = 64
ROPE_BASE = 10000.0
HY_PE_BANDS = 16
HY_PE_MIN_PERIOD = 2.0
HY_PE_MAX_PERIOD = 4096.0
S5_GROUP = 16
S5_CHUNK = 16
LANES = 128
S5_GROUPS_PER_STEP = LANES // S5_GROUP
MOD_ROWS = 16

VMEM_LIMIT = 56 * 1024 * 1024


def _cparams(sem):
    return pltpu.CompilerParams(dimension_semantics=sem, vmem_limit_bytes=VMEM_LIMIT)


def _rms(x):
    return lax.rsqrt(jnp.mean(x * x, axis=-1, keepdims=True) + NORM_EPS)


def _mod_kernel(c_ref, w_ref, b_ref, o_ref):
    c = c_ref[...]
    s = c * jax.nn.sigmoid(c)
    o_ref[...] = jnp.dot(s, w_ref[...], precision=HIGHEST, preferred_element_type=F32) + b_ref[...]


def _modulation(cond, w_mod, b_mod):
    depth, d, n = w_mod.shape
    tn = 1024
    return pl.pallas_call(
        _mod_kernel,
        grid=(depth, n // tn),
        in_specs=[
            pl.BlockSpec((MOD_ROWS, d), lambda l, j: (0, 0)),
            pl.BlockSpec((None, d, tn), lambda l, j: (l, 0, j)),
            pl.BlockSpec((None, 1, tn), lambda l, j: (l, 0, j)),
        ],
        out_specs=pl.BlockSpec((None, MOD_ROWS, tn), lambda l, j: (l, 0, j)),
        out_shape=jax.ShapeDtypeStruct((depth, MOD_ROWS, n), F32),
        compiler_params=_cparams(("parallel", "parallel")),
        name="adaln_mod",
    )(cond, w_mod, b_mod.reshape(depth, 1, n))


def _mod_spec(d, which, rowfn):
    return pl.BlockSpec((None, None, 1, d), lambda i, *_: (rowfn(i), which, 0, 0))


def _row_fn(tm, seq_len, latent):
    if latent:
        assert seq_len % tm == 0, "a latent row tile must sit inside one sequence"
        return lambda i: 1 + (i * tm) // seq_len
    return lambda i: 0


ROW_CHUNK = 16


def _for_row_chunks(n_rows, fn):
    def body(c, carry):
        fn(pl.ds(pl.multiple_of(c * ROW_CHUNK, ROW_CHUNK), ROW_CHUNK))
        return carry

    lax.fori_loop(0, n_rows // ROW_CHUNK, body, 0, unroll=16)


def _premod_rows(x_ref, g_ref, sh_ref, sc_ref, out_ref):
    gs, sh = g_ref[...] * (1.0 + sc_ref[...]), sh_ref[...]

    def rows_fn(rows):
        x = x_ref[rows, :]
        out_ref[rows, :] = ((x * _rms(x)) * gs + sh).astype(out_ref.dtype)

    _for_row_chunks(x_ref.shape[0], rows_fn)


def _post_add_rows(x_ref, o_fn, g_ref, gate_ref, out_ref):
    gg = gate_ref[...] * g_ref[...]

    def rows_fn(rows):
        o = o_fn(rows)
        out_ref[rows, :] = x_ref[rows, :] + (o * _rms(o)) * gg

    _for_row_chunks(x_ref.shape[0], rows_fn)


def _premod_mm3_kernel(x_ref, g_ref, sh_ref, sc_ref, wa_ref, wb_ref, wc_ref, ba_ref, bb_ref, bc_ref,
                       oa_ref, ob_ref, oc_ref, h_scr):
    @pl.when(pl.program_id(1) == 0)
    def _():
        _premod_rows(x_ref, g_ref, sh_ref, sc_ref, h_scr)

    h = h_scr[...]
    for w_ref, b_ref, o_ref in ((wa_ref, ba_ref, oa_ref), (wb_ref, bb_ref, ob_ref), (wc_ref, bc_ref, oc_ref)):
        o_ref[...] = (jnp.dot(h, w_ref[...], preferred_element_type=F32) + b_ref[...]).astype(o_ref.dtype)


def _premod_mm3(x, m_l, g, w, layer, b, *, seq_len, latent, out_dtype=F32):
    t, d = x.shape
    tm = min(1024, seq_len if latent else t)
    tn = 512
    nj = d // tn
    rowfn = _row_fn(tm, seq_len, latent)
    wspec = lambda o: pl.BlockSpec((None, d, tn), lambda i, j: (layer, 0, o * nj + j))
    bspec = lambda o: pl.BlockSpec((1, tn), lambda i, j: (0, o * nj + j))
    ospec = pl.BlockSpec((tm, tn), lambda i, j: (i, j))
    osh = jax.ShapeDtypeStruct((t, d), out_dtype)
    return pl.pallas_call(
        _premod_mm3_kernel,
        grid=(t // tm, nj),
        in_specs=[
            pl.BlockSpec((tm, d), lambda i, j: (i, 0)),
            pl.BlockSpec((1, d), lambda i, j: (0, 0)),
            _mod_spec(d, 0, rowfn),
            _mod_spec(d, 1, rowfn),
            wspec(0), wspec(1), wspec(2), bspec(0), bspec(1), bspec(2),
        ],
        out_specs=[ospec, ospec, ospec],
        out_shape=[osh, osh, osh],
        scratch_shapes=[pltpu.VMEM((tm, d), BF16)],
        compiler_params=_cparams(("parallel", "arbitrary")),
        name="premod_mm3",
    )(x, g.reshape(1, d), m_l, m_l, w, w, w, b, b, b)


def _premod_kernel(x_ref, g_ref, sh_ref, sc_ref, o_ref):
    _premod_rows(x_ref, g_ref, sh_ref, sc_ref, o_ref)


def _premod_time_major(x, m_l, g, *, seq_len, latent):
    t, d = x.shape
    nb = t // seq_len
    tm = min(512, seq_len)
    per_seq = seq_len // tm
    rowfn = _row_fn(tm, seq_len, latent)
    return pl.pallas_call(
        _premod_kernel,
        grid=(t // tm,),
        in_specs=[
            pl.BlockSpec((tm, d), lambda i: (i, 0)),
            pl.BlockSpec((1, d), lambda i: (0, 0)),
            _mod_spec(d, 0, rowfn),
            _mod_spec(d, 1, rowfn),
        ],
        out_specs=pl.BlockSpec((tm, d), lambda i: (i % per_seq, i // per_seq)),
        out_shape=jax.ShapeDtypeStruct((seq_len, nb * d), F32),
        compiler_params=_cparams(("parallel",)),
        name="premod",
    )(x, g.reshape(1, d), m_l, m_l)


def _mlp_kernel(x_ref, g2_ref, sh_ref, sc_ref, gate_ref, g3_ref, w1_ref, w2_ref, o_ref, h_scr, acc):
    f = pl.program_id(1)

    @pl.when(f == 0)
    def _():
        _premod_rows(x_ref, g2_ref, sh_ref, sc_ref, h_scr)
        acc[...] = jnp.zeros_like(acc)

    a = jnp.dot(h_scr[...], w1_ref[...], preferred_element_type=F32)
    a = jnp.square(jnp.maximum(a, 0.0)).astype(BF16)
    acc[...] += jnp.dot(a, w2_ref[...], preferred_element_type=F32)

    @pl.when(f == pl.num_programs(1) - 1)
    def _():
        _post_add_rows(x_ref, lambda rows: acc[rows, :], g3_ref, gate_ref, o_ref)


def _mlp(x, m_l, g2, g3, w1, w2, layer, *, seq_len, latent):
    t, d = x.shape
    dff = w1.shape[-1]
    tm = min(512, t)
    tf = 1024
    rowfn = _row_fn(tm, seq_len, latent)
    return pl.pallas_call(
        _mlp_kernel,
        grid=(t // tm, dff // tf),
        in_specs=[
            pl.BlockSpec((tm, d), lambda i, f: (i, 0)),
            pl.BlockSpec((1, d), lambda i, f: (0, 0)),
            _mod_spec(d, 3, rowfn),
            _mod_spec(d, 4, rowfn),
            _mod_spec(d, 5, rowfn),
            pl.BlockSpec((1, d), lambda i, f: (0, 0)),
            pl.BlockSpec((None, d, tf), lambda i, f: (layer, 0, f)),
            pl.BlockSpec((None, tf, d), lambda i, f: (layer, f, 0)),
        ],
        out_specs=pl.BlockSpec((tm, d), lambda i, f: (i, 0)),
        out_shape=jax.ShapeDtypeStruct((t, d), F32),
        scratch_shapes=[pltpu.VMEM((tm, d), BF16), pltpu.VMEM((tm, d), F32)],
        compiler_params=_cparams(("parallel", "arbitrary")),
        name="mlp",
    )(x, g2.reshape(1, d), m_l, m_l, m_l, g3.reshape(1, d), w1, w2)


def _mm_postadd_kernel(a_ref, w_ref, b_ref, x_ref, gate_ref, g_ref, o_ref, acc):
    acc[...] = jnp.dot(a_ref[...], w_ref[...], preferred_element_type=F32)
    b = b_ref[...]
    _post_add_rows(x_ref, lambda rows: acc[rows, :] + b, g_ref, gate_ref, o_ref)


def _mm_postadd(a, w, layer, b, x, m_l, g1, *, seq_len, latent):
    t, d = x.shape
    tm = min(512, t)
    rowfn = _row_fn(tm, seq_len, latent)
    return pl.pallas_call(
        _mm_postadd_kernel,
        grid=(t // tm,),
        in_specs=[
            pl.BlockSpec((tm, d), lambda i: (i, 0)),
            pl.BlockSpec((None, d, d), lambda i: (layer, 0, 0)),
            pl.BlockSpec((1, d), lambda i: (0, 0)),
            pl.BlockSpec((tm, d), lambda i: (i, 0)),
            _mod_spec(d, 2, rowfn),
            pl.BlockSpec((1, d), lambda i: (0, 0)),
        ],
        out_specs=pl.BlockSpec((tm, d), lambda i: (i, 0)),
        out_shape=jax.ShapeDtypeStruct((t, d), F32),
        scratch_shapes=[pltpu.VMEM((tm, d), F32)],
        compiler_params=_cparams(("parallel",)),
        name="mm_postadd",
    )(a, w, b.reshape(1, d), x, m_l, g1.reshape(1, d))


def _dft_tables(L):
    idx = np.arange(L, dtype=np.int64)
    ang = np.pi * ((idx[:, None] * idx[None, :]) % (2 * L)).astype(np.float64) / L
    c = np.cos(ang)
    s = np.sin(ang)
    alt = np.where(idx % 2 == 0, 1.0, -1.0)
    sp = s.copy()
    sp[0, :] = alt
    wf = np.full((L,), 2.0)
    wf[0] = 1.0
    ci = c * wf[None, :] / (2 * L)
    sip = -s * 2.0 / (2 * L)
    sip[:, 0] = alt / (2 * L)
    f32 = lambda a: jnp.asarray(a.astype(np.float32))
    return f32(c), f32(s), f32(sp), f32(ci), f32(sip)


def _hy_pe(L):
    t = np.arange(L, dtype=np.float64)
    periods = HY_PE_MIN_PERIOD * (HY_PE_MAX_PERIOD / HY_PE_MIN_PERIOD) ** (
        np.arange(HY_PE_BANDS, dtype=np.float64) / (HY_PE_BANDS - 1))
    ang = t[:, None] * (2.0 * math.pi / periods)[None]
    return jnp.asarray(np.concatenate([np.sin(ang), np.cos(ang)], axis=-1).astype(np.float32))


def _hy_filter_kernel(pe_ref, w1_ref, b1_ref, fr1_ref, w2_ref, b2_ref, fr2_ref, w3f_ref, w3b_ref,
                      la_ref, c_ref, s_ref, kre_ref, kim_ref):
    L, dc = kre_ref.shape
    h = jnp.sin(fr1_ref[...] * (jnp.dot(pe_ref[...], w1_ref[...], precision=HIGHEST,
                                        preferred_element_type=F32) + b1_ref[...]))
    h = jnp.sin(fr2_ref[...] * (jnp.dot(h, w2_ref[...], precision=HIGHEST,
                                        preferred_element_type=F32) + b2_ref[...]))
    row = lax.broadcasted_iota(jnp.int32, (L, dc), 0)
    dec = jnp.exp(-jnp.exp(la_ref[...]) * row.astype(F32))
    kf = jnp.dot(h, w3f_ref[...], precision=HIGHEST, preferred_element_type=F32) * dec
    kb = jnp.dot(h, w3b_ref[...], precision=HIGHEST, preferred_element_type=F32) * dec
    kb = jnp.where(row == 0, 0.0, kb)
    norm = jnp.sum(jnp.abs(kf) + jnp.abs(kb), axis=0, keepdims=True) + 1e-6
    inv = 1.0 / norm
    ks = (kf + kb) * inv
    kd = (kb - kf) * inv
    kre = jnp.dot(c_ref[...], ks, precision=HIGHEST, preferred_element_type=F32)
    kim = jnp.dot(s_ref[...], kd, precision=HIGHEST, preferred_element_type=F32)
    alt = jnp.where(row % 2 == 0, 1.0, -1.0)
    nyq = jnp.sum(alt * ks, axis=0, keepdims=True)
    kre_ref[...] = kre
    kim_ref[...] = jnp.where(row == 0, nyq, kim)


def _hy_filter(L, w1, b1, fr1, w2, b2, fr2, w3, log_alpha, c_mat, s_mat):
    d = log_alpha.shape[-1]
    fw = w1.shape[1]
    dc = 512
    nd = d // dc
    full = lambda a: pl.BlockSpec(a.shape, lambda j: (0,) * a.ndim)
    pe = _hy_pe(L)
    b1, fr1, b2, fr2 = (a.reshape(1, fw) for a in (b1, fr1, b2, fr2))
    osh = jax.ShapeDtypeStruct((L, d), F32)
    return pl.pallas_call(
        _hy_filter_kernel,
        grid=(nd,),
        in_specs=[full(pe), full(w1), full(b1), full(fr1), full(w2), full(b2), full(fr2),
                  pl.BlockSpec((fw, dc), lambda j: (0, j)),
                  pl.BlockSpec((fw, dc), lambda j: (0, nd + j)),
                  pl.BlockSpec((1, dc), lambda j: (0, j)),
                  full(c_mat), full(s_mat)],
        out_specs=[pl.BlockSpec((L, dc), lambda j: (0, j))] * 2,
        out_shape=[osh, osh],
        compiler_params=_cparams(("parallel",)),
        name="hyena_filter",
    )(pe, w1, b1, fr1, w2, b2, fr2, w3, w3, log_alpha.reshape(1, d), c_mat, s_mat)


def _hy_conv_kernel(x0_ref, x1_ref, v_ref, w0_ref, w1_ref, wv_ref, b0_ref, b1_ref, bv_ref,
                    kre_ref, kim_ref, skip_ref, c_ref, sp_ref, ci_ref, sip_ref, o_ref):
    L, dc = o_ref.shape
    row = lax.broadcasted_iota(jnp.int32, (L, dc), 0)

    def short_conv(z_ref, w_ref, b_ref):
        z = z_ref[...].astype(F32)
        zm = jnp.where(row == 0, 0.0, pltpu.roll(z, 1, 0))
        zp = jnp.where(row == L - 1, 0.0, pltpu.roll(z, L - 1, 0))
        return zm * w_ref[0:1, :] + z * w_ref[1:2, :] + zp * w_ref[2:3, :] + b_ref[...]

    x0 = short_conv(x0_ref, w0_ref, b0_ref)
    x1 = short_conv(x1_ref, w1_ref, b1_ref)
    v = short_conv(v_ref, wv_ref, bv_ref) * x1
    vb = v.astype(BF16)
    a = jnp.dot(c_ref[...], vb, preferred_element_type=F32)
    bm = jnp.dot(sp_ref[...], vb, preferred_element_type=F32)
    kre = kre_ref[...]
    kim = kim_ref[...]
    bk = bm * kim
    first = row == 0
    yre = a * kre + jnp.where(first, 0.0, bk)
    yim = jnp.where(first, bk, a * kim - bm * kre)
    y = (jnp.dot(ci_ref[...], yre.astype(BF16), preferred_element_type=F32)
         + jnp.dot(sip_ref[...], yim.astype(BF16), preferred_element_type=F32))
    o_ref[...] = (x0 * (y + v * skip_ref[...])).astype(BF16)


def _hy_conv(zx0, zx1, zv, w_sh, b_sh, kre, kim, skip, tables_bf16, *, seq_len):
    t, d = zx0.shape
    L = seq_len
    nb = t // L
    dc = 512
    nd = d // dc
    once = pl.Buffered(1)
    zspec = pl.BlockSpec((L, dc), lambda j, b: (b, j))
    wspec = lambda o: pl.BlockSpec((3, dc), lambda j, b: (0, o * nd + j))
    bspec = lambda o: pl.BlockSpec((1, dc), lambda j, b: (0, o * nd + j))
    kspec = pl.BlockSpec((L, dc), lambda j, b: (0, j), pipeline_mode=once)
    mspec = pl.BlockSpec((L, L), lambda j, b: (0, 0), pipeline_mode=once)
    b_sh = b_sh.reshape(1, 3 * d)
    return pl.pallas_call(
        _hy_conv_kernel,
        grid=(nd, nb),
        in_specs=[zspec, zspec, zspec, wspec(0), wspec(1), wspec(2), bspec(0), bspec(1), bspec(2),
                  kspec, kspec, pl.BlockSpec((1, dc), lambda j, b: (0, j)),
                  mspec, mspec, mspec, mspec],
        out_specs=pl.BlockSpec((L, dc), lambda j, b: (b, j)),
        out_shape=jax.ShapeDtypeStruct((t, d), BF16),
        compiler_params=_cparams(("parallel", "arbitrary")),
        name="hyena_conv",
    )(zx0, zx1, zv, w_sh, w_sh, w_sh, b_sh, b_sh, b_sh, kre, kim, skip.reshape(1, d), *tables_bf16)


def _rope_tables(L, head_dim):
    rows = L // GRID_W
    row = np.repeat(np.arange(rows), GRID_W).astype(np.float64)
    col = np.tile(np.arange(GRID_W), rows).astype(np.float64)
    half = head_dim // 2
    inv = ROPE_BASE ** (-np.arange(0, half, 2, dtype=np.float64) / half)
    ar = row[:, None] * inv
    ac = col[:, None] * inv
    ang = np.concatenate([ar, ar, ac, ac], axis=-1)
    return (jnp.asarray(np.cos(ang).astype(np.float32)), jnp.asarray(np.sin(ang).astype(np.float32)))


def _rope(x, cos, sin):
    hd = x.shape[-1]
    q = hd // 4
    lane = lax.broadcasted_iota(jnp.int32, x.shape, 1)
    rot = jnp.where((lane % (2 * q)) < q, -pltpu.roll(x, hd - q, 1), pltpu.roll(x, q, 1))
    return x * cos + rot * sin


def _diff_lambda(lam_ref, lam_init):
    lp = lam_ref[...]
    s01 = jnp.sum(lp[0:1, :] * lp[1:2, :], axis=-1, keepdims=True)
    s23 = jnp.sum(lp[2:3, :] * lp[3:4, :], axis=-1, keepdims=True)
    return jnp.exp(s01) - jnp.exp(s23) + lam_init


def _diff_attend_head(q2, k_segs, v_segs, lam, g_sub, out_scale, hd):
    scale = hd ** -0.5
    outs = []
    for comp in range(2):
        qc = (q2[comp] * scale).astype(BF16)
        ss = [lax.dot_general(qc, ks[comp], (((1,), (1,)), ((), ())), preferred_element_type=F32)
              for ks in k_segs]
        m = ss[0].max(axis=-1, keepdims=True)
        for s in ss[1:]:
            m = jnp.maximum(m, s.max(axis=-1, keepdims=True))
        den = None
        pv = None
        for s, v in zip(ss, v_segs):
            e = jnp.exp(s - m)
            dsum = e.sum(axis=-1, keepdims=True)
            den = dsum if den is None else den + dsum
            part = jnp.dot(e.astype(BF16), v, preferred_element_type=F32)
            pv = part if pv is None else pv + part
        outs.append(pv * (1.0 / den))
    o = outs[0] - lam * outs[1]
    return o * _rms(o) * (g_sub * out_scale)


def _attn_ctx_kernel(q_ref, k_ref, v_ref, lam_ref, g_ref, o_ref, *, lam_init, hd):
    lam = _diff_lambda(lam_ref, lam_init)
    vd = 2 * hd
    for h in range(N_HEADS):
        c0 = h * vd
        q2 = [q_ref[:, c0:c0 + hd], q_ref[:, c0 + hd:c0 + vd]]
        k2 = (k_ref[:, c0:c0 + hd].astype(BF16), k_ref[:, c0 + hd:c0 + vd].astype(BF16))
        v = v_ref[:, c0:c0 + vd].astype(BF16)
        o = _diff_attend_head(q2, [k2], [v], lam, g_ref[...], 1.0 - lam_init, hd)
        o_ref[:, c0:c0 + vd] = o.astype(BF16)


def _attn_ctx(q, k, v, at_lam, g_sub, *, seq_len, lam_init):
    t, d = q.shape
    hd = d // N_HEADS // 2
    spec = pl.BlockSpec((seq_len, d), lambda b: (b, 0))
    return pl.pallas_call(
        functools.partial(_attn_ctx_kernel, lam_init=lam_init, hd=hd),
        grid=(t // seq_len,),
        in_specs=[spec, spec, spec,
                  pl.BlockSpec(at_lam.shape, lambda b: (0, 0)),
                  pl.BlockSpec((1, 2 * hd), lambda b: (0, 0))],
        out_specs=spec,
        out_shape=jax.ShapeDtypeStruct((t, d), BF16),
        compiler_params=_cparams(("parallel",)),
        name="attn_ctx",
    )(q, k, v, at_lam, g_sub.reshape(1, 2 * hd))


def _attn_lat_kernel(q_ref, k_ref, v_ref, ck_ref, cv_ref, cq_ref, sq_ref, ckk_ref, skk_ref,
                     lam_ref, g_ref, o_ref, kl_scr, kc_scr, vl_scr, vc_scr, *, lam_init, hd, sub):
    @pl.when(pl.program_id(2) == 0)
    def _():
        ckk, skk = ckk_ref[...], skk_ref[...]
        kl_scr[:, 0:hd] = _rope(k_ref[:, 0:hd], ckk, skk).astype(BF16)
        kl_scr[:, hd:2 * hd] = _rope(k_ref[:, hd:2 * hd], ckk, skk).astype(BF16)
        kc_scr[...] = ck_ref[...].astype(BF16)
        vl_scr[...] = v_ref[...].astype(BF16)
        vc_scr[...] = cv_ref[...].astype(BF16)

    lam = _diff_lambda(lam_ref, lam_init)
    kl = (kl_scr[:, 0:hd], kl_scr[:, hd:2 * hd])
    kc = (kc_scr[:, 0:hd], kc_scr[:, hd:2 * hd])
    for r0 in range(0, q_ref.shape[0], sub):
        rows = slice(r0, r0 + sub)
        cq, sq = cq_ref[rows, :], sq_ref[rows, :]
        q2 = [_rope(q_ref[rows, 0:hd], cq, sq), _rope(q_ref[rows, hd:2 * hd], cq, sq)]
        o = _diff_attend_head(q2, [kc, kl], [vc_scr[...], vl_scr[...]], lam, g_ref[...],
                              1.0 - lam_init, hd)
        o_ref[rows, :] = o.astype(BF16)


def _attn_lat(q, k, v, cache_k, cache_v, at_lam, g_sub, *, seq_len, lam_init):
    t, d = q.shape
    hd = d // N_HEADS // 2
    vd = 2 * hd
    nb = t // seq_len
    past = cache_k.shape[1]
    qb = min(512, seq_len)
    nq = seq_len // qb
    cos, sin = _rope_tables(seq_len, hd)
    return pl.pallas_call(
        functools.partial(_attn_lat_kernel, lam_init=lam_init, hd=hd, sub=min(256, qb)),
        grid=(nb, N_HEADS, nq),
        in_specs=[
            pl.BlockSpec((qb, vd), lambda b, h, i: (b * nq + i, h)),
            pl.BlockSpec((seq_len, vd), lambda b, h, i: (b, h)),
            pl.BlockSpec((seq_len, vd), lambda b, h, i: (b, h)),
            pl.BlockSpec((None, past, vd), lambda b, h, i: (b, 0, h)),
            pl.BlockSpec((None, past, vd), lambda b, h, i: (b, 0, h)),
            pl.BlockSpec((qb, hd), lambda b, h, i: (i, 0)),
            pl.BlockSpec((qb, hd), lambda b, h, i: (i, 0)),
            pl.BlockSpec((seq_len, hd), lambda b, h, i: (0, 0)),
            pl.BlockSpec((seq_len, hd), lambda b, h, i: (0, 0)),
            pl.BlockSpec(at_lam.shape, lambda b, h, i: (0, 0)),
            pl.BlockSpec((1, vd), lambda b, h, i: (0, 0)),
        ],
        out_specs=pl.BlockSpec((qb, vd), lambda b, h, i: (b * nq + i, h)),
        out_shape=jax.ShapeDtypeStruct((t, d), BF16),
        scratch_shapes=[pltpu.VMEM((seq_len, vd), BF16), pltpu.VMEM((past, vd), BF16),
                        pltpu.VMEM((seq_len, vd), BF16), pltpu.VMEM((past, vd), BF16)],
        compiler_params=_cparams(("parallel", "parallel", "arbitrary")),
        name="attn_lat",
    )(q, k, v, cache_k, cache_v, cos, sin, cos, sin, at_lam, g_sub.reshape(1, vd))


def _s5_prep_kernel(lre_r, lim_r, ldt_r, lre_c, lim_c, ldt_c, btre_ref, btim_ref, ctre_ref, ctim_ref,
                    wst_ref, t_ref, wout_ref, lam_ref):
    Q = S5_CHUNK
    H = S5_GROUP
    P2 = lre_r.shape[-1]
    P = P2 // 2
    N = Q * H

    def cexp(n, re_dt, im_dt):
        mag = jnp.exp(n * re_dt)
        return mag * jnp.cos(n * im_dt), mag * jnp.sin(n * im_dt)

    re = jnp.minimum(lre_r[...], -1e-4)
    im = lim_r[...]
    dt = jnp.exp(ldt_r[...])
    re_dt, im_dt = re * dt, im * dt
    lb_re, lb_im = cexp(1.0, re_dt, im_dt)
    den = re * re + im * im
    q_re = ((lb_re - 1.0) * re + lb_im * im) / den
    q_im = (lb_im * re - (lb_re - 1.0) * im) / den
    bt_re, bt_im = btre_ref[...], btim_ref[...]
    bb_re = q_re * bt_re - q_im * bt_im
    bb_im = q_re * bt_im + q_im * bt_re
    fwd_lane = lax.broadcasted_iota(jnp.int32, (Q, P2), 1) < P
    srow = lax.broadcasted_iota(jnp.int32, (Q, P2), 0)
    n = jnp.where(fwd_lane, (Q - 1) - srow, srow).astype(F32)
    pw_re, pw_im = cexp(n, re_dt, im_dt)
    for s in range(Q):
        pr, pi = pw_re[s:s + 1, :], pw_im[s:s + 1, :]
        wst_ref[s * H:(s + 1) * H, 0:P2] = (bb_re * pr - bb_im * pi).astype(BF16)
        wst_ref[s * H:(s + 1) * H, P2:2 * P2] = (bb_re * pi + bb_im * pr).astype(BF16)
    lq_re, lq_im = cexp(float(Q), re_dt, im_dt)
    lam_ref[0:1, :] = lq_re
    lam_ref[1:2, :] = lq_im

    rec = jnp.minimum(lre_c[...], -1e-4)
    imc = lim_c[...]
    dtc = jnp.exp(ldt_c[...])
    rec_dt, imc_dt = rec * dtc, imc * dtc
    l1_re, l1_im = cexp(1.0, rec_dt, imc_dt)
    tlane = lax.broadcasted_iota(jnp.int32, (P2, N), 1) // H
    fwd_row = lax.broadcasted_iota(jnp.int32, (P2, N), 0) < P
    nt = jnp.where(fwd_row, tlane, (Q - 1) - tlane)
    g_re = jnp.ones((P2, N), F32)
    g_im = jnp.zeros((P2, N), F32)
    b_re, b_im = l1_re, l1_im
    bit = 1
    while bit < Q:
        use = (nt & bit) != 0
        f_re = jnp.where(use, b_re, 1.0)
        f_im = jnp.where(use, b_im, 0.0)
        g_re, g_im = g_re * f_re - g_im * f_im, g_re * f_im + g_im * f_re
        b_re, b_im = b_re * b_re - b_im * b_im, 2.0 * b_re * b_im
        bit *= 2
    c_re, c_im = ctre_ref[...], ctim_ref[...]
    gx_re = c_re * g_re - c_im * g_im
    gx_im = c_re * g_im + c_im * g_re
    wout_ref[0:P2, :] = (gx_re * l1_re - gx_im * l1_im).astype(BF16)
    wout_ref[P2:2 * P2, :] = (-(gx_re * l1_im + gx_im * l1_re)).astype(BF16)

    gx = jnp.concatenate([gx_re, gx_im], axis=0)
    is_f = lax.broadcasted_iota(jnp.int32, (H, P2), 1) < P
    zero = jnp.zeros((H, P2), F32)
    lhs_f = jnp.concatenate([jnp.where(is_f, bb_re, zero), jnp.where(is_f, -bb_im, zero)], axis=1)
    lhs_b = jnp.concatenate([jnp.where(is_f, zero, bb_re), jnp.where(is_f, zero, -bb_im)], axis=1)
    m_f = jnp.dot(lhs_f, gx, precision=HIGHEST, preferred_element_type=F32)
    m_b = jnp.dot(lhs_b, gx, precision=HIGHEST, preferred_element_type=F32)
    lane_n = lax.broadcasted_iota(jnp.int32, (H, N), 1)
    for s in range(Q):
        tf = m_f if s == 0 else pltpu.roll(m_f, s * H, 1)
        tb = m_b if s == Q - 1 else pltpu.roll(m_b, (s + 1) * H, 1)
        slab = jnp.where(lane_n >= s * H, tf, 0.0) + jnp.where(lane_n < (s + 1) * H, tb, 0.0)
        t_ref[s * H:(s + 1) * H, :] = slab.astype(BF16)


def _s5_prep(lam_re, lam_im, log_dt, b_re, b_im, c_re, c_im):
    _, G, P = lam_re.shape
    H = S5_GROUP
    N = S5_CHUNK * H
    P2 = 2 * P
    fb_lanes = lambda a: jnp.concatenate([a[0], a[1]], axis=-1)
    ldt = jnp.broadcast_to(log_dt[..., None], (2, G, P))
    rows = [fb_lanes(a).reshape(G, 1, P2) for a in (lam_re, lam_im, ldt)]
    cols = [fb_lanes(a).reshape(G, P2, 1) for a in (lam_re, lam_im, ldt)]
    bt = [fb_lanes(jnp.swapaxes(a, -1, -2)) for a in (b_re, b_im)]
    ct = [jnp.tile(jnp.concatenate([jnp.swapaxes(a[0], -1, -2), jnp.swapaxes(a[1], -1, -2)], axis=1),
                   (1, 1, S5_CHUNK)) for a in (c_re, c_im)]
    ins = rows + cols + bt + ct
    gspec = lambda a: pl.BlockSpec((None,) + a.shape[1:], lambda g: (g, 0, 0))
    mspec = pl.BlockSpec((None, N, N), lambda g: (g, 0, 0))
    msh = jax.ShapeDtypeStruct((G, N, N), BF16)
    return pl.pallas_call(
        _s5_prep_kernel,
        grid=(G,),
        in_specs=[gspec(a) for a in ins],
        out_specs=[mspec, mspec, mspec, pl.BlockSpec((None, 2, P2), lambda g: (g, 0, 0))],
        out_shape=[msh, msh, msh, jax.ShapeDtypeStruct((G, 2, P2), F32)],
        compiler_params=_cparams(("parallel",)),
        name="s5_prep",
    )(*ins)


def _s5_core_kernel(h_ref, wst_ref, t_ref, wout_ref, lam_ref, s0_ref, y_ref, fin_ref,
                    u_scr, loc, sa, sb, *, nb, nc, gl):
    Q, H, GB = S5_CHUNK, S5_GROUP, S5_GROUPS_PER_STEP
    R = nb * nc
    P2 = lam_ref.shape[-1]
    lane_blk = lax.broadcasted_iota(jnp.int32, (R, LANES), 1) // H

    def gather_blocks(pieces, src_blk):
        acc = None
        for b, piece in enumerate(pieces):
            shift = ((b - src_blk) % GB) * H
            r = piece if shift == 0 else pltpu.roll(piece, shift, 1)
            acc = r if acc is None else jnp.where(lane_blk == b, r, acc)
        return acc

    slabs = [h_ref[:, s, :, :].reshape(R, LANES) for s in range(Q)]
    for g in range(GB):
        u = jnp.concatenate([gather_blocks(slabs[0:GB], g), gather_blocks(slabs[GB:Q], g)],
                            axis=-1).astype(BF16)
        u_scr[g] = u
        loc[g] = jnp.dot(u, wst_ref[g], preferred_element_type=F32)

    fwd_half = lax.broadcasted_iota(jnp.int32, (nb, P2), 1) < P2 // 2
    for g0 in range(0, GB, gl):
        def body(k, carry, g0=g0):
            rf = pl.multiple_of(k * nb, nb)
            rb = pl.multiple_of((nc - 1 - k) * nb, nb)
            out = []
            for gi in range(gl):
                g = g0 + gi
                xr, xi = carry[2 * gi], carry[2 * gi + 1]
                sa[g, pl.ds(rf, nb), 0:P2] = xr
                sa[g, pl.ds(rf, nb), P2:2 * P2] = xi
                sb[g, pl.ds(rb, nb), 0:P2] = xr
                sb[g, pl.ds(rb, nb), P2:2 * P2] = xi
                lr = jnp.where(fwd_half, loc[g, pl.ds(rf, nb), 0:P2], loc[g, pl.ds(rb, nb), 0:P2])
                li = jnp.where(fwd_half, loc[g, pl.ds(rf, nb), P2:2 * P2],
                               loc[g, pl.ds(rb, nb), P2:2 * P2])
                ar, ai = lam_ref[g, 0:1, :], lam_ref[g, 1:2, :]
                out += [ar * xr - ai * xi + lr, ar * xi + ai * xr + li]
            return tuple(out)

        init = []
        for gi in range(gl):
            init += [s0_ref[g0 + gi, :, 0:P2], s0_ref[g0 + gi, :, P2:2 * P2]]
        fin = lax.fori_loop(0, nc, body, tuple(init))
        for gi in range(gl):
            fin_ref[g0 + gi, :, 0:P2] = fin[2 * gi]
            fin_ref[g0 + gi, :, P2:2 * P2] = fin[2 * gi + 1]

    fsel = (lax.broadcasted_iota(jnp.int32, (R, 2 * P2), 1) % P2) < P2 // 2
    for g in range(GB):
        s_in = jnp.where(fsel, sa[g], sb[g]).astype(BF16)
        loc[g] = (jnp.dot(u_scr[g], t_ref[g], preferred_element_type=F32)
                  + jnp.dot(s_in, wout_ref[g], preferred_element_type=F32))
    for half in range(Q // GB):
        ys = [loc[g, :, half * LANES:(half + 1) * LANES] for g in range(GB)]
        for tl in range(GB):
            y_ref[:, half * GB + tl, :, :] = gather_blocks(ys, tl).reshape(nc, nb, LANES)


def _s5_core(h_tm, s0, wst, tmat, wout, lam, *, nb, seq_len):
    d = h_tm.shape[1] // nb
    H, Q, GB = S5_GROUP, S5_CHUNK, S5_GROUPS_PER_STEP
    G = d // H
    nc = seq_len // Q
    R = nb * nc
    N = Q * H
    P4 = s0.shape[-1]
    gl = max(1, min(GB, (8 * GB) // nb))
    h4 = h_tm.reshape(nc, Q, nb, d)
    hspec = pl.BlockSpec((nc, Q, nb, LANES), lambda gb: (0, 0, 0, gb))
    mspec = pl.BlockSpec((GB, N, N), lambda gb: (gb, 0, 0))
    sspec = pl.BlockSpec((GB, nb, P4), lambda gb: (gb, 0, 0))
    y, fin = pl.pallas_call(
        functools.partial(_s5_core_kernel, nb=nb, nc=nc, gl=gl),
        grid=(G // GB,),
        in_specs=[hspec, mspec, mspec, mspec,
                  pl.BlockSpec((GB, 2, P4 // 2), lambda gb: (gb, 0, 0)), sspec],
        out_specs=[hspec, sspec],
        out_shape=[jax.ShapeDtypeStruct((nc, Q, nb, d), F32), jax.ShapeDtypeStruct((G, nb, P4), F32)],
        scratch_shapes=[pltpu.VMEM((GB, R, N), BF16), pltpu.VMEM((GB, R, N), F32),
                        pltpu.VMEM((GB, R, P4), F32), pltpu.VMEM((GB, R, P4), F32)],
        compiler_params=_cparams(("parallel",)),
        name="s5_core",
    )(h4, wst, tmat, wout, lam, s0)
    return y.reshape(seq_len, nb * d), fin


def _glu_postadd_kernel(h_ref, y_ref, d_ref, wa_ref, wg_ref, ba_ref, bg_ref, x_ref, gate_ref, g_ref,
                        o_ref, acc_a, acc_g):
    k = pl.program_id(1)

    @pl.when(k == 0)
    def _():
        acc_a[...] = jnp.zeros_like(acc_a)
        acc_g[...] = jnp.zeros_like(acc_g)

    u = jax.nn.gelu(d_ref[...] * h_ref[...] + y_ref[...]).astype(BF16)
    acc_a[...] += jnp.dot(u, wa_ref[...], preferred_element_type=F32)
    acc_g[...] += jnp.dot(u, wg_ref[...], preferred_element_type=F32)

    @pl.when(k == pl.num_programs(1) - 1)
    def _():
        ba, bg = ba_ref[...], bg_ref[...]
        _post_add_rows(x_ref, lambda rows: (acc_a[rows, :] + ba) * jax.nn.sigmoid(acc_g[rows, :] + bg),
                       g_ref, gate_ref, o_ref)


def _glu_postadd(h_tm, y_tm, dskip, w_glu, layer, b_glu, x, m_l, g1, *, seq_len, latent):
    t, d = x.shape
    tm = min(512, seq_len)
    per_seq = seq_len // tm
    tk = 512
    nk = d // tk
    rowfn = _row_fn(tm, seq_len, latent)
    b_glu = b_glu.reshape(1, 2 * d)
    tspec = pl.BlockSpec((tm, tk), lambda i, k: (i % per_seq, (i // per_seq) * nk + k))
    return pl.pallas_call(
        _glu_postadd_kernel,
        grid=(t // tm, nk),
        in_specs=[
            tspec, tspec,
            pl.BlockSpec((1, tk), lambda i, k: (0, k)),
            pl.BlockSpec((None, tk, d), lambda i, k: (layer, k, 0)),
            pl.BlockSpec((None, tk, d), lambda i, k: (layer, k, 1)),
            pl.BlockSpec((1, d), lambda i, k: (0, 0)),
            pl.BlockSpec((1, d), lambda i, k: (0, 1)),
            pl.BlockSpec((tm, d), lambda i, k: (i, 0)),
            _mod_spec(d, 2, rowfn),
            pl.BlockSpec((1, d), lambda i, k: (0, 0)),
        ],
        out_specs=pl.BlockSpec((tm, d), lambda i, k: (i, 0)),
        out_shape=jax.ShapeDtypeStruct((t, d), F32),
        scratch_shapes=[pltpu.VMEM((tm, d), F32), pltpu.VMEM((tm, d), F32)],
        compiler_params=_cparams(("parallel", "arbitrary")),
        name="glu_postadd",
    )(h_tm, y_tm, dskip.reshape(1, d), w_glu, w_glu, b_glu, b_glu, x, m_l, g1.reshape(1, d))


def kernel(x_prompt, x_sample, cache_attn_k, cache_attn_v, state_s5_re, state_s5_im, c, c_ctx, w_mod, b_mod, g_norm, w_mlp_in, w_mlp_out, hy_w_in, hy_b_in, hy_w_short, hy_b_short, hy_f_w1, hy_f_b1, hy_f_freq1, hy_f_w2, hy_f_b2, hy_f_freq2, hy_f_w3, hy_log_alpha, hy_skip, hy_w_out, hy_b_out, at_w_qkv, at_lam, at_g_sub, at_w_o, s5_lam_re, s5_lam_im, s5_log_dt, s5_b_re, s5_b_im, s5_c_re, s5_c_im, s5_d, s5_w_glu, s5_b_glu):
    bc, lc, d = x_prompt.shape
    bl, ll, _ = x_sample.shape
    depth = w_mod.shape[0]
    assert 1 + bl <= MOD_ROWS
    assert cache_attn_k.shape[1] == 1 and state_s5_re.shape[1] == 1, "one attention and one S5 layer"
    hd = d // N_HEADS // 2
    G = d // S5_GROUP
    P = s5_lam_re.shape[-1]

    cond = jnp.concatenate([c_ctx[None], c, jnp.zeros((MOD_ROWS - 1 - bl, d), F32)], axis=0)
    mod = _modulation(cond, w_mod, b_mod).reshape(depth, MOD_ROWS, N_MOD, 1, d)

    streams = [dict(seq_len=lc, latent=False), dict(seq_len=ll, latent=True)]
    xs = [x_prompt.reshape(bc * lc, d), x_sample.reshape(bl * ll, d)]

    tables = {}
    for L in {lc, ll}:
        cm, sm, sp, ci, sip = _dft_tables(L)
        tables[L] = (cm, sm, tuple(a.astype(BF16) for a in (cm, sp, ci, sip)))

    w_mlp_in_b, w_mlp_out_b = w_mlp_in.astype(BF16), w_mlp_out.astype(BF16)
    hy_w_in_b, hy_w_out_b = hy_w_in.astype(BF16), hy_w_out.astype(BF16)
    at_w_qkv_b, at_w_o_b = at_w_qkv.astype(BF16), at_w_o.astype(BF16)
    s5_w_glu_b = s5_w_glu.astype(BF16)

    new_k = new_v = None
    fin_ctx = None
    for i in range(depth):
        kind, j = i % 3, i // 3
        m_l = mod[i]
        g = g_norm[i]
        if kind == 0:
            filt = {}
            for L in {lc, ll}:
                filt[L] = _hy_filter(L, hy_f_w1[j], hy_f_b1[j], hy_f_freq1[j], hy_f_w2[j], hy_f_b2[j],
                                     hy_f_freq2[j], hy_f_w3[j], hy_log_alpha[j], tables[L][0], tables[L][1])
            for si, st in enumerate(streams):
                L = st["seq_len"]
                z0, z1, zv = _premod_mm3(xs[si], m_l, g[0], hy_w_in_b, j, hy_b_in[j].reshape(1, 3 * d),
                                         out_dtype=BF16, **st)
                a = _hy_conv(z0, z1, zv, hy_w_short[j], hy_b_short[j], filt[L][0], filt[L][1],
                             hy_skip[j], tables[L][2], seq_len=L)
                xs[si] = _mm_postadd(a, hy_w_out_b, j, hy_b_out[j], xs[si], m_l, g[1], **st)
        elif kind == 1:
            lam_init = 0.8 - 0.6 * math.exp(-0.3 * i)
            zero_b = jnp.zeros((1, 3 * d), F32)
            for si, st in enumerate(streams):
                L = st["seq_len"]
                q, k, v = _premod_mm3(xs[si], m_l, g[0], at_w_qkv_b, j, zero_b, **st)
                if not st["latent"]:
                    new_k = k.reshape(bc, 1, lc, N_HEADS, 2, hd)
                    new_v = v.reshape(bc, 1, lc, N_HEADS, 2 * hd)
                    a = _attn_ctx(q, k, v, at_lam[j], at_g_sub[j], seq_len=L, lam_init=lam_init)
                else:
                    ck = cache_attn_k[:, j].reshape(bl, -1, d)
                    cv = cache_attn_v[:, j].reshape(bl, -1, d)
                    a = _attn_lat(q, k, v, ck, cv, at_lam[j], at_g_sub[j], seq_len=L, lam_init=lam_init)
                xs[si] = _mm_postadd(a, at_w_o_b, j, jnp.zeros((d,), F32), xs[si], m_l, g[1], **st)
        else:
            wst, tmat, wout, lam_q = _s5_prep(s5_lam_re[j], s5_lam_im[j], s5_log_dt[j], s5_b_re[j],
                                              s5_b_im[j], s5_c_re[j], s5_c_im[j])
            for si, st in enumerate(streams):
                L = st["seq_len"]
                nb = xs[si].shape[0] // L
                if st["latent"]:
                    sre, sim = state_s5_re[:, j], state_s5_im[:, j]
                    s0 = jnp.concatenate([sre[:, 0], sre[:, 1], sim[:, 0], sim[:, 1]], axis=-1)
                    s0 = s0.transpose(1, 0, 2)
                else:
                    s0 = jnp.zeros((G, nb, 4 * P), F32)
                h_tm = _premod_time_major(xs[si], m_l, g[0], **st)
                y_tm, fin = _s5_core(h_tm, s0, wst, tmat, wout, lam_q, nb=nb, seq_len=L)
                if not st["latent"]:
                    fin_ctx = fin.reshape(G, nb, 2, 2, P).transpose(1, 2, 3, 0, 4)
                xs[si] = _glu_postadd(h_tm, y_tm, s5_d[j], s5_w_glu_b, j, s5_b_glu[j], xs[si], m_l, g[1], **st)
        for si, st in enumerate(streams):
            xs[si] = _mlp(xs[si], m_l, g[2], g[3], w_mlp_in_b, w_mlp_out_b, i, **st)

    new_s_re = fin_ctx[:, 0][:, None]
    new_s_im = fin_ctx[:, 1][:, None]
    return (xs[0].reshape(bc, lc, d), xs[1].reshape(bl, ll, d), new_k, new_v, new_s_re, new_s_im)
```

```python
import functools
import math

import numpy as np
import jax
import jax.numpy as jnp
from jax import lax
from jax.experimental import pallas as pl
from jax.experimental.pallas import tpu as pltpu

F32 = jnp.float32
BF16 = jnp.bfloat16
HIGHEST = lax.Precision.HIGHEST

NORM_EPS = 1e-6
N_MOD = 6
N_HEADS = 8
GRID_W = 64
ROPE_BASE = 10000.0
HY_PE_BANDS = 16
HY_PE_MIN_PERIOD = 2.0
HY_PE_MAX_PERIOD = 4096.0
S5_GROUP = 16
S5_CHUNK = 16
LANES = 128
S5_GROUPS_PER_STEP = LANES // S5_GROUP
MOD_ROWS = 16

VMEM_LIMIT = 56 * 1024 * 1024


def _cparams(sem):
    return pltpu.CompilerParams(dimension_semantics=sem, vmem_limit_bytes=VMEM_LIMIT)


def _rms(x):
    return lax.rsqrt(jnp.mean(x * x, axis=-1, keepdims=True) + NORM_EPS)


def _mod_kernel(c_ref, w_ref, b_ref, o_ref):
    c = c_ref[...]
    s = c * jax.nn.sigmoid(c)
    o_ref[...] = jnp.dot(s, w_ref[...], precision=HIGHEST, preferred_element_type=F32) + b_ref[...]


def _modulation(cond, w_mod, b_mod):
    depth, d, n = w_mod.shape
    tn = 1024
    return pl.pallas_call(
        _mod_kernel,
        grid=(depth, n // tn),
        in_specs=[
            pl.BlockSpec((MOD_ROWS, d), lambda l, j: (0, 0)),
            pl.BlockSpec((None, d, tn), lambda l, j: (l, 0, j)),
            pl.BlockSpec((None, 1, tn), lambda l, j: (l, 0, j)),
        ],
        out_specs=pl.BlockSpec((None, MOD_ROWS, tn), lambda l, j: (l, 0, j)),
        out_shape=jax.ShapeDtypeStruct((depth, MOD_ROWS, n), F32),
        compiler_params=_cparams(("parallel", "parallel")),
        name="adaln_mod",
    )(cond, w_mod, b_mod.reshape(depth, 1, n))


def _mod_spec(d, which, rowfn):
    return pl.BlockSpec((None, None, 1, d), lambda i, *_: (rowfn(i), which, 0, 0))


def _row_fn(tm, seq_len, latent):
    if latent:
        assert seq_len % tm == 0, "a latent row tile must sit inside one sequence"
        return lambda i: 1 + (i * tm) // seq_len
    return lambda i: 0


ROW_CHUNK = 16


def _for_row_chunks(n_rows, fn):
    def body(c, carry):
        fn(pl.ds(pl.multiple_of(c * ROW_CHUNK, ROW_CHUNK), ROW_CHUNK))
        return carry

    lax.fori_loop(0, n_rows // ROW_CHUNK, body, 0, unroll=16)


def _premod_rows(x_ref, g_ref, sh_ref, sc_ref, out_ref):
    gs, sh = g_ref[...] * (1.0 + sc_ref[...]), sh_ref[...]

    def rows_fn(rows):
        x = x_ref[rows, :]
        out_ref[rows, :] = ((x * _rms(x)) * gs + sh).astype(out_ref.dtype)

    _for_row_chunks(x_ref.shape[0], rows_fn)


def _post_add_rows(x_ref, o_fn, g_ref, gate_ref, out_ref):
    gg = gate_ref[...] * g_ref[...]

    def rows_fn(rows):
        o = o_fn(rows)
        out_ref[rows, :] = x_ref[rows, :] + (o * _rms(o)) * gg

    _for_row_chunks(x_ref.shape[0], rows_fn)


def _premod_mm3_kernel(x_ref, g_ref, sh_ref, sc_ref, wa_ref, wb_ref, wc_ref, oa_ref, ob_ref, oc_ref, h_scr):
    @pl.when(pl.program_id(1) == 0)
    def _():
        _premod_rows(x_ref, g_ref, sh_ref, sc_ref, h_scr)

    h = h_scr[...]
    for w_ref, o_ref in ((wa_ref, oa_ref), (wb_ref, ob_ref), (wc_ref, oc_ref)):
        o_ref[...] = jnp.dot(h, w_ref[...], preferred_element_type=F32)


def _premod_mm3(x, m_l, g, w, layer, *, seq_len, latent):
    t, d = x.shape
    tm = min(1024, seq_len if latent else t)
    tn = 512
    nj = d // tn
    rowfn = _row_fn(tm, seq_len, latent)
    wspec = lambda o: pl.BlockSpec((None, d, tn), lambda i, j: (layer, 0, o * nj + j))
    ospec = pl.BlockSpec((tm, tn), lambda i, j: (i, j))
    osh = jax.ShapeDtypeStruct((t, d), F32)
    return pl.pallas_call(
        _premod_mm3_kernel,
        grid=(t // tm, nj),
        in_specs=[
            pl.BlockSpec((tm, d), lambda i, j: (i, 0)),
            pl.BlockSpec((1, d), lambda i, j: (0, 0)),
            _mod_spec(d, 0, rowfn),
            _mod_spec(d, 1, rowfn),
            wspec(0), wspec(1), wspec(2),
        ],
        out_specs=[ospec, ospec, ospec],
        out_shape=[osh, osh, osh],
        scratch_shapes=[pltpu.VMEM((tm, d), BF16)],
        compiler_params=_cparams(("parallel", "arbitrary")),
        name="premod_mm3",
    )(x, g.reshape(1, d), m_l, m_l, w, w, w)


def _premod_kernel(x_ref, g_ref, sh_ref, sc_ref, o_ref):
    _premod_rows(x_ref, g_ref, sh_ref, sc_ref, o_ref)


def _premod_time_major(x, m_l, g, *, seq_len, latent):
    t, d = x.shape
    nb = t // seq_len
    tm = min(512, seq_len)
    per_seq = seq_len // tm
    rowfn = _row_fn(tm, seq_len, latent)
    return pl.pallas_call(
        _premod_kernel,
        grid=(t // tm,),
        in_specs=[
            pl.BlockSpec((tm, d), lambda i: (i, 0)),
            pl.BlockSpec((1, d), lambda i: (0, 0)),
            _mod_spec(d, 0, rowfn),
            _mod_spec(d, 1, rowfn),
        ],
        out_specs=pl.BlockSpec((tm, d), lambda i: (i % per_seq, i // per_seq)),
        out_shape=jax.ShapeDtypeStruct((seq_len, nb * d), F32),
        compiler_params=_cparams(("parallel",)),
        name="premod",
    )(x, g.reshape(1, d), m_l, m_l)


def _mlp_kernel(x_ref, g2_ref, sh_ref, sc_ref, gate_ref, g3_ref, w1_ref, w2_ref, o_ref, h_scr, acc):
    f = pl.program_id(1)

    @pl.when(f == 0)
    def _():
        _premod_rows(x_ref, g2_ref, sh_ref, sc_ref, h_scr)
        acc[...] = jnp.zeros_like(acc)

    a = jnp.dot(h_scr[...], w1_ref[...], preferred_element_type=F32)
    a = jnp.square(jnp.maximum(a, 0.0)).astype(BF16)
    acc[...] += jnp.dot(a, w2_ref[...], preferred_element_type=F32)

    @pl.when(f == pl.num_programs(1) - 1)
    def _():
        _post_add_rows(x_ref, lambda rows: acc[rows, :], g3_ref, gate_ref, o_ref)


def _mlp(x, m_l, g2, g3, w1, w2, layer, *, seq_len, latent):
    t, d = x.shape
    dff = w1.shape[-1]
    tm = min(512, t)
    tf = 1024
    rowfn = _row_fn(tm, seq_len, latent)
    return pl.pallas_call(
        _mlp_kernel,
        grid=(t // tm, dff // tf),
        in_specs=[
            pl.BlockSpec((tm, d), lambda i, f: (i, 0)),
            pl.BlockSpec((1, d), lambda i, f: (0, 0)),
            _mod_spec(d, 3, rowfn),
            _mod_spec(d, 4, rowfn),
            _mod_spec(d, 5, rowfn),
            pl.BlockSpec((1, d), lambda i, f: (0, 0)),
            pl.BlockSpec((None, d, tf), lambda i, f: (layer, 0, f)),
            pl.BlockSpec((None, tf, d), lambda i, f: (layer, f, 0)),
        ],
        out_specs=pl.BlockSpec((tm, d), lambda i, f: (i, 0)),
        out_shape=jax.ShapeDtypeStruct((t, d), F32),
        scratch_shapes=[pltpu.VMEM((tm, d), BF16), pltpu.VMEM((tm, d), F32)],
        compiler_params=_cparams(("parallel", "arbitrary")),
        name="mlp",
    )(x, g2.reshape(1, d), m_l, m_l, m_l, g3.reshape(1, d), w1, w2)


def _mm_postadd_kernel(a_ref, w_ref, b_ref, x_ref, gate_ref, g_ref, o_ref, acc):
    acc[...] = jnp.dot(a_ref[...], w_ref[...], preferred_element_type=F32)
    b = b_ref[...]
    _post_add_rows(x_ref, lambda rows: acc[rows, :] + b, g_ref, gate_ref, o_ref)


def _mm_postadd(a, w, layer, b, x, m_l, g1, *, seq_len, latent):
    t, d = x.shape
    tm = min(512, t)
    rowfn = _row_fn(tm, seq_len, latent)
    return pl.pallas_call(
        _mm_postadd_kernel,
        grid=(t // tm,),
        in_specs=[
            pl.BlockSpec((tm, d), lambda i: (i, 0)),
            pl.BlockSpec((None, d, d), lambda i: (layer, 0, 0)),
            pl.BlockSpec((1, d), lambda i: (0, 0)),
            pl.BlockSpec((tm, d), lambda i: (i, 0)),
            _mod_spec(d, 2, rowfn),
            pl.BlockSpec((1, d), lambda i: (0, 0)),
        ],
        out_specs=pl.BlockSpec((tm, d), lambda i: (i, 0)),
        out_shape=jax.ShapeDtypeStruct((t, d), F32),
        scratch_shapes=[pltpu.VMEM((tm, d), F32)],
        compiler_params=_cparams(("parallel",)),
        name="mm_postadd",
    )(a, w, b.reshape(1, d), x, m_l, g1.reshape(1, d))


def _dft_tables(L):
    idx = np.arange(L, dtype=np.int64)
    ang = np.pi * ((idx[:, None] * idx[None, :]) % (2 * L)).astype(np.float64) / L
    c = np.cos(ang)
    s = np.sin(ang)
    alt = np.where(idx % 2 == 0, 1.0, -1.0)
    sp = s.copy()
    sp[0, :] = alt
    wf = np.full((L,), 2.0)
    wf[0] = 1.0
    ci = c * wf[None, :] / (2 * L)
    sip = -s * 2.0 / (2 * L)
    sip[:, 0] = alt / (2 * L)
    f32 = lambda a: jnp.asarray(a.astype(np.float32))
    return f32(c), f32(s), f32(sp), f32(ci), f32(sip)


def _hy_pe(L):
    t = np.arange(L, dtype=np.float64)
    periods = HY_PE_MIN_PERIOD * (HY_PE_MAX_PERIOD / HY_PE_MIN_PERIOD) ** (
        np.arange(HY_PE_BANDS, dtype=np.float64) / (HY_PE_BANDS - 1))
    ang = t[:, None] * (2.0 * math.pi / periods)[None]
    return jnp.asarray(np.concatenate([np.sin(ang), np.cos(ang)], axis=-1).astype(np.float32))


def _hy_filter_kernel(pe_ref, w1_ref, b1_ref, fr1_ref, w2_ref, b2_ref, fr2_ref, w3f_ref, w3b_ref,
                      la_ref, c_ref, s_ref, kre_ref, kim_ref):
    L, dc = kre_ref.shape
    h = jnp.sin(fr1_ref[...] * (jnp.dot(pe_ref[...], w1_ref[...], precision=HIGHEST,
                                        preferred_element_type=F32) + b1_ref[...]))
    h = jnp.sin(fr2_ref[...] * (jnp.dot(h, w2_ref[...], precision=HIGHEST,
                                        preferred_element_type=F32) + b2_ref[...]))
    row = lax.broadcasted_iota(jnp.int32, (L, dc), 0)
    dec = jnp.exp(-jnp.exp(la_ref[...]) * row.astype(F32))
    kf = jnp.dot(h, w3f_ref[...], precision=HIGHEST, preferred_element_type=F32) * dec
    kb = jnp.dot(h, w3b_ref[...], precision=HIGHEST, preferred_element_type=F32) * dec
    kb = jnp.where(row == 0, 0.0, kb)
    norm = jnp.sum(jnp.abs(kf) + jnp.abs(kb), axis=0, keepdims=True) + 1e-6
    inv = 1.0 / norm
    ks = (kf + kb) * inv
    kd = (kb - kf) * inv
    kre = jnp.dot(c_ref[...], ks, precision=HIGHEST, preferred_element_type=F32)
    kim = jnp.dot(s_ref[...], kd, precision=HIGHEST, preferred_element_type=F32)
    alt = jnp.where(row % 2 == 0, 1.0, -1.0)
    nyq = jnp.sum(alt * ks, axis=0, keepdims=True)
    kre_ref[...] = kre
    kim_ref[...] = jnp.where(row == 0, nyq, kim)


def _hy_filter(L, w1, b1, fr1, w2, b2, fr2, w3, log_alpha, c_mat, s_mat):
    d = log_alpha.shape[-1]
    fw = w1.shape[1]
    dc = 512
    nd = d // dc
    full = lambda a: pl.BlockSpec(a.shape, lambda j: (0,) * a.ndim)
    pe = _hy_pe(L)
    b1, fr1, b2, fr2 = (a.reshape(1, fw) for a in (b1, fr1, b2, fr2))
    osh = jax.ShapeDtypeStruct((L, d), F32)
    return pl.pallas_call(
        _hy_filter_kernel,
        grid=(nd,),
        in_specs=[full(pe), full(w1), full(b1), full(fr1), full(w2), full(b2), full(fr2),
                  pl.BlockSpec((fw, dc), lambda j: (0, j)),
                  pl.BlockSpec((fw, dc), lambda j: (0, nd + j)),
                  pl.BlockSpec((1, dc), lambda j: (0, j)),
                  full(c_mat), full(s_mat)],
        out_specs=[pl.BlockSpec((L, dc), lambda j: (0, j))] * 2,
        out_shape=[osh, osh],
        compiler_params=_cparams(("parallel",)),
        name="hyena_filter",
    )(pe, w1, b1, fr1, w2, b2, fr2, w3, w3, log_alpha.reshape(1, d), c_mat, s_mat)


def _hy_inproj_kernel(x_ref, g_ref, sh_ref, sc_ref, w0_ref, w1_ref, wv_ref, b0_ref, b1_ref, bv_ref,
                      s0_ref, s1_ref, sv_ref, c0_ref, c1_ref, cv_ref, ox_ref, ov_ref, h_scr, *, seq_len):
    @pl.when(pl.program_id(1) == 0)
    def _():
        _premod_rows(x_ref, g_ref, sh_ref, sc_ref, h_scr)

    h = h_scr[...]
    tm, tn = ox_ref.shape
    pos = lax.broadcasted_iota(jnp.int32, (tm, tn), 0) % seq_len
    first, last = pos == 0, pos == seq_len - 1

    def section(w_ref, b_ref, s_ref, c_ref):
        z = jnp.dot(h, w_ref[...], preferred_element_type=F32) + b_ref[...]
        zm = jnp.where(first, 0.0, pltpu.roll(z, 1, 0))
        zp = jnp.where(last, 0.0, pltpu.roll(z, tm - 1, 0))
        return zm * s_ref[0:1, :] + z * s_ref[1:2, :] + zp * s_ref[2:3, :] + c_ref[...]

    ox_ref[...] = section(w0_ref, b0_ref, s0_ref, c0_ref).astype(ox_ref.dtype)
    x1 = section(w1_ref, b1_ref, s1_ref, c1_ref)
    ov_ref[...] = (section(wv_ref, bv_ref, sv_ref, cv_ref) * x1).astype(ov_ref.dtype)


def _hy_inproj(x, m_l, g, w, layer, b, w_sh, b_sh, *, seq_len, latent):
    t, d = x.shape
    tm = min(1024, seq_len if latent else t)
    assert tm % seq_len == 0, "row tiles hold whole sequences, so the short conv needs no halo"
    tn = 512
    nj = d // tn
    rowfn = _row_fn(tm, seq_len, latent)
    wspec = lambda o: pl.BlockSpec((None, d, tn), lambda i, j: (layer, 0, o * nj + j))
    vspec = lambda rows: (lambda o: pl.BlockSpec((rows, tn), lambda i, j: (0, o * nj + j)))
    bspec, sspec = vspec(1), vspec(3)
    ospec = pl.BlockSpec((tm, tn), lambda i, j: (i, j))
    osh = jax.ShapeDtypeStruct((t, d), BF16)
    b = b.reshape(1, 3 * d)
    b_sh = b_sh.reshape(1, 3 * d)
    return pl.pallas_call(
        functools.partial(_hy_inproj_kernel, seq_len=seq_len),
        grid=(t // tm, nj),
        in_specs=[
            pl.BlockSpec((tm, d), lambda i, j: (i, 0)),
            pl.BlockSpec((1, d), lambda i, j: (0, 0)),
            _mod_spec(d, 0, rowfn),
            _mod_spec(d, 1, rowfn),
            wspec(0), wspec(1), wspec(2), bspec(0), bspec(1), bspec(2),
            sspec(0), sspec(1), sspec(2), bspec(0), bspec(1), bspec(2),
        ],
        out_specs=[ospec, ospec],
        out_shape=[osh, osh],
        scratch_shapes=[pltpu.VMEM((tm, d), BF16)],
        compiler_params=_cparams(("parallel", "arbitrary")),
        name="hyena_inproj",
    )(x, g.reshape(1, d), m_l, m_l, w, w, w, b, b, b, w_sh, w_sh, w_sh, b_sh, b_sh, b_sh)


def _hy_conv_kernel(x0_ref, v_ref, kre_ref, kim_ref, skip_ref, c_ref, sp_ref, ci_ref, sip_ref, o_ref):
    L, dc = o_ref.shape
    vb = v_ref[...]
    a = jnp.dot(c_ref[...], vb, preferred_element_type=F32)
    bm = jnp.dot(sp_ref[...], vb, preferred_element_type=F32)
    kre = kre_ref[...]
    kim = kim_ref[...]
    bk = bm * kim
    first = lax.broadcasted_iota(jnp.int32, (L, dc), 0) == 0
    yre = a * kre + jnp.where(first, 0.0, bk)
    yim = jnp.where(first, bk, a * kim - bm * kre)
    y = (jnp.dot(ci_ref[...], yre.astype(BF16), preferred_element_type=F32)
         + jnp.dot(sip_ref[...], yim.astype(BF16), preferred_element_type=F32))
    o_ref[...] = (x0_ref[...].astype(F32) * (y + vb.astype(F32) * skip_ref[...])).astype(BF16)


def _hy_conv(x0, v, kre, kim, skip, tables_bf16, *, seq_len):
    t, d = x0.shape
    L = seq_len
    nb = t // L
    dc = 512
    nd = d // dc
    once = pl.Buffered(1)
    zspec = pl.BlockSpec((L, dc), lambda j, b: (b, j))
    kspec = pl.BlockSpec((L, dc), lambda j, b: (0, j), pipeline_mode=once)
    mspec = pl.BlockSpec((L, L), lambda j, b: (0, 0), pipeline_mode=once)
    return pl.pallas_call(
        _hy_conv_kernel,
        grid=(nd, nb),
        in_specs=[zspec, zspec, kspec, kspec, pl.BlockSpec((1, dc), lambda j, b: (0, j)),
                  mspec, mspec, mspec, mspec],
        out_specs=pl.BlockSpec((L, dc), lambda j, b: (b, j)),
        out_shape=jax.ShapeDtypeStruct((t, d), BF16),
        compiler_params=_cparams(("parallel", "arbitrary")),
        name="hyena_conv",
    )(x0, v, kre, kim, skip.reshape(1, d), *tables_bf16)


def _rope_tables(L, head_dim):
    rows = L // GRID_W
    row = np.repeat(np.arange(rows), GRID_W).astype(np.float64)
    col = np.tile(np.arange(GRID_W), rows).astype(np.float64)
    half = head_dim // 2
    inv = ROPE_BASE ** (-np.arange(0, half, 2, dtype=np.float64) / half)
    ar = row[:, None] * inv
    ac = col[:, None] * inv
    ang = np.concatenate([ar, ar, ac, ac], axis=-1)
    return (jnp.asarray(np.cos(ang).astype(np.float32)), jnp.asarray(np.sin(ang).astype(np.float32)))


def _rope(x, cos, sin):
    hd = x.shape[-1]
    q = hd // 4
    lane = lax.broadcasted_iota(jnp.int32, x.shape, 1)
    rot = jnp.where((lane % (2 * q)) < q, -pltpu.roll(x, hd - q, 1), pltpu.roll(x, q, 1))
    return x * cos + rot * sin


def _diff_lambda(lam_ref, lam_init):
    lp = lam_ref[...]
    s01 = jnp.sum(lp[0:1, :] * lp[1:2, :], axis=-1, keepdims=True)
    s23 = jnp.sum(lp[2:3, :] * lp[3:4, :], axis=-1, keepdims=True)
    return jnp.exp(s01) - jnp.exp(s23) + lam_init


def _diff_attend_head(q2, k_segs, v_segs, lam, g_sub, out_scale, hd):
    scale = hd ** -0.5
    outs = []
    for comp in range(2):
        qc = (q2[comp] * scale).astype(BF16)
        ss = [lax.dot_general(qc, ks[comp], (((1,), (1,)), ((), ())), preferred_element_type=F32)
              for ks in k_segs]
        m = ss[0].max(axis=-1, keepdims=True)
        for s in ss[1:]:
            m = jnp.maximum(m, s.max(axis=-1, keepdims=True))
        den = None
        pv = None
        for s, v in zip(ss, v_segs):
            e = jnp.exp(s - m)
            dsum = e.sum(axis=-1, keepdims=True)
            den = dsum if den is None else den + dsum
            part = jnp.dot(e.astype(BF16), v, preferred_element_type=F32)
            pv = part if pv is None else pv + part
        outs.append(pv * (1.0 / den))
    o = outs[0] - lam * outs[1]
    return o * _rms(o) * (g_sub * out_scale)


def _attn_ctx_kernel(q_ref, k_ref, v_ref, lam_ref, g_ref, o_ref, *, lam_init, hd):
    lam = _diff_lambda(lam_ref, lam_init)
    vd = 2 * hd
    for h in range(N_HEADS):
        c0 = h * vd
        q2 = [q_ref[:, c0:c0 + hd], q_ref[:, c0 + hd:c0 + vd]]
        k2 = (k_ref[:, c0:c0 + hd].astype(BF16), k_ref[:, c0 + hd:c0 + vd].astype(BF16))
        v = v_ref[:, c0:c0 + vd].astype(BF16)
        o = _diff_attend_head(q2, [k2], [v], lam, g_ref[...], 1.0 - lam_init, hd)
        o_ref[:, c0:c0 + vd] = o.astype(BF16)


def _attn_ctx(q, k, v, at_lam, g_sub, *, seq_len, lam_init):
    t, d = q.shape
    hd = d // N_HEADS // 2
    spec = pl.BlockSpec((seq_len, d), lambda b: (b, 0))
    return pl.pallas_call(
        functools.partial(_attn_ctx_kernel, lam_init=lam_init, hd=hd),
        grid=(t // seq_len,),
        in_specs=[spec, spec, spec,
                  pl.BlockSpec(at_lam.shape, lambda b: (0, 0)),
                  pl.BlockSpec((1, 2 * hd), lambda b: (0, 0))],
        out_specs=spec,
        out_shape=jax.ShapeDtypeStruct((t, d), BF16),
        compiler_params=_cparams(("parallel",)),
        name="attn_ctx",
    )(q, k, v, at_lam, g_sub.reshape(1, 2 * hd))


def _attn_lat_kernel(q_ref, k_ref, v_ref, ck_ref, cv_ref, cq_ref, sq_ref, ckk_ref, skk_ref,
                     lam_ref, g_ref, o_ref, kl_scr, kc_scr, vl_scr, vc_scr, *, lam_init, hd, sub):
    @pl.when(pl.program_id(2) == 0)
    def _():
        ckk, skk = ckk_ref[...], skk_ref[...]
        kl_scr[:, 0:hd] = _rope(k_ref[:, 0:hd], ckk, skk).astype(BF16)
        kl_scr[:, hd:2 * hd] = _rope(k_ref[:, hd:2 * hd], ckk, skk).astype(BF16)
        kc_scr[...] = ck_ref[...].astype(BF16)
        vl_scr[...] = v_ref[...].astype(BF16)
        vc_scr[...] = cv_ref[...].astype(BF16)

    lam = _diff_lambda(lam_ref, lam_init)
    kl = (kl_scr[:, 0:hd], kl_scr[:, hd:2 * hd])
    kc = (kc_scr[:, 0:hd], kc_scr[:, hd:2 * hd])
    for r0 in range(0, q_ref.shape[0], sub):
        rows = slice(r0, r0 + sub)
        cq, sq = cq_ref[rows, :], sq_ref[rows, :]
        q2 = [_rope(q_ref[rows, 0:hd], cq, sq), _rope(q_ref[rows, hd:2 * hd], cq, sq)]
        o = _diff_attend_head(q2, [kc, kl], [vc_scr[...], vl_scr[...]], lam, g_ref[...],
                              1.0 - lam_init, hd)
        o_ref[rows, :] = o.astype(BF16)


def _attn_lat(q, k, v, cache_k, cache_v, at_lam, g_sub, *, seq_len, lam_init):
    t, d = q.shape
    hd = d // N_HEADS // 2
    vd = 2 * hd
    nb = t // seq_len
    past = cache_k.shape[1]
    qb = min(512, seq_len)
    nq = seq_len // qb
    cos, sin = _rope_tables(seq_len, hd)
    return pl.pallas_call(
        functools.partial(_attn_lat_kernel, lam_init=lam_init, hd=hd, sub=min(256, qb)),
        grid=(nb, N_HEADS, nq),
        in_specs=[
            pl.BlockSpec((qb, vd), lambda b, h, i: (b * nq + i, h)),
            pl.BlockSpec((seq_len, vd), lambda b, h, i: (b, h)),
            pl.BlockSpec((seq_len, vd), lambda b, h, i: (b, h)),
            pl.BlockSpec((None, past, vd), lambda b, h, i: (b, 0, h)),
            pl.BlockSpec((None, past, vd), lambda b, h, i: (b, 0, h)),
            pl.BlockSpec((qb, hd), lambda b, h, i: (i, 0)),
            pl.BlockSpec((qb, hd), lambda b, h, i: (i, 0)),
            pl.BlockSpec((seq_len, hd), lambda b, h, i: (0, 0)),
            pl.BlockSpec((seq_len, hd), lambda b, h, i: (0, 0)),
            pl.BlockSpec(at_lam.shape, lambda b, h, i: (0, 0)),
            pl.BlockSpec((1, vd), lambda b, h, i: (0, 0)),
        ],
        out_specs=pl.BlockSpec((qb, vd), lambda b, h, i: (b * nq + i, h)),
        out_shape=jax.ShapeDtypeStruct((t, d), BF16),
        scratch_shapes=[pltpu.VMEM((seq_len, vd), BF16), pltpu.VMEM((past, vd), BF16),
                        pltpu.VMEM((seq_len, vd), BF16), pltpu.VMEM((past, vd), BF16)],
        compiler_params=_cparams(("parallel", "parallel", "arbitrary")),
        name="attn_lat",
    )(q, k, v, cache_k, cache_v, cos, sin, cos, sin, at_lam, g_sub.reshape(1, vd))


def _s5_prep_kernel(lre_r, lim_r, ldt_r, lre_c, lim_c, ldt_c, btre_ref, btim_ref, ctre_ref, ctim_ref,
                    wst_ref, t_ref, wout_ref, lam_ref):
    Q = S5_CHUNK
    H = S5_GROUP
    P2 = lre_r.shape[-1]
    P = P2 // 2
    N = Q * H

    def cexp(n, re_dt, im_dt):
        mag = jnp.exp(n * re_dt)
        return mag * jnp.cos(n * im_dt), mag * jnp.sin(n * im_dt)

    re = jnp.minimum(lre_r[...], -1e-4)
    im = lim_r[...]
    dt = jnp.exp(ldt_r[...])
    re_dt, im_dt = re * dt, im * dt
    lb_re, lb_im = cexp(1.0, re_dt, im_dt)
    den = re * re + im * im
    q_re = ((lb_re - 1.0) * re + lb_im * im) / den
    q_im = (lb_im * re - (lb_re - 1.0) * im) / den
    bt_re, bt_im = btre_ref[...], btim_ref[...]
    bb_re = q_re * bt_re - q_im * bt_im
    bb_im = q_re * bt_im + q_im * bt_re
    fwd_lane = lax.broadcasted_iota(jnp.int32, (Q, P2), 1) < P
    srow = lax.broadcasted_iota(jnp.int32, (Q, P2), 0)
    n = jnp.where(fwd_lane, (Q - 1) - srow, srow).astype(F32)
    pw_re, pw_im = cexp(n, re_dt, im_dt)
    for s in range(Q):
        pr, pi = pw_re[s:s + 1, :], pw_im[s:s + 1, :]
        wst_ref[s * H:(s + 1) * H, 0:P2] = (bb_re * pr - bb_im * pi).astype(BF16)
        wst_ref[s * H:(s + 1) * H, P2:2 * P2] = (bb_re * pi + bb_im * pr).astype(BF16)
    lq_re, lq_im = cexp(float(Q), re_dt, im_dt)
    lam_ref[0:1, :] = lq_re
    lam_ref[1:2, :] = lq_im

    rec = jnp.minimum(lre_c[...], -1e-4)
    imc = lim_c[...]
    dtc = jnp.exp(ldt_c[...])
    rec_dt, imc_dt = rec * dtc, imc * dtc
    l1_re, l1_im = cexp(1.0, rec_dt, imc_dt)
    tlane = lax.broadcasted_iota(jnp.int32, (P2, N), 1) // H
    fwd_row = lax.broadcasted_iota(jnp.int32, (P2, N), 0) < P
    nt = jnp.where(fwd_row, tlane, (Q - 1) - tlane)
    g_re = jnp.ones((P2, N), F32)
    g_im = jnp.zeros((P2, N), F32)
    b_re, b_im = l1_re, l1_im
    bit = 1
    while bit < Q:
        use = (nt & bit) != 0
        f_re = jnp.where(use, b_re, 1.0)
        f_im = jnp.where(use, b_im, 0.0)
        g_re, g_im = g_re * f_re - g_im * f_im, g_re * f_im + g_im * f_re
        b_re, b_im = b_re * b_re - b_im * b_im, 2.0 * b_re * b_im
        bit *= 2
    c_re, c_im = ctre_ref[...], ctim_ref[...]
    gx_re = c_re * g_re - c_im * g_im
    gx_im = c_re * g_im + c_im * g_re
    wout_ref[0:P2, :] = (gx_re * l1_re - gx_im * l1_im).astype(BF16)
    wout_ref[P2:2 * P2, :] = (-(gx_re * l1_im + gx_im * l1_re)).astype(BF16)

    gx = jnp.concatenate([gx_re, gx_im], axis=0)
    is_f = lax.broadcasted_iota(jnp.int32, (H, P2), 1) < P
    zero = jnp.zeros((H, P2), F32)
    lhs_f = jnp.concatenate([jnp.where(is_f, bb_re, zero), jnp.where(is_f, -bb_im, zero)], axis=1)
    lhs_b = jnp.concatenate([jnp.where(is_f, zero, bb_re), jnp.where(is_f, zero, -bb_im)], axis=1)
    m_f = jnp.dot(lhs_f, gx, precision=HIGHEST, preferred_element_type=F32)
    m_b = jnp.dot(lhs_b, gx, precision=HIGHEST, preferred_element_type=F32)
    lane_n = lax.broadcasted_iota(jnp.int32, (H, N), 1)
    for s in range(Q):
        tf = m_f if s == 0 else pltpu.roll(m_f, s * H, 1)
        tb = m_b if s == Q - 1 else pltpu.roll(m_b, (s + 1) * H, 1)
        slab = jnp.where(lane_n >= s * H, tf, 0.0) + jnp.where(lane_n < (s + 1) * H, tb, 0.0)
        t_ref[s * H:(s + 1) * H, :] = slab.astype(BF16)


def _s5_prep(lam_re, lam_im, log_dt, b_re, b_im, c_re, c_im):
    _, G, P = lam_re.shape
    H = S5_GROUP
    N = S5_CHUNK * H
    P2 = 2 * P
    fb_lanes = lambda a: jnp.concatenate([a[0], a[1]], axis=-1)
    ldt = jnp.broadcast_to(log_dt[..., None], (2, G, P))
    rows = [fb_lanes(a).reshape(G, 1, P2) for a in (lam_re, lam_im, ldt)]
    cols = [fb_lanes(a).reshape(G, P2, 1) for a in (lam_re, lam_im, ldt)]
    bt = [fb_lanes(jnp.swapaxes(a, -1, -2)) for a in (b_re, b_im)]
    ct = [jnp.tile(jnp.concatenate([jnp.swapaxes(a[0], -1, -2), jnp.swapaxes(a[1], -1, -2)], axis=1),
                   (1, 1, S5_CHUNK)) for a in (c_re, c_im)]
    ins = rows + cols + bt + ct
    gspec = lambda a: pl.BlockSpec((None,) + a.shape[1:], lambda g: (g, 0, 0))
    mspec = pl.BlockSpec((None, N, N), lambda g: (g, 0, 0))
    msh = jax.ShapeDtypeStruct((G, N, N), BF16)
    return pl.pallas_call(
        _s5_prep_kernel,
        grid=(G,),
        in_specs=[gspec(a) for a in ins],
        out_specs=[mspec, mspec, mspec, pl.BlockSpec((None, 2, P2), lambda g: (g, 0, 0))],
        out_shape=[msh, msh, msh, jax.ShapeDtypeStruct((G, 2, P2), F32)],
        compiler_params=_cparams(("parallel",)),
        name="s5_prep",
    )(*ins)


def _s5_core_kernel(h_ref, wst_ref, t_ref, wout_ref, lam_ref, s0_ref, y_ref, fin_ref,
                    u_scr, loc, sa, sb, *, nb, nc, gl):
    Q, H, GB = S5_CHUNK, S5_GROUP, S5_GROUPS_PER_STEP
    R = nb * nc
    P2 = lam_ref.shape[-1]
    lane_blk = lax.broadcasted_iota(jnp.int32, (R, LANES), 1) // H

    def gather_blocks(pieces, src_blk):
        acc = None
        for b, piece in enumerate(pieces):
            shift = ((b - src_blk) % GB) * H
            r = piece if shift == 0 else pltpu.roll(piece, shift, 1)
            acc = r if acc is None else jnp.where(lane_blk == b, r, acc)
        return acc

    hi_mask = jnp.int32(-65536)

    def pack2(a, b):
        abits = lax.bitcast_convert_type(a.astype(BF16).astype(F32), jnp.int32)
        bbits = lax.bitcast_convert_type(b.astype(BF16).astype(F32), jnp.int32)
        return (abits & hi_mask) | lax.shift_right_logical(bbits, 16)

    def unpack2(p):
        return [lax.bitcast_convert_type(p & hi_mask, F32),
                lax.bitcast_convert_type(lax.shift_left(p, 16), F32)]

    packed = [pack2(h_ref[:, s, :, :].reshape(R, LANES), h_ref[:, s + GB, :, :].reshape(R, LANES))
              for s in range(GB)]
    for g in range(GB):
        u = jnp.concatenate(unpack2(gather_blocks(packed, g)), axis=-1).astype(BF16)
        u_scr[g] = u
        loc[g] = jnp.dot(u, wst_ref[g], preferred_element_type=F32)

    fwd_half = lax.broadcasted_iota(jnp.int32, (nb, P2), 1) < P2 // 2
    for g0 in range(0, GB, gl):
        def body(k, carry, g0=g0):
            rf = pl.multiple_of(k * nb, nb)
            rb = pl.multiple_of((nc - 1 - k) * nb, nb)
            out = []
            for gi in range(gl):
                g = g0 + gi
                xr, xi = carry[2 * gi], carry[2 * gi + 1]
                sa[g, pl.ds(rf, nb), 0:P2] = xr
                sa[g, pl.ds(rf, nb), P2:2 * P2] = xi
                sb[g, pl.ds(rb, nb), 0:P2] = xr
                sb[g, pl.ds(rb, nb), P2:2 * P2] = xi
                lr = jnp.where(fwd_half, loc[g, pl.ds(rf, nb), 0:P2], loc[g, pl.ds(rb, nb), 0:P2])
                li = jnp.where(fwd_half, loc[g, pl.ds(rf, nb), P2:2 * P2],
                               loc[g, pl.ds(rb, nb), P2:2 * P2])
                ar, ai = lam_ref[g, 0:1, :], lam_ref[g, 1:2, :]
                out += [ar * xr - ai * xi + lr, ar * xi + ai * xr + li]
            return tuple(out)

        init = []
        for gi in range(gl):
            init += [s0_ref[g0 + gi, :, 0:P2], s0_ref[g0 + gi, :, P2:2 * P2]]
        fin = lax.fori_loop(0, nc, body, tuple(init))
        for gi in range(gl):
            fin_ref[g0 + gi, :, 0:P2] = fin[2 * gi]
            fin_ref[g0 + gi, :, P2:2 * P2] = fin[2 * gi + 1]

    fsel = (lax.broadcasted_iota(jnp.int32, (R, 2 * P2), 1) % P2) < P2 // 2
    for g in range(GB):
        s_in = jnp.where(fsel, sa[g], sb[g]).astype(BF16)
        loc[g] = (jnp.dot(u_scr[g], t_ref[g], preferred_element_type=F32)
                  + jnp.dot(s_in, wout_ref[g], preferred_element_type=F32))
    ys = [pack2(loc[g, :, 0:LANES], loc[g, :, LANES:2 * LANES]) for g in range(GB)]
    for tl in range(GB):
        lo, hi = unpack2(gather_blocks(ys, tl))
        y_ref[:, tl, :, :] = lo.reshape(nc, nb, LANES)
        y_ref[:, tl + GB, :, :] = hi.reshape(nc, nb, LANES)


def _s5_core(h_tm, s0, wst, tmat, wout, lam, *, nb, seq_len):
    d = h_tm.shape[1] // nb
    H, Q, GB = S5_GROUP, S5_CHUNK, S5_GROUPS_PER_STEP
    G = d // H
    nc = seq_len // Q
    R = nb * nc
    N = Q * H
    P4 = s0.shape[-1]
    gl = max(1, min(GB, (8 * GB) // nb))
    h4 = h_tm.reshape(nc, Q, nb, d)
    hspec = pl.BlockSpec((nc, Q, nb, LANES), lambda gb: (0, 0, 0, gb))
    mspec = pl.BlockSpec((GB, N, N), lambda gb: (gb, 0, 0))
    sspec = pl.BlockSpec((GB, nb, P4), lambda gb: (gb, 0, 0))
    y, fin = pl.pallas_call(
        functools.partial(_s5_core_kernel, nb=nb, nc=nc, gl=gl),
        grid=(G // GB,),
        in_specs=[hspec, mspec, mspec, mspec,
                  pl.BlockSpec((GB, 2, P4 // 2), lambda gb: (gb, 0, 0)), sspec],
        out_specs=[hspec, sspec],
        out_shape=[jax.ShapeDtypeStruct((nc, Q, nb, d), F32), jax.ShapeDtypeStruct((G, nb, P4), F32)],
        scratch_shapes=[pltpu.VMEM((GB, R, N), BF16), pltpu.VMEM((GB, R, N), F32),
                        pltpu.VMEM((GB, R, P4), F32), pltpu.VMEM((GB, R, P4), F32)],
        compiler_params=_cparams(("parallel",)),
        name="s5_core",
    )(h4, wst, tmat, wout, lam, s0)
    return y.reshape(seq_len, nb * d), fin


def _glu_postadd_kernel(*refs, nsub):
    h_refs, y_refs = refs[:nsub], refs[nsub:2 * nsub]
    (d_ref, wa_ref, wg_ref, ba_ref, bg_ref, x_ref, gate_ref, g_ref, o_ref, acc_a, acc_g, u_scr) = refs[2 * nsub:]
    k = pl.program_id(1)

    @pl.when(k == 0)
    def _():
        acc_a[...] = jnp.zeros_like(acc_a)
        acc_g[...] = jnp.zeros_like(acc_g)

    rows = h_refs[0].shape[0]
    for s in range(nsub):
        u_scr[s * rows:(s + 1) * rows, :] = jax.nn.gelu(
            d_ref[...] * h_refs[s][...] + y_refs[s][...]).astype(BF16)
    u = u_scr[...]
    acc_a[...] += jnp.dot(u, wa_ref[...], preferred_element_type=F32)
    acc_g[...] += jnp.dot(u, wg_ref[...], preferred_element_type=F32)

    @pl.when(k == pl.num_programs(1) - 1)
    def _():
        ba, bg = ba_ref[...], bg_ref[...]
        _post_add_rows(x_ref, lambda rows: (acc_a[rows, :] + ba) * jax.nn.sigmoid(acc_g[rows, :] + bg),
                       g_ref, gate_ref, o_ref)


def _glu_postadd(h_tm, y_tm, dskip, w_glu, layer, b_glu, x, m_l, g1, *, seq_len, latent):
    t, d = x.shape
    tm = min(512, t)
    nsub = max(1, tm // seq_len)
    per_seq = max(1, seq_len // tm)
    sub_rows = tm // nsub
    tk = 1024
    nk = d // tk
    rowfn = _row_fn(tm, seq_len, latent)
    b_glu = b_glu.reshape(1, 2 * d)
    tspecs = [pl.BlockSpec((sub_rows, tk), lambda i, k, s=s: (i % per_seq, ((i // per_seq) * nsub + s) * nk + k))
              for s in range(nsub)]
    return pl.pallas_call(
        functools.partial(_glu_postadd_kernel, nsub=nsub),
        grid=(t // tm, nk),
        in_specs=tspecs + tspecs + [
            pl.BlockSpec((1, tk), lambda i, k: (0, k)),
            pl.BlockSpec((None, tk, d), lambda i, k: (layer, k, 0)),
            pl.BlockSpec((None, tk, d), lambda i, k: (layer, k, 1)),
            pl.BlockSpec((1, d), lambda i, k: (0, 0)),
            pl.BlockSpec((1, d), lambda i, k: (0, 1)),
            pl.BlockSpec((tm, d), lambda i, k: (i, 0)),
            _mod_spec(d, 2, rowfn),
            pl.BlockSpec((1, d), lambda i, k: (0, 0)),
        ],
        out_specs=pl.BlockSpec((tm, d), lambda i, k: (i, 0)),
        out_shape=jax.ShapeDtypeStruct((t, d), F32),
        scratch_shapes=[pltpu.VMEM((tm, d), F32), pltpu.VMEM((tm, d), F32), pltpu.VMEM((tm, tk), BF16)],
        compiler_params=_cparams(("parallel", "arbitrary")),
        name="glu_postadd",
    )(*([h_tm] * nsub), *([y_tm] * nsub), dskip.reshape(1, d), w_glu, w_glu, b_glu, b_glu, x, m_l,
      g1.reshape(1, d))


def kernel(x_prompt, x_sample, cache_attn_k, cache_attn_v, state_s5_re, state_s5_im, c, c_ctx, w_mod, b_mod, g_norm, w_mlp_in, w_mlp_out, hy_w_in, hy_b_in, hy_w_short, hy_b_short, hy_f_w1, hy_f_b1, hy_f_freq1, hy_f_w2, hy_f_b2, hy_f_freq2, hy_f_w3, hy_log_alpha, hy_skip, hy_w_out, hy_b_out, at_w_qkv, at_lam, at_g_sub, at_w_o, s5_lam_re, s5_lam_im, s5_log_dt, s5_b_re, s5_b_im, s5_c_re, s5_c_im, s5_d, s5_w_glu, s5_b_glu):
    bc, lc, d = x_prompt.shape
    bl, ll, _ = x_sample.shape
    depth = w_mod.shape[0]
    assert 1 + bl <= MOD_ROWS
    assert cache_attn_k.shape[1] == 1 and state_s5_re.shape[1] == 1, "one attention and one S5 layer"
    hd = d // N_HEADS // 2
    G = d // S5_GROUP
    P = s5_lam_re.shape[-1]

    cond = jnp.concatenate([c_ctx[None], c, jnp.zeros((MOD_ROWS - 1 - bl, d), F32)], axis=0)
    mod = _modulation(cond, w_mod, b_mod).reshape(depth, MOD_ROWS, N_MOD, 1, d)

    streams = [dict(seq_len=lc, latent=False), dict(seq_len=ll, latent=True)]
    xs = [x_prompt.reshape(bc * lc, d), x_sample.reshape(bl * ll, d)]

    tables = {}
    for L in {lc, ll}:
        cm, sm, sp, ci, sip = _dft_tables(L)
        tables[L] = (cm, sm, tuple(a.astype(BF16) for a in (cm, sp, ci, sip)))

    w_mlp_in_b, w_mlp_out_b = w_mlp_in.astype(BF16), w_mlp_out.astype(BF16)
    hy_w_in_b, hy_w_out_b = hy_w_in.astype(BF16), hy_w_out.astype(BF16)
    at_w_qkv_b, at_w_o_b = at_w_qkv.astype(BF16), at_w_o.astype(BF16)
    s5_w_glu_b = s5_w_glu.astype(BF16)

    new_k = new_v = None
    fin_ctx = None
    for i in range(depth):
        kind, j = i % 3, i // 3
        m_l = mod[i]
        g = g_norm[i]
        if kind == 0:
            filt = {}
            for L in {lc, ll}:
                filt[L] = _hy_filter(L, hy_f_w1[j], hy_f_b1[j], hy_f_freq1[j], hy_f_w2[j], hy_f_b2[j],
                                     hy_f_freq2[j], hy_f_w3[j], hy_log_alpha[j], tables[L][0], tables[L][1])
            for si, st in enumerate(streams):
                L = st["seq_len"]
                x0, vg = _hy_inproj(xs[si], m_l, g[0], hy_w_in_b, j, hy_b_in[j], hy_w_short[j],
                                    hy_b_short[j], **st)
                a = _hy_conv(x0, vg, filt[L][0], filt[L][1], hy_skip[j], tables[L][2], seq_len=L)
                xs[si] = _mm_postadd(a, hy_w_out_b, j, hy_b_out[j], xs[si], m_l, g[1], **st)
        elif kind == 1:
            lam_init = 0.8 - 0.6 * math.exp(-0.3 * i)
            for si, st in enumerate(streams):
                L = st["seq_len"]
                q, k, v = _premod_mm3(xs[si], m_l, g[0], at_w_qkv_b, j, **st)
                if not st["latent"]:
                    new_k = k.reshape(bc, 1, lc, N_HEADS, 2, hd)
                    new_v = v.reshape(bc, 1, lc, N_HEADS, 2 * hd)
                    a = _attn_ctx(q, k, v, at_lam[j], at_g_sub[j], seq_len=L, lam_init=lam_init)
                else:
                    ck = cache_attn_k[:, j].reshape(bl, -1, d)
                    cv = cache_attn_v[:, j].reshape(bl, -1, d)
                    a = _attn_lat(q, k, v, ck, cv, at_lam[j], at_g_sub[j], seq_len=L, lam_init=lam_init)
                xs[si] = _mm_postadd(a, at_w_o_b, j, jnp.zeros((d,), F32), xs[si], m_l, g[1], **st)
        else:
            wst, tmat, wout, lam_q = _s5_prep(s5_lam_re[j], s5_lam_im[j], s5_log_dt[j], s5_b_re[j],
                                              s5_b_im[j], s5_c_re[j], s5_c_im[j])
            for si, st in enumerate(streams):
                L = st["seq_len"]
                nb = xs[si].shape[0] // L
                if st["latent"]:
                    sre, sim = state_s5_re[:, j], state_s5_im[:, j]
                    s0 = jnp.concatenate([sre[:, 0], sre[:, 1], sim[:, 0], sim[:, 1]], axis=-1)
                    s0 = s0.transpose(1, 0, 2)
                else:
                    s0 = jnp.zeros((G, nb, 4 * P), F32)
                h_tm = _premod_time_major(xs[si], m_l, g[0], **st)
                y_tm, fin = _s5_core(h_tm, s0, wst, tmat, wout, lam_q, nb=nb, seq_len=L)
                if not st["latent"]:
                    fin_ctx = fin.reshape(G, nb, 2, 2, P).transpose(1, 2, 3, 0, 4)
                xs[si] = _glu_postadd(h_tm, y_tm, s5_d[j], s5_w_glu_b, j, s5_b_glu[j], xs[si], m_l, g[1], **st)
        for si, st in enumerate(streams):
            xs[si] = _mlp(xs[si], m_l, g[2], g[3], w_mlp_in_b, w_mlp_out_b, i, **st)

    new_s_re = fin_ctx[:, 0][:, None]
    new_s_im = fin_ctx[:, 1][:, None]
    return (xs[0].reshape(bc, lc, d), xs[1].reshape(bl, ll, d), new_k, new_v, new_s_re, new_s_im)
```

```python
import functools
import math

import numpy as np
import jax
import jax.numpy as jnp
from jax import lax
from jax.experimental import pallas as pl
from jax.experimental.pallas import tpu as pltpu

F32 = jnp.float32
BF16 = jnp.bfloat16
HIGHEST = lax.Precision.HIGHEST

NORM_EPS = 1e-6
N_MOD = 6
N_HEADS = 8
GRID_W = 64
ROPE_BASE = 10000.0
HY_PE_BANDS = 16
HY_PE_MIN_PERIOD = 2.0
HY_PE_MAX_PERIOD = 4096.0
S5_GROUP = 16
S5_CHUNK = 16
LANES = 128
S5_GROUPS_PER_STEP = LANES // S5_GROUP
MOD_ROWS = 16
MLP_TF = 1024
PROJ_TN = 512

VMEM_LIMIT = 56 * 1024 * 1024


def _cparams(sem):
    return pltpu.CompilerParams(dimension_semantics=sem, vmem_limit_bytes=VMEM_LIMIT)


def _rms(x):
    return lax.rsqrt(jnp.mean(x * x, axis=-1, keepdims=True) + NORM_EPS)


def _split_bf16(a):
    hi = a.astype(BF16)
    return hi, (a - hi.astype(F32)).astype(BF16)


def _cast_kernel(w_ref, o_ref):
    o_ref[...] = w_ref[...].astype(BF16)


def _cast_tiles(w, tc):
    n, k, m = w.shape
    rk = min(k, (2 * 1024 * 1024) // tc)
    return pl.pallas_call(
        _cast_kernel,
        grid=(n, m // tc, k // rk),
        in_specs=[pl.BlockSpec((None, rk, tc), lambda l, j, r: (l, r, j))],
        out_specs=pl.BlockSpec((None, None, rk, tc), lambda l, j, r: (l, j, r, 0)),
        out_shape=jax.ShapeDtypeStruct((n, m // tc, k, tc), BF16),
        compiler_params=_cparams(("parallel", "parallel", "parallel")),
        name="cast_bf16",
    )(w)


def _mod_kernel(c_ref, w_ref, b_ref, o_ref):
    c = c_ref[...]
    s_hi, s_lo = _split_bf16(c * jax.nn.sigmoid(c))
    both = jnp.dot(jnp.concatenate([s_hi, s_lo], axis=0), w_ref[...].astype(BF16),
                   preferred_element_type=F32)
    o_ref[...] = both[:MOD_ROWS] + both[MOD_ROWS:] + b_ref[...]


def _modulation(cond, w_mod, b_mod):
    depth, d, n = w_mod.shape
    tn = 1024
    return pl.pallas_call(
        _mod_kernel,
        grid=(depth, n // tn),
        in_specs=[
            pl.BlockSpec((MOD_ROWS, d), lambda l, j: (0, 0)),
            pl.BlockSpec((None, d, tn), lambda l, j: (l, 0, j)),
            pl.BlockSpec((None, 1, tn), lambda l, j: (l, 0, j)),
        ],
        out_specs=pl.BlockSpec((None, MOD_ROWS, tn), lambda l, j: (l, 0, j)),
        out_shape=jax.ShapeDtypeStruct((depth, MOD_ROWS, n), F32),
        compiler_params=_cparams(("parallel", "parallel")),
        name="adaln_mod",
    )(cond, w_mod, b_mod.reshape(depth, 1, n))


def _mod_spec(d, which, rowfn):
    return pl.BlockSpec((None, None, 1, d), lambda i, *_: (rowfn(i), which, 0, 0))


def _row_fn(tm, seq_len, latent):
    if latent:
        assert seq_len % tm == 0, "a latent row tile must sit inside one sequence"
        return lambda i: 1 + (i * tm) // seq_len
    return lambda i: 0


ROW_CHUNK = 16


def _for_row_chunks(n_rows, fn):
    def body(c, carry):
        fn(pl.ds(pl.multiple_of(c * ROW_CHUNK, ROW_CHUNK), ROW_CHUNK))
        return carry

    lax.fori_loop(0, n_rows // ROW_CHUNK, body, 0, unroll=16)


def _premod_rows(x_ref, g_ref, sh_ref, sc_ref, out_ref):
    gs, sh = g_ref[...] * (1.0 + sc_ref[...]), sh_ref[...]

    def rows_fn(rows):
        x = x_ref[rows, :]
        out_ref[rows, :] = ((x * _rms(x)) * gs + sh).astype(out_ref.dtype)

    _for_row_chunks(x_ref.shape[0], rows_fn)


def _post_add_rows(x_ref, o_fn, g_ref, gate_ref, out_ref):
    gg = gate_ref[...] * g_ref[...]

    def rows_fn(rows):
        o = o_fn(rows)
        out_ref[rows, :] = x_ref[rows, :] + (o * _rms(o)) * gg

    _for_row_chunks(x_ref.shape[0], rows_fn)


def _premod_mm3_kernel(x_ref, g_ref, sh_ref, sc_ref, wa_ref, wb_ref, wc_ref, oa_ref, ob_ref, oc_ref, h_scr):
    @pl.when(pl.program_id(1) == 0)
    def _():
        _premod_rows(x_ref, g_ref, sh_ref, sc_ref, h_scr)

    h = h_scr[...]
    for w_ref, o_ref in ((wa_ref, oa_ref), (wb_ref, ob_ref), (wc_ref, oc_ref)):
        o_ref[...] = jnp.dot(h, w_ref[...], preferred_element_type=F32)


def _premod_mm3(x, m_l, g, w, layer, *, seq_len, latent):
    t, d = x.shape
    tm = min(1024, seq_len if latent else t)
    tn = PROJ_TN
    nj = d // tn
    rowfn = _row_fn(tm, seq_len, latent)
    wspec = lambda o: pl.BlockSpec((None, None, d, tn), lambda i, j: (layer, o * nj + j, 0, 0))
    ospec = pl.BlockSpec((tm, tn), lambda i, j: (i, j))
    osh = jax.ShapeDtypeStruct((t, d), F32)
    return pl.pallas_call(
        _premod_mm3_kernel,
        grid=(t // tm, nj),
        in_specs=[
            pl.BlockSpec((tm, d), lambda i, j: (i, 0)),
            pl.BlockSpec((1, d), lambda i, j: (0, 0)),
            _mod_spec(d, 0, rowfn),
            _mod_spec(d, 1, rowfn),
            wspec(0), wspec(1), wspec(2),
        ],
        out_specs=[ospec, ospec, ospec],
        out_shape=[osh, osh, osh],
        scratch_shapes=[pltpu.VMEM((tm, d), BF16)],
        compiler_params=_cparams(("parallel", "arbitrary")),
        name="premod_mm3",
    )(x, g.reshape(1, d), m_l, m_l, w, w, w)


def _premod_kernel(x_ref, g_ref, sh_ref, sc_ref, o_ref):
    _premod_rows(x_ref, g_ref, sh_ref, sc_ref, o_ref)


def _premod_time_major(x, m_l, g, *, seq_len, latent):
    t, d = x.shape
    nb = t // seq_len
    tm = min(512, seq_len)
    per_seq = seq_len // tm
    rowfn = _row_fn(tm, seq_len, latent)
    return pl.pallas_call(
        _premod_kernel,
        grid=(t // tm,),
        in_specs=[
            pl.BlockSpec((tm, d), lambda i: (i, 0)),
            pl.BlockSpec((1, d), lambda i: (0, 0)),
            _mod_spec(d, 0, rowfn),
            _mod_spec(d, 1, rowfn),
        ],
        out_specs=pl.BlockSpec((tm, d), lambda i: (i % per_seq, i // per_seq)),
        out_shape=jax.ShapeDtypeStruct((seq_len, nb * d), F32),
        compiler_params=_cparams(("parallel",)),
        name="premod",
    )(x, g.reshape(1, d), m_l, m_l)


def _mlp_kernel(x_ref, g2_ref, sh_ref, sc_ref, gate_ref, g3_ref, w1_ref, w2_ref, o_ref, h_scr, acc):
    f = pl.program_id(1)

    @pl.when(f == 0)
    def _():
        _premod_rows(x_ref, g2_ref, sh_ref, sc_ref, h_scr)
        acc[...] = jnp.zeros_like(acc)

    a = jnp.dot(h_scr[...], w1_ref[...], preferred_element_type=F32)
    a = jnp.square(jnp.maximum(a, 0.0)).astype(BF16)
    acc[...] += jnp.dot(a, w2_ref[...], preferred_element_type=F32)

    @pl.when(f == pl.num_programs(1) - 1)
    def _():
        _post_add_rows(x_ref, lambda rows: acc[rows, :], g3_ref, gate_ref, o_ref)


def _mlp(x, m_l, g2, g3, w1, w2, layer, *, seq_len, latent):
    t, d = x.shape
    tf = MLP_TF
    dff = w1.shape[1] * tf
    tm = min(512, t)
    rowfn = _row_fn(tm, seq_len, latent)
    return pl.pallas_call(
        _mlp_kernel,
        grid=(t // tm, dff // tf),
        in_specs=[
            pl.BlockSpec((tm, d), lambda i, f: (i, 0)),
            pl.BlockSpec((1, d), lambda i, f: (0, 0)),
            _mod_spec(d, 3, rowfn),
            _mod_spec(d, 4, rowfn),
            _mod_spec(d, 5, rowfn),
            pl.BlockSpec((1, d), lambda i, f: (0, 0)),
            pl.BlockSpec((None, None, d, tf), lambda i, f: (layer, f, 0, 0)),
            pl.BlockSpec((None, tf, d), lambda i, f: (layer, f, 0)),
        ],
        out_specs=pl.BlockSpec((tm, d), lambda i, f: (i, 0)),
        out_shape=jax.ShapeDtypeStruct((t, d), F32),
        scratch_shapes=[pltpu.VMEM((tm, d), BF16), pltpu.VMEM((tm, d), F32)],
        compiler_params=_cparams(("parallel", "arbitrary")),
        name="mlp",
    )(x, g2.reshape(1, d), m_l, m_l, m_l, g3.reshape(1, d), w1, w2)


def _mm_postadd_kernel(a_ref, w_ref, b_ref, x_ref, gate_ref, g_ref, o_ref, acc):
    acc[...] = jnp.dot(a_ref[...], w_ref[...], preferred_element_type=F32)
    b = b_ref[...]
    _post_add_rows(x_ref, lambda rows: acc[rows, :] + b, g_ref, gate_ref, o_ref)


def _mm_postadd(a, w, layer, b, x, m_l, g1, *, seq_len, latent):
    t, d = x.shape
    tm = min(512, t)
    rowfn = _row_fn(tm, seq_len, latent)
    return pl.pallas_call(
        _mm_postadd_kernel,
        grid=(t // tm,),
        in_specs=[
            pl.BlockSpec((tm, d), lambda i: (i, 0)),
            pl.BlockSpec((None, d, d), lambda i: (layer, 0, 0)),
            pl.BlockSpec((1, d), lambda i: (0, 0)),
            pl.BlockSpec((tm, d), lambda i: (i, 0)),
            _mod_spec(d, 2, rowfn),
            pl.BlockSpec((1, d), lambda i: (0, 0)),
        ],
        out_specs=pl.BlockSpec((tm, d), lambda i: (i, 0)),
        out_shape=jax.ShapeDtypeStruct((t, d), F32),
        scratch_shapes=[pltpu.VMEM((tm, d), F32)],
        compiler_params=_cparams(("parallel",)),
        name="mm_postadd",
    )(a, w, b.reshape(1, d), x, m_l, g1.reshape(1, d))


def _dft_tables(L):
    idx = np.arange(L, dtype=np.int64)
    ang = np.pi * ((idx[:, None] * idx[None, :]) % (2 * L)).astype(np.float64) / L
    c = np.cos(ang)
    s = np.sin(ang)
    alt = np.where(idx % 2 == 0, 1.0, -1.0)
    sp = s.copy()
    sp[0, :] = alt
    wf = np.full((L,), 2.0)
    wf[0] = 1.0
    ci = c * wf[None, :] / (2 * L)
    sip = -s * 2.0 / (2 * L)
    sip[:, 0] = alt / (2 * L)
    f32 = lambda a: jnp.asarray(a.astype(np.float32))
    return f32(c), f32(s), f32(sp), f32(ci), f32(sip)


def _hy_pe(L):
    t = np.arange(L, dtype=np.float64)
    periods = HY_PE_MIN_PERIOD * (HY_PE_MAX_PERIOD / HY_PE_MIN_PERIOD) ** (
        np.arange(HY_PE_BANDS, dtype=np.float64) / (HY_PE_BANDS - 1))
    ang = t[:, None] * (2.0 * math.pi / periods)[None]
    return jnp.asarray(np.concatenate([np.sin(ang), np.cos(ang)], axis=-1).astype(np.float32))


def _dot_bf16x3(a_hi, a_lo, b):
    b_hi, b_lo = _split_bf16(b)
    n = b.shape[1]
    both = jnp.dot(a_hi, jnp.concatenate([b_hi, b_lo], axis=1), preferred_element_type=F32)
    return both[:, :n] + both[:, n:] + jnp.dot(a_lo, b_hi, preferred_element_type=F32)


def _hy_filter_kernel(pe_ref, w1_ref, b1_ref, fr1_ref, w2_ref, b2_ref, fr2_ref, w3f_ref, w3b_ref,
                      la_ref, chi_ref, clo_ref, shi_ref, slo_ref, kre_ref, kim_ref, h_scr):
    L, dc = kre_ref.shape

    @pl.when(pl.program_id(0) == 0)
    def _():
        h1 = jnp.sin(fr1_ref[...] * (jnp.dot(pe_ref[...], w1_ref[...], precision=HIGHEST,
                                             preferred_element_type=F32) + b1_ref[...]))
        h_scr[...] = jnp.sin(fr2_ref[...] * (jnp.dot(h1, w2_ref[...], precision=HIGHEST,
                                                     preferred_element_type=F32) + b2_ref[...]))

    h = h_scr[...]
    row = lax.broadcasted_iota(jnp.int32, (L, dc), 0)
    dec = jnp.exp(-jnp.exp(la_ref[...]) * row.astype(F32))
    kf = jnp.dot(h, w3f_ref[...], precision=HIGHEST, preferred_element_type=F32) * dec
    kb = jnp.dot(h, w3b_ref[...], precision=HIGHEST, preferred_element_type=F32) * dec
    kb = jnp.where(row == 0, 0.0, kb)
    norm = jnp.sum(jnp.abs(kf) + jnp.abs(kb), axis=0, keepdims=True) + 1e-6
    inv = 1.0 / norm
    ks = (kf + kb) * inv
    kd = (kb - kf) * inv
    kre = _dot_bf16x3(chi_ref[...], clo_ref[...], ks)
    kim = _dot_bf16x3(shi_ref[...], slo_ref[...], kd)
    alt = jnp.where(row % 2 == 0, 1.0, -1.0)
    nyq = jnp.sum(alt * ks, axis=0, keepdims=True)
    kre_ref[...] = kre
    kim_ref[...] = jnp.where(row == 0, nyq, kim)


def _hy_filter(L, w1, b1, fr1, w2, b2, fr2, w3, log_alpha, cs_split):
    d = log_alpha.shape[-1]
    fw = w1.shape[1]
    dc = 512
    nd = d // dc
    full = lambda a: pl.BlockSpec(a.shape, lambda j: (0,) * a.ndim)
    pe = _hy_pe(L)
    b1, fr1, b2, fr2 = (a.reshape(1, fw) for a in (b1, fr1, b2, fr2))
    osh = jax.ShapeDtypeStruct((L, d), F32)
    return pl.pallas_call(
        _hy_filter_kernel,
        grid=(nd,),
        in_specs=[full(pe), full(w1), full(b1), full(fr1), full(w2), full(b2), full(fr2),
                  pl.BlockSpec((fw, dc), lambda j: (0, j)),
                  pl.BlockSpec((fw, dc), lambda j: (0, nd + j)),
                  pl.BlockSpec((1, dc), lambda j: (0, j))] + [full(a) for a in cs_split],
        out_specs=[pl.BlockSpec((L, dc), lambda j: (0, j))] * 2,
        out_shape=[osh, osh],
        scratch_shapes=[pltpu.VMEM((L, fw), F32)],
        compiler_params=_cparams(("arbitrary",)),
        name="hyena_filter",
    )(pe, w1, b1, fr1, w2, b2, fr2, w3, w3, log_alpha.reshape(1, d), *cs_split)


def _hy_inproj_kernel(x_ref, g_ref, sh_ref, sc_ref, w0_ref, w1_ref, wv_ref, b0_ref, b1_ref, bv_ref,
                      s0_ref, s1_ref, sv_ref, c0_ref, c1_ref, cv_ref, ox_ref, ov_ref, h_scr, *, seq_len):
    @pl.when(pl.program_id(1) == 0)
    def _():
        _premod_rows(x_ref, g_ref, sh_ref, sc_ref, h_scr)

    h = h_scr[...]
    tm, tn = ox_ref.shape
    pos = lax.broadcasted_iota(jnp.int32, (tm, tn), 0) % seq_len
    first, last = pos == 0, pos == seq_len - 1

    def section(w_ref, b_ref, s_ref, c_ref):
        z = jnp.dot(h, w_ref[...], preferred_element_type=F32) + b_ref[...]
        zm = jnp.where(first, 0.0, pltpu.roll(z, 1, 0))
        zp = jnp.where(last, 0.0, pltpu.roll(z, tm - 1, 0))
        return zm * s_ref[0:1, :] + z * s_ref[1:2, :] + zp * s_ref[2:3, :] + c_ref[...]

    ox_ref[...] = section(w0_ref, b0_ref, s0_ref, c0_ref).astype(ox_ref.dtype)
    x1 = section(w1_ref, b1_ref, s1_ref, c1_ref)
    ov_ref[...] = (section(wv_ref, bv_ref, sv_ref, cv_ref) * x1).astype(ov_ref.dtype)


def _hy_inproj(x, m_l, g, w, layer, b, w_sh, b_sh, *, seq_len, latent):
    t, d = x.shape
    tm = min(1024, seq_len if latent else t)
    assert tm % seq_len == 0, "row tiles hold whole sequences, so the short conv needs no halo"
    tn = PROJ_TN
    nj = d // tn
    rowfn = _row_fn(tm, seq_len, latent)
    wspec = lambda o: pl.BlockSpec((None, None, d, tn), lambda i, j: (layer, o * nj + j, 0, 0))
    vspec = lambda rows: (lambda o: pl.BlockSpec((rows, tn), lambda i, j: (0, o * nj + j)))
    bspec, sspec = vspec(1), vspec(3)
    ospec = pl.BlockSpec((tm, tn), lambda i, j: (i, j))
    osh = jax.ShapeDtypeStruct((t, d), BF16)
    b = b.reshape(1, 3 * d)
    b_sh = b_sh.reshape(1, 3 * d)
    return pl.pallas_call(
        functools.partial(_hy_inproj_kernel, seq_len=seq_len),
        grid=(t // tm, nj),
        in_specs=[
            pl.BlockSpec((tm, d), lambda i, j: (i, 0)),
            pl.BlockSpec((1, d), lambda i, j: (0, 0)),
            _mod_spec(d, 0, rowfn),
            _mod_spec(d, 1, rowfn),
            wspec(0), wspec(1), wspec(2), bspec(0), bspec(1), bspec(2),
            sspec(0), sspec(1), sspec(2), bspec(0), bspec(1), bspec(2),
        ],
        out_specs=[ospec, ospec],
        out_shape=[osh, osh],
        scratch_shapes=[pltpu.VMEM((tm, d), BF16)],
        compiler_params=_cparams(("parallel", "arbitrary")),
        name="hyena_inproj",
    )(x, g.reshape(1, d), m_l, m_l, w, w, w, b, b, b, w_sh, w_sh, w_sh, b_sh, b_sh, b_sh)


def _hy_conv_kernel(x0_ref, v_ref, kre_ref, kim_ref, skip_ref, c_ref, sp_ref, ci_ref, sip_ref, o_ref):
    L, dc = o_ref.shape
    vb = v_ref[...]
    a = jnp.dot(c_ref[...], vb, preferred_element_type=F32)
    bm = jnp.dot(sp_ref[...], vb, preferred_element_type=F32)
    kre = kre_ref[...]
    kim = kim_ref[...]
    bk = bm * kim
    first = lax.broadcasted_iota(jnp.int32, (L, dc), 0) == 0
    yre = a * kre + jnp.where(first, 0.0, bk)
    yim = jnp.where(first, bk, a * kim - bm * kre)
    y = (jnp.dot(ci_ref[...], yre.astype(BF16), preferred_element_type=F32)
         + jnp.dot(sip_ref[...], yim.astype(BF16), preferred_element_type=F32))
    o_ref[...] = (x0_ref[...].astype(F32) * (y + vb.astype(F32) * skip_ref[...])).astype(BF16)


def _hy_conv(x0, v, kre, kim, skip, tables_bf16, *, seq_len):
    t, d = x0.shape
    L = seq_len
    nb = t // L
    dc = 512
    nd = d // dc
    once = pl.Buffered(1)
    zspec = pl.BlockSpec((L, dc), lambda j, b: (b, j))
    kspec = pl.BlockSpec((L, dc), lambda j, b: (0, j), pipeline_mode=once)
    mspec = pl.BlockSpec((L, L), lambda j, b: (0, 0), pipeline_mode=once)
    return pl.pallas_call(
        _hy_conv_kernel,
        grid=(nd, nb),
        in_specs=[zspec, zspec, kspec, kspec, pl.BlockSpec((1, dc), lambda j, b: (0, j)),
                  mspec, mspec, mspec, mspec],
        out_specs=pl.BlockSpec((L, dc), lambda j, b: (b, j)),
        out_shape=jax.ShapeDtypeStruct((t, d), BF16),
        compiler_params=_cparams(("parallel", "arbitrary")),
        name="hyena_conv",
    )(x0, v, kre, kim, skip.reshape(1, d), *tables_bf16)


def _rope_tables(L, head_dim):
    rows = L // GRID_W
    row = np.repeat(np.arange(rows), GRID_W).astype(np.float64)
    col = np.tile(np.arange(GRID_W), rows).astype(np.float64)
    half = head_dim // 2
    inv = ROPE_BASE ** (-np.arange(0, half, 2, dtype=np.float64) / half)
    ar = row[:, None] * inv
    ac = col[:, None] * inv
    ang = np.concatenate([ar, ar, ac, ac], axis=-1)
    return (jnp.asarray(np.cos(ang).astype(np.float32)), jnp.asarray(np.sin(ang).astype(np.float32)))


def _rope(x, cos, sin):
    hd = x.shape[-1]
    q = hd // 4
    lane = lax.broadcasted_iota(jnp.int32, x.shape, 1)
    rot = jnp.where((lane % (2 * q)) < q, -pltpu.roll(x, hd - q, 1), pltpu.roll(x, q, 1))
    return x * cos + rot * sin


def _diff_lambda(lam_ref, lam_init):
    lp = lam_ref[...]
    s01 = jnp.sum(lp[0:1, :] * lp[1:2, :], axis=-1, keepdims=True)
    s23 = jnp.sum(lp[2:3, :] * lp[3:4, :], axis=-1, keepdims=True)
    return jnp.exp(s01) - jnp.exp(s23) + lam_init


def _diff_attend_head(q2, k_segs, v_segs, lam, g_sub, out_scale, hd):
    scale = hd ** -0.5
    outs = []
    for comp in range(2):
        qc = (q2[comp] * scale).astype(BF16)
        ss = [lax.dot_general(qc, ks[comp], (((1,), (1,)), ((), ())), preferred_element_type=F32)
              for ks in k_segs]
        m = ss[0].max(axis=-1, keepdims=True)
        for s in ss[1:]:
            m = jnp.maximum(m, s.max(axis=-1, keepdims=True))
        den = None
        pv = None
        for s, v in zip(ss, v_segs):
            e = jnp.exp(s - m)
            dsum = e.sum(axis=-1, keepdims=True)
            den = dsum if den is None else den + dsum
            part = jnp.dot(e.astype(BF16), v, preferred_element_type=F32)
            pv = part if pv is None else pv + part
        outs.append(pv * (1.0 / den))
    o = outs[0] - lam * outs[1]
    return o * _rms(o) * (g_sub * out_scale)


def _attn_ctx_kernel(q_ref, k_ref, v_ref, lam_ref, g_ref, o_ref, *, lam_init, hd):
    lam = _diff_lambda(lam_ref, lam_init)
    vd = 2 * hd
    for h in range(N_HEADS):
        c0 = h * vd
        q2 = [q_ref[:, c0:c0 + hd], q_ref[:, c0 + hd:c0 + vd]]
        k2 = (k_ref[:, c0:c0 + hd].astype(BF16), k_ref[:, c0 + hd:c0 + vd].astype(BF16))
        v = v_ref[:, c0:c0 + vd].astype(BF16)
        o = _diff_attend_head(q2, [k2], [v], lam, g_ref[...], 1.0 - lam_init, hd)
        o_ref[:, c0:c0 + vd] = o.astype(BF16)


def _attn_ctx(q, k, v, at_lam, g_sub, *, seq_len, lam_init):
    t, d = q.shape
    hd = d // N_HEADS // 2
    spec = pl.BlockSpec((seq_len, d), lambda b: (b, 0))
    return pl.pallas_call(
        functools.partial(_attn_ctx_kernel, lam_init=lam_init, hd=hd),
        grid=(t // seq_len,),
        in_specs=[spec, spec, spec,
                  pl.BlockSpec(at_lam.shape, lambda b: (0, 0)),
                  pl.BlockSpec((1, 2 * hd), lambda b: (0, 0))],
        out_specs=spec,
        out_shape=jax.ShapeDtypeStruct((t, d), BF16),
        compiler_params=_cparams(("parallel",)),
        name="attn_ctx",
    )(q, k, v, at_lam, g_sub.reshape(1, 2 * hd))


def _attn_lat_kernel(q_ref, k_ref, v_ref, ck_ref, cv_ref, cq_ref, sq_ref, ckk_ref, skk_ref,
                     lam_ref, g_ref, o_ref, kl_scr, kc_scr, vl_scr, vc_scr, *, lam_init, hd, sub):
    @pl.when(pl.program_id(2) == 0)
    def _():
        ckk, skk = ckk_ref[...], skk_ref[...]
        kl_scr[:, 0:hd] = _rope(k_ref[:, 0:hd], ckk, skk).astype(BF16)
        kl_scr[:, hd:2 * hd] = _rope(k_ref[:, hd:2 * hd], ckk, skk).astype(BF16)
        kc_scr[...] = ck_ref[...].astype(BF16)
        vl_scr[...] = v_ref[...].astype(BF16)
        vc_scr[...] = cv_ref[...].astype(BF16)

    lam = _diff_lambda(lam_ref, lam_init)
    kl = (kl_scr[:, 0:hd], kl_scr[:, hd:2 * hd])
    kc = (kc_scr[:, 0:hd], kc_scr[:, hd:2 * hd])
    for r0 in range(0, q_ref.shape[0], sub):
        rows = slice(r0, r0 + sub)
        cq, sq = cq_ref[rows, :], sq_ref[rows, :]
        q2 = [_rope(q_ref[rows, 0:hd], cq, sq), _rope(q_ref[rows, hd:2 * hd], cq, sq)]
        o = _diff_attend_head(q2, [kc, kl], [vc_scr[...], vl_scr[...]], lam, g_ref[...],
                              1.0 - lam_init, hd)
        o_ref[rows, :] = o.astype(BF16)


def _attn_lat(q, k, v, cache_k, cache_v, at_lam, g_sub, *, seq_len, lam_init):
    t, d = q.shape
    hd = d // N_HEADS // 2
    vd = 2 * hd
    nb = t // seq_len
    past = cache_k.shape[1]
    qb = min(512, seq_len)
    nq = seq_len // qb
    cos, sin = _rope_tables(seq_len, hd)
    return pl.pallas_call(
        functools.partial(_attn_lat_kernel, lam_init=lam_init, hd=hd, sub=min(256, qb)),
        grid=(nb, N_HEADS, nq),
        in_specs=[
            pl.BlockSpec((qb, vd), lambda b, h, i: (b * nq + i, h)),
            pl.BlockSpec((seq_len, vd), lambda b, h, i: (b, h)),
            pl.BlockSpec((seq_len, vd), lambda b, h, i: (b, h)),
            pl.BlockSpec((None, past, vd), lambda b, h, i: (b, 0, h)),
            pl.BlockSpec((None, past, vd), lambda b, h, i: (b, 0, h)),
            pl.BlockSpec((qb, hd), lambda b, h, i: (i, 0)),
            pl.BlockSpec((qb, hd), lambda b, h, i: (i, 0)),
            pl.BlockSpec((seq_len, hd), lambda b, h, i: (0, 0)),
            pl.BlockSpec((seq_len, hd), lambda b, h, i: (0, 0)),
            pl.BlockSpec(at_lam.shape, lambda b, h, i: (0, 0)),
            pl.BlockSpec((1, vd), lambda b, h, i: (0, 0)),
        ],
        out_specs=pl.BlockSpec((qb, vd), lambda b, h, i: (b * nq + i, h)),
        out_shape=jax.ShapeDtypeStruct((t, d), BF16),
        scratch_shapes=[pltpu.VMEM((seq_len, vd), BF16), pltpu.VMEM((past, vd), BF16),
                        pltpu.VMEM((seq_len, vd), BF16), pltpu.VMEM((past, vd), BF16)],
        compiler_params=_cparams(("parallel", "parallel", "arbitrary")),
        name="attn_lat",
    )(q, k, v, cache_k, cache_v, cos, sin, cos, sin, at_lam, g_sub.reshape(1, vd))


def _s5_prep_kernel(lre_r, lim_r, ldt_r, lre_c, lim_c, ldt_c, btre_ref, btim_ref, ctre_ref, ctim_ref,
                    wst_ref, t_ref, wout_ref, lam_ref):
    Q = S5_CHUNK
    H = S5_GROUP
    P2 = lre_r.shape[-1]
    P = P2 // 2
    N = Q * H

    def cexp(n, re_dt, im_dt):
        mag = jnp.exp(n * re_dt)
        return mag * jnp.cos(n * im_dt), mag * jnp.sin(n * im_dt)

    re = jnp.minimum(lre_r[...], -1e-4)
    im = lim_r[...]
    dt = jnp.exp(ldt_r[...])
    re_dt, im_dt = re * dt, im * dt
    lb_re, lb_im = cexp(1.0, re_dt, im_dt)
    den = re * re + im * im
    q_re = ((lb_re - 1.0) * re + lb_im * im) / den
    q_im = (lb_im * re - (lb_re - 1.0) * im) / den
    bt_re, bt_im = btre_ref[...], btim_ref[...]
    bb_re = q_re * bt_re - q_im * bt_im
    bb_im = q_re * bt_im + q_im * bt_re
    fwd_lane = lax.broadcasted_iota(jnp.int32, (Q, P2), 1) < P
    srow = lax.broadcasted_iota(jnp.int32, (Q, P2), 0)
    n = jnp.where(fwd_lane, (Q - 1) - srow, srow).astype(F32)
    pw_re, pw_im = cexp(n, re_dt, im_dt)
    for s in range(Q):
        pr, pi = pw_re[s:s + 1, :], pw_im[s:s + 1, :]
        wst_ref[s * H:(s + 1) * H, 0:P2] = (bb_re * pr - bb_im * pi).astype(BF16)
        wst_ref[s * H:(s + 1) * H, P2:2 * P2] = (bb_re * pi + bb_im * pr).astype(BF16)
    lq_re, lq_im = cexp(float(Q), re_dt, im_dt)
    lam_ref[0:1, :] = lq_re
    lam_ref[1:2, :] = lq_im

    rec = jnp.minimum(lre_c[...], -1e-4)
    imc = lim_c[...]
    dtc = jnp.exp(ldt_c[...])
    rec_dt, imc_dt = rec * dtc, imc * dtc
    l1_re, l1_im = cexp(1.0, rec_dt, imc_dt)
    tlane = lax.broadcasted_iota(jnp.int32, (P2, N), 1) // H
    fwd_row = lax.broadcasted_iota(jnp.int32, (P2, N), 0) < P
    nt = jnp.where(fwd_row, tlane, (Q - 1) - tlane)
    g_re = jnp.ones((P2, N), F32)
    g_im = jnp.zeros((P2, N), F32)
    b_re, b_im = l1_re, l1_im
    bit = 1
    while bit < Q:
        use = (nt & bit) != 0
        f_re = jnp.where(use, b_re, 1.0)
        f_im = jnp.where(use, b_im, 0.0)
        g_re, g_im = g_re * f_re - g_im * f_im, g_re * f_im + g_im * f_re
        b_re, b_im = b_re * b_re - b_im * b_im, 2.0 * b_re * b_im
        bit *= 2
    c_re, c_im = ctre_ref[...], ctim_ref[...]
    gx_re = c_re * g_re - c_im * g_im
    gx_im = c_re * g_im + c_im * g_re
    wout_ref[0:P2, :] = (gx_re * l1_re - gx_im * l1_im).astype(BF16)
    wout_ref[P2:2 * P2, :] = (-(gx_re * l1_im + gx_im * l1_re)).astype(BF16)

    gx = jnp.concatenate([gx_re, gx_im], axis=0)
    is_f = lax.broadcasted_iota(jnp.int32, (H, P2), 1) < P
    zero = jnp.zeros((H, P2), F32)
    lhs_f = jnp.concatenate([jnp.where(is_f, bb_re, zero), jnp.where(is_f, -bb_im, zero)], axis=1)
    lhs_b = jnp.concatenate([jnp.where(is_f, zero, bb_re), jnp.where(is_f, zero, -bb_im)], axis=1)
    m_f = jnp.dot(lhs_f, gx, precision=HIGHEST, preferred_element_type=F32)
    m_b = jnp.dot(lhs_b, gx, precision=HIGHEST, preferred_element_type=F32)
    lane_n = lax.broadcasted_iota(jnp.int32, (H, N), 1)
    for s in range(Q):
        tf = m_f if s == 0 else pltpu.roll(m_f, s * H, 1)
        tb = m_b if s == Q - 1 else pltpu.roll(m_b, (s + 1) * H, 1)
        slab = jnp.where(lane_n >= s * H, tf, 0.0) + jnp.where(lane_n < (s + 1) * H, tb, 0.0)
        t_ref[s * H:(s + 1) * H, :] = slab.astype(BF16)


def _s5_prep(lam_re, lam_im, log_dt, b_re, b_im, c_re, c_im):
    _, G, P = lam_re.shape
    H = S5_GROUP
    N = S5_CHUNK * H
    P2 = 2 * P
    fb_lanes = lambda a: jnp.concatenate([a[0], a[1]], axis=-1)
    ldt = jnp.broadcast_to(log_dt[..., None], (2, G, P))
    rows = [fb_lanes(a).reshape(G, 1, P2) for a in (lam_re, lam_im, ldt)]
    cols = [fb_lanes(a).reshape(G, P2, 1) for a in (lam_re, lam_im, ldt)]
    bt = [fb_lanes(jnp.swapaxes(a, -1, -2)) for a in (b_re, b_im)]
    ct = [jnp.tile(jnp.concatenate([jnp.swapaxes(a[0], -1, -2), jnp.swapaxes(a[1], -1, -2)], axis=1),
                   (1, 1, S5_CHUNK)) for a in (c_re, c_im)]
    ins = rows + cols + bt + ct
    gspec = lambda a: pl.BlockSpec((None,) + a.shape[1:], lambda g: (g, 0, 0))
    mspec = pl.BlockSpec((None, N, N), lambda g: (g, 0, 0))
    msh = jax.ShapeDtypeStruct((G, N, N), BF16)
    return pl.pallas_call(
        _s5_prep_kernel,
        grid=(G,),
        in_specs=[gspec(a) for a in ins],
        out_specs=[mspec, mspec, mspec, pl.BlockSpec((None, 2, P2), lambda g: (g, 0, 0))],
        out_shape=[msh, msh, msh, jax.ShapeDtypeStruct((G, 2, P2), F32)],
        compiler_params=_cparams(("parallel",)),
        name="s5_prep",
    )(*ins)


def _s5_core_kernel(h_ref, wst_ref, t_ref, wout_ref, lam_ref, s0_ref, y_ref, fin_ref,
                    u_scr, loc, sa, sb, *, nb, nc, gl):
    Q, H, GB = S5_CHUNK, S5_GROUP, S5_GROUPS_PER_STEP
    R = nb * nc
    P2 = lam_ref.shape[-1]
    lane_blk = lax.broadcasted_iota(jnp.int32, (R, LANES), 1) // H

    def gather_blocks(pieces, src_blk):
        acc = None
        for b, piece in enumerate(pieces):
            shift = ((b - src_blk) % GB) * H
            r = piece if shift == 0 else pltpu.roll(piece, shift, 1)
            acc = r if acc is None else jnp.where(lane_blk == b, r, acc)
        return acc

    hi_mask = jnp.int32(-65536)

    def pack2(a, b):
        abits = lax.bitcast_convert_type(a.astype(BF16).astype(F32), jnp.int32)
        bbits = lax.bitcast_convert_type(b.astype(BF16).astype(F32), jnp.int32)
        return (abits & hi_mask) | lax.shift_right_logical(bbits, 16)

    def unpack2(p):
        return [lax.bitcast_convert_type(p & hi_mask, F32),
                lax.bitcast_convert_type(lax.shift_left(p, 16), F32)]

    packed = [pack2(h_ref[:, s, :, :].reshape(R, LANES), h_ref[:, s + GB, :, :].reshape(R, LANES))
              for s in range(GB)]
    for g in range(GB):
        u = jnp.concatenate(unpack2(gather_blocks(packed, g)), axis=-1).astype(BF16)
        u_scr[g] = u
        loc[g] = jnp.dot(u, wst_ref[g], preferred_element_type=F32)

    fwd_half = lax.broadcasted_iota(jnp.int32, (nb, P2), 1) < P2 // 2
    for g0 in range(0, GB, gl):
        def body(k, carry, g0=g0):
            rf = pl.multiple_of(k * nb, nb)
            rb = pl.multiple_of((nc - 1 - k) * nb, nb)
            out = []
            for gi in range(gl):
                g = g0 + gi
                xr, xi = carry[2 * gi], carry[2 * gi + 1]
                sa[g, pl.ds(rf, nb), 0:P2] = xr
                sa[g, pl.ds(rf, nb), P2:2 * P2] = xi
                sb[g, pl.ds(rb, nb), 0:P2] = xr
                sb[g, pl.ds(rb, nb), P2:2 * P2] = xi
                lr = jnp.where(fwd_half, loc[g, pl.ds(rf, nb), 0:P2], loc[g, pl.ds(rb, nb), 0:P2])
                li = jnp.where(fwd_half, loc[g, pl.ds(rf, nb), P2:2 * P2],
                               loc[g, pl.ds(rb, nb), P2:2 * P2])
                ar, ai = lam_ref[g, 0:1, :], lam_ref[g, 1:2, :]
                out += [ar * xr - ai * xi + lr, ar * xi + ai * xr + li]
            return tuple(out)

        init = []
        for gi in range(gl):
            init += [s0_ref[g0 + gi, :, 0:P2], s0_ref[g0 + gi, :, P2:2 * P2]]
        fin = lax.fori_loop(0, nc, body, tuple(init))
        for gi in range(gl):
            fin_ref[g0 + gi, :, 0:P2] = fin[2 * gi]
            fin_ref[g0 + gi, :, P2:2 * P2] = fin[2 * gi + 1]

    fsel = (lax.broadcasted_iota(jnp.int32, (R, 2 * P2), 1) % P2) < P2 // 2
    for g in range(GB):
        s_in = jnp.where(fsel, sa[g], sb[g]).astype(BF16)
        loc[g] = (jnp.dot(u_scr[g], t_ref[g], preferred_element_type=F32)
                  + jnp.dot(s_in, wout_ref[g], preferred_element_type=F32))
    ys = [pack2(loc[g, :, 0:LANES], loc[g, :, LANES:2 * LANES]) for g in range(GB)]
    for tl in range(GB):
        lo, hi = unpack2(gather_blocks(ys, tl))
        y_ref[:, tl, :, :] = lo.reshape(nc, nb, LANES)
        y_ref[:, tl + GB, :, :] = hi.reshape(nc, nb, LANES)


def _s5_core(h_tm, s0, wst, tmat, wout, lam, *, nb, seq_len):
    d = h_tm.shape[1] // nb
    H, Q, GB = S5_GROUP, S5_CHUNK, S5_GROUPS_PER_STEP
    G = d // H
    nc = seq_len // Q
    R = nb * nc
    N = Q * H
    P4 = s0.shape[-1]
    gl = max(1, min(GB, (8 * GB) // nb))
    h4 = h_tm.reshape(nc, Q, nb, d)
    hspec = pl.BlockSpec((nc, Q, nb, LANES), lambda gb: (0, 0, 0, gb))
    mspec = pl.BlockSpec((GB, N, N), lambda gb: (gb, 0, 0))
    sspec = pl.BlockSpec((GB, nb, P4), lambda gb: (gb, 0, 0))
    y, fin = pl.pallas_call(
        functools.partial(_s5_core_kernel, nb=nb, nc=nc, gl=gl),
        grid=(G // GB,),
        in_specs=[hspec, mspec, mspec, mspec,
                  pl.BlockSpec((GB, 2, P4 // 2), lambda gb: (gb, 0, 0)), sspec],
        out_specs=[hspec, sspec],
        out_shape=[jax.ShapeDtypeStruct((nc, Q, nb, d), F32), jax.ShapeDtypeStruct((G, nb, P4), F32)],
        scratch_shapes=[pltpu.VMEM((GB, R, N), BF16), pltpu.VMEM((GB, R, N), F32),
                        pltpu.VMEM((GB, R, P4), F32), pltpu.VMEM((GB, R, P4), F32)],
        compiler_params=_cparams(("parallel",)),
        name="s5_core",
    )(h4, wst, tmat, wout, lam, s0)
    return y.reshape(seq_len, nb * d), fin


def _glu_postadd_kernel(*refs, nsub):
    h_refs, y_refs = refs[:nsub], refs[nsub:2 * nsub]
    (d_ref, wa_ref, wg_ref, ba_ref, bg_ref, x_ref, gate_ref, g_ref, o_ref, acc_a, acc_g, u_scr) = refs[2 * nsub:]
    k = pl.program_id(1)

    @pl.when(k == 0)
    def _():
        acc_a[...] = jnp.zeros_like(acc_a)
        acc_g[...] = jnp.zeros_like(acc_g)

    rows = h_refs[0].shape[0]
    for s in range(nsub):
        u_scr[s * rows:(s + 1) * rows, :] = jax.nn.gelu(
            d_ref[...] * h_refs[s][...] + y_refs[s][...]).astype(BF16)
    u = u_scr[...]
    acc_a[...] += jnp.dot(u, wa_ref[...], preferred_element_type=F32)
    acc_g[...] += jnp.dot(u, wg_ref[...], preferred_element_type=F32)

    @pl.when(k == pl.num_programs(1) - 1)
    def _():
        ba, bg = ba_ref[...], bg_ref[...]
        _post_add_rows(x_ref, lambda rows: (acc_a[rows, :] + ba) * jax.nn.sigmoid(acc_g[rows, :] + bg),
                       g_ref, gate_ref, o_ref)


def _glu_postadd(h_tm, y_tm, dskip, w_glu, layer, b_glu, x, m_l, g1, *, seq_len, latent):
    t, d = x.shape
    tm = min(512, t)
    nsub = max(1, tm // seq_len)
    per_seq = max(1, seq_len // tm)
    sub_rows = tm // nsub
    tk = 1024
    nk = d // tk
    rowfn = _row_fn(tm, seq_len, latent)
    b_glu = b_glu.reshape(1, 2 * d)
    tspecs = [pl.BlockSpec((sub_rows, tk), lambda i, k, s=s: (i % per_seq, ((i // per_seq) * nsub + s) * nk + k))
              for s in range(nsub)]
    return pl.pallas_call(
        functools.partial(_glu_postadd_kernel, nsub=nsub),
        grid=(t // tm, nk),
        in_specs=tspecs + tspecs + [
            pl.BlockSpec((1, tk), lambda i, k: (0, k)),
            pl.BlockSpec((None, None, tk, d), lambda i, k: (layer, 0, k, 0)),
            pl.BlockSpec((None, None, tk, d), lambda i, k: (layer, 1, k, 0)),
            pl.BlockSpec((1, d), lambda i, k: (0, 0)),
            pl.BlockSpec((1, d), lambda i, k: (0, 1)),
            pl.BlockSpec((tm, d), lambda i, k: (i, 0)),
            _mod_spec(d, 2, rowfn),
            pl.BlockSpec((1, d), lambda i, k: (0, 0)),
        ],
        out_specs=pl.BlockSpec((tm, d), lambda i, k: (i, 0)),
        out_shape=jax.ShapeDtypeStruct((t, d), F32),
        scratch_shapes=[pltpu.VMEM((tm, d), F32), pltpu.VMEM((tm, d), F32), pltpu.VMEM((tm, tk), BF16)],
        compiler_params=_cparams(("parallel", "arbitrary")),
        name="glu_postadd",
    )(*([h_tm] * nsub), *([y_tm] * nsub), dskip.reshape(1, d), w_glu, w_glu, b_glu, b_glu, x, m_l,
      g1.reshape(1, d))


def kernel(x_prompt, x_sample, cache_attn_k, cache_attn_v, state_s5_re, state_s5_im, c, c_ctx, w_mod, b_mod, g_norm, w_mlp_in, w_mlp_out, hy_w_in, hy_b_in, hy_w_short, hy_b_short, hy_f_w1, hy_f_b1, hy_f_freq1, hy_f_w2, hy_f_b2, hy_f_freq2, hy_f_w3, hy_log_alpha, hy_skip, hy_w_out, hy_b_out, at_w_qkv, at_lam, at_g_sub, at_w_o, s5_lam_re, s5_lam_im, s5_log_dt, s5_b_re, s5_b_im, s5_c_re, s5_c_im, s5_d, s5_w_glu, s5_b_glu):
    bc, lc, d = x_prompt.shape
    bl, ll, _ = x_sample.shape
    depth = w_mod.shape[0]
    assert 1 + bl <= MOD_ROWS
    assert cache_attn_k.shape[1] == 1 and state_s5_re.shape[1] == 1, "one attention and one S5 layer"
    hd = d // N_HEADS // 2
    G = d // S5_GROUP
    P = s5_lam_re.shape[-1]

    cond = jnp.concatenate([c_ctx[None], c, jnp.zeros((MOD_ROWS - 1 - bl, d), F32)], axis=0)
    mod = _modulation(cond, w_mod, b_mod).reshape(depth, MOD_ROWS, N_MOD, 1, d)

    streams = [dict(seq_len=lc, latent=False), dict(seq_len=ll, latent=True)]
    xs = [x_prompt.reshape(bc * lc, d), x_sample.reshape(bl * ll, d)]

    tables = {}
    for L in {lc, ll}:
        cm, sm, sp, ci, sip = _dft_tables(L)
        tables[L] = (_split_bf16(cm) + _split_bf16(sm), tuple(a.astype(BF16) for a in (cm, sp, ci, sip)))

    w_mlp_in_b = _cast_tiles(w_mlp_in, MLP_TF)
    w_mlp_out_b = _cast_tiles(w_mlp_out.reshape(-1, MLP_TF, d), d).reshape(w_mlp_out.shape)
    hy_w_in_b = _cast_tiles(hy_w_in, PROJ_TN)
    at_w_qkv_b = _cast_tiles(at_w_qkv, PROJ_TN)
    hy_w_out_b = _cast_tiles(hy_w_out, d).reshape(hy_w_out.shape)
    at_w_o_b = _cast_tiles(at_w_o, d).reshape(at_w_o.shape)
    s5_w_glu_b = _cast_tiles(s5_w_glu, d)

    new_k = new_v = None
    fin_ctx = None
    for i in range(depth):
        kind, j = i % 3, i // 3
        m_l = mod[i]
        g = g_norm[i]
        if kind == 0:
            filt = {}
            for L in {lc, ll}:
                filt[L] = _hy_filter(L, hy_f_w1[j], hy_f_b1[j], hy_f_freq1[j], hy_f_w2[j], hy_f_b2[j],
                                     hy_f_freq2[j], hy_f_w3[j], hy_log_alpha[j], tables[L][0])
            for si, st in enumerate(streams):
                L = st["seq_len"]
                x0, vg = _hy_inproj(xs[si], m_l, g[0], hy_w_in_b, j, hy_b_in[j], hy_w_short[j],
                                    hy_b_short[j], **st)
                a = _hy_conv(x0, vg, filt[L][0], filt[L][1], hy_skip[j], tables[L][1], seq_len=L)
                xs[si] = _mm_postadd(a, hy_w_out_b, j, hy_b_out[j], xs[si], m_l, g[1], **st)
        elif kind == 1:
            lam_init = 0.8 - 0.6 * math.exp(-0.3 * i)
            for si, st in enumerate(streams):
                L = st["seq_len"]
                q, k, v = _premod_mm3(xs[si], m_l, g[0], at_w_qkv_b, j, **st)
                if not st["latent"]:
                    new_k = k.reshape(bc, 1, lc, N_HEADS, 2, hd)
                    new_v = v.reshape(bc, 1, lc, N_HEADS, 2 * hd)
                    a = _attn_ctx(q, k, v, at_lam[j], at_g_sub[j], seq_len=L, lam_init=lam_init)
                else:
                    ck = cache_attn_k[:, j].reshape(bl, -1, d)
                    cv = cache_attn_v[:, j].reshape(bl, -1, d)
                    a = _attn_lat(q, k, v, ck, cv, at_lam[j], at_g_sub[j], seq_len=L, lam_init=lam_init)
                xs[si] = _mm_postadd(a, at_w_o_b, j, jnp.zeros((d,), F32), xs[si], m_l, g[1], **st)
        else:
            wst, tmat, wout, lam_q = _s5_prep(s5_lam_re[j], s5_lam_im[j], s5_log_dt[j], s5_b_re[j],
                                              s5_b_im[j], s5_c_re[j], s5_c_im[j])
            for si, st in enumerate(streams):
                L = st["seq_len"]
                nb = xs[si].shape[0] // L
                if st["latent"]:
                    sre, sim = state_s5_re[:, j], state_s5_im[:, j]
                    s0 = jnp.concatenate([sre[:, 0], sre[:, 1], sim[:, 0], sim[:, 1]], axis=-1)
                    s0 = s0.transpose(1, 0, 2)
                else:
                    s0 = jnp.zeros((G, nb, 4 * P), F32)
                h_tm = _premod_time_major(xs[si], m_l, g[0], **st)
                y_tm, fin = _s5_core(h_tm, s0, wst, tmat, wout, lam_q, nb=nb, seq_len=L)
                if not st["latent"]:
                    fin_ctx = fin.reshape(G, nb, 2, 2, P).transpose(1, 2, 3, 0, 4)
                xs[si] = _glu_postadd(h_tm, y_tm, s5_d[j], s5_w_glu_b, j, s5_b_glu[j], xs[si], m_l, g[1], **st)
        for si, st in enumerate(streams):
            xs[si] = _mlp(xs[si], m_l, g[2], g[3], w_mlp_in_b, w_mlp_out_b, i, **st)

    new_s_re = fin_ctx[:, 0][:, None]
    new_s_im = fin_ctx[:, 1][:, None]
    return (xs[0].reshape(bc, lc, d), xs[1].reshape(bl, ll, d), new_k, new_v, new_s_re, new_s_im)
```

```python
import functools
import math

import numpy as np
import jax
import jax.numpy as jnp
from jax import lax
from jax.experimental import pallas as pl
from jax.experimental.pallas import tpu as pltpu

F32 = jnp.float32
BF16 = jnp.bfloat16
HIGHEST = lax.Precision.HIGHEST

NORM_EPS = 1e-6
N_MOD = 6
N_HEADS = 8
GRID_W = 64
ROPE_BASE = 10000.0
HY_PE_BANDS = 16
HY_PE_MIN_PERIOD = 2.0
HY_PE_MAX_PERIOD = 4096.0
S5_GROUP = 16
S5_CHUNK = 16
LANES = 128
S5_GROUPS_PER_STEP = LANES // S5_GROUP
MOD_ROWS = 16
MLP_TF = 512
PROJ_TN = 512

VMEM_LIMIT = 56 * 1024 * 1024


def _cparams(sem):
    return pltpu.CompilerParams(dimension_semantics=sem, vmem_limit_bytes=VMEM_LIMIT)


def _rms(x):
    return lax.rsqrt(jnp.mean(x * x, axis=-1, keepdims=True) + NORM_EPS)


def _split_bf16(a):
    hi = a.astype(BF16)
    return hi, (a - hi.astype(F32)).astype(BF16)


def _cast_kernel(w_ref, o_ref):
    o_ref[...] = w_ref[...].astype(BF16)


def _cast_tiles(w, tc):
    n, k, m = w.shape
    rk = min(k, (2 * 1024 * 1024) // tc)
    return pl.pallas_call(
        _cast_kernel,
        grid=(n, m // tc, k // rk),
        in_specs=[pl.BlockSpec((None, rk, tc), lambda l, j, r: (l, r, j))],
        out_specs=pl.BlockSpec((None, None, rk, tc), lambda l, j, r: (l, j, r, 0)),
        out_shape=jax.ShapeDtypeStruct((n, m // tc, k, tc), BF16),
        compiler_params=_cparams(("parallel", "parallel", "parallel")),
        name="cast_bf16",
    )(w)


def _mod_kernel(c_ref, w_ref, b_ref, o_ref):
    c = c_ref[...]
    s_hi, s_lo = _split_bf16(c * jax.nn.sigmoid(c))
    both = jnp.dot(jnp.concatenate([s_hi, s_lo], axis=0), w_ref[...].astype(BF16),
                   preferred_element_type=F32)
    o_ref[...] = both[:MOD_ROWS] + both[MOD_ROWS:] + b_ref[...]


def _modulation(cond, w_mod, b_mod):
    depth, d, n = w_mod.shape
    tn = 1024
    return pl.pallas_call(
        _mod_kernel,
        grid=(depth, n // tn),
        in_specs=[
            pl.BlockSpec((MOD_ROWS, d), lambda l, j: (0, 0)),
            pl.BlockSpec((None, d, tn), lambda l, j: (l, 0, j)),
            pl.BlockSpec((None, 1, tn), lambda l, j: (l, 0, j)),
        ],
        out_specs=pl.BlockSpec((None, MOD_ROWS, tn), lambda l, j: (l, 0, j)),
        out_shape=jax.ShapeDtypeStruct((depth, MOD_ROWS, n), F32),
        compiler_params=_cparams(("parallel", "parallel")),
        name="adaln_mod",
    )(cond, w_mod, b_mod.reshape(depth, 1, n))


def _mod_spec(d, which, rowfn):
    return pl.BlockSpec((None, None, 1, d), lambda i, *_: (rowfn(i), which, 0, 0))


def _row_fn(tm, seq_len, latent):
    if latent:
        assert seq_len % tm == 0, "a latent row tile must sit inside one sequence"
        return lambda i: 1 + (i * tm) // seq_len
    return lambda i: 0


ROW_CHUNK = 16


def _for_row_chunks(n_rows, fn):
    def body(c, carry):
        fn(pl.ds(pl.multiple_of(c * ROW_CHUNK, ROW_CHUNK), ROW_CHUNK))
        return carry

    lax.fori_loop(0, n_rows // ROW_CHUNK, body, 0, unroll=16)


def _premod_rows(x_ref, g_ref, sh_ref, sc_ref, out_ref):
    gs, sh = g_ref[...] * (1.0 + sc_ref[...]), sh_ref[...]

    def rows_fn(rows):
        x = x_ref[rows, :]
        out_ref[rows, :] = ((x * _rms(x)) * gs + sh).astype(out_ref.dtype)

    _for_row_chunks(x_ref.shape[0], rows_fn)


def _post_add_rows(x_ref, o_fn, g_ref, gate_ref, out_ref):
    gg = gate_ref[...] * g_ref[...]

    def rows_fn(rows):
        o = o_fn(rows)
        out_ref[rows, :] = x_ref[rows, :] + (o * _rms(o)) * gg

    _for_row_chunks(x_ref.shape[0], rows_fn)


def _premod_mm3_kernel(x_ref, g_ref, sh_ref, sc_ref, wa_ref, wb_ref, wc_ref, oa_ref, ob_ref, oc_ref, h_scr):
    @pl.when(pl.program_id(1) == 0)
    def _():
        _premod_rows(x_ref, g_ref, sh_ref, sc_ref, h_scr)

    h = h_scr[...]
    for w_ref, o_ref in ((wa_ref, oa_ref), (wb_ref, ob_ref), (wc_ref, oc_ref)):
        o_ref[...] = jnp.dot(h, w_ref[...], preferred_element_type=F32)


def _premod_mm3(x, m_l, g, w, layer, *, seq_len, latent):
    t, d = x.shape
    tm = min(1024, seq_len if latent else t)
    tn = PROJ_TN
    nj = d // tn
    rowfn = _row_fn(tm, seq_len, latent)
    wspec = lambda o: pl.BlockSpec((None, None, d, tn), lambda i, j: (layer, o * nj + j, 0, 0))
    ospec = pl.BlockSpec((tm, tn), lambda i, j: (i, j))
    osh = jax.ShapeDtypeStruct((t, d), F32)
    return pl.pallas_call(
        _premod_mm3_kernel,
        grid=(t // tm, nj),
        in_specs=[
            pl.BlockSpec((tm, d), lambda i, j: (i, 0)),
            pl.BlockSpec((1, d), lambda i, j: (0, 0)),
            _mod_spec(d, 0, rowfn),
            _mod_spec(d, 1, rowfn),
            wspec(0), wspec(1), wspec(2),
        ],
        out_specs=[ospec, ospec, ospec],
        out_shape=[osh, osh, osh],
        scratch_shapes=[pltpu.VMEM((tm, d), BF16)],
        compiler_params=_cparams(("parallel", "arbitrary")),
        name="premod_mm3",
    )(x, g.reshape(1, d), m_l, m_l, w, w, w)


def _premod_kernel(x_ref, g_ref, sh_ref, sc_ref, o_ref):
    _premod_rows(x_ref, g_ref, sh_ref, sc_ref, o_ref)


def _premod_time_major(x, m_l, g, *, seq_len, latent):
    t, d = x.shape
    nb = t // seq_len
    tm = min(512, seq_len)
    per_seq = seq_len // tm
    rowfn = _row_fn(tm, seq_len, latent)
    return pl.pallas_call(
        _premod_kernel,
        grid=(t // tm,),
        in_specs=[
            pl.BlockSpec((tm, d), lambda i: (i, 0)),
            pl.BlockSpec((1, d), lambda i: (0, 0)),
            _mod_spec(d, 0, rowfn),
            _mod_spec(d, 1, rowfn),
        ],
        out_specs=pl.BlockSpec((tm, d), lambda i: (i % per_seq, i // per_seq)),
        out_shape=jax.ShapeDtypeStruct((seq_len, nb * d), F32),
        compiler_params=_cparams(("parallel",)),
        name="premod",
    )(x, g.reshape(1, d), m_l, m_l)


def _mlp_kernel(x_ref, g2_ref, sh_ref, sc_ref, gate_ref, g3_ref, w1_ref, w2_ref, o_ref, h_scr, acc):
    f = pl.program_id(1)

    @pl.when(f == 0)
    def _():
        _premod_rows(x_ref, g2_ref, sh_ref, sc_ref, h_scr)
        acc[...] = jnp.zeros_like(acc)

    a = jnp.dot(h_scr[...], w1_ref[...], preferred_element_type=F32)
    a = jnp.square(jnp.maximum(a, 0.0)).astype(BF16)
    acc[...] += jnp.dot(a, w2_ref[...], preferred_element_type=F32)

    @pl.when(f == pl.num_programs(1) - 1)
    def _():
        _post_add_rows(x_ref, lambda rows: acc[rows, :], g3_ref, gate_ref, o_ref)


def _mlp(x, m_l, g2, g3, w1, w2, layer, *, seq_len, latent):
    t, d = x.shape
    tf = MLP_TF
    dff = w1.shape[1] * tf
    tm = min(1024, seq_len if latent else t)
    rowfn = _row_fn(tm, seq_len, latent)
    return pl.pallas_call(
        _mlp_kernel,
        grid=(t // tm, dff // tf),
        in_specs=[
            pl.BlockSpec((tm, d), lambda i, f: (i, 0)),
            pl.BlockSpec((1, d), lambda i, f: (0, 0)),
            _mod_spec(d, 3, rowfn),
            _mod_spec(d, 4, rowfn),
            _mod_spec(d, 5, rowfn),
            pl.BlockSpec((1, d), lambda i, f: (0, 0)),
            pl.BlockSpec((None, None, d, tf), lambda i, f: (layer, f, 0, 0)),
            pl.BlockSpec((None, tf, d), lambda i, f: (layer, f, 0)),
        ],
        out_specs=pl.BlockSpec((tm, d), lambda i, f: (i, 0), pipeline_mode=pl.Buffered(1)),
        out_shape=jax.ShapeDtypeStruct((t, d), F32),
        scratch_shapes=[pltpu.VMEM((tm, d), BF16), pltpu.VMEM((tm, d), F32)],
        compiler_params=_cparams(("parallel", "arbitrary")),
        name="mlp",
    )(x, g2.reshape(1, d), m_l, m_l, m_l, g3.reshape(1, d), w1, w2)


def _mm_postadd_kernel(a_ref, w_ref, b_ref, x_ref, gate_ref, g_ref, o_ref, acc):
    acc[...] = jnp.dot(a_ref[...], w_ref[...], preferred_element_type=F32)
    b = b_ref[...]
    _post_add_rows(x_ref, lambda rows: acc[rows, :] + b, g_ref, gate_ref, o_ref)


def _mm_postadd(a, w, layer, b, x, m_l, g1, *, seq_len, latent):
    t, d = x.shape
    tm = min(512, t)
    rowfn = _row_fn(tm, seq_len, latent)
    return pl.pallas_call(
        _mm_postadd_kernel,
        grid=(t // tm,),
        in_specs=[
            pl.BlockSpec((tm, d), lambda i: (i, 0)),
            pl.BlockSpec((None, d, d), lambda i: (layer, 0, 0)),
            pl.BlockSpec((1, d), lambda i: (0, 0)),
            pl.BlockSpec((tm, d), lambda i: (i, 0)),
            _mod_spec(d, 2, rowfn),
            pl.BlockSpec((1, d), lambda i: (0, 0)),
        ],
        out_specs=pl.BlockSpec((tm, d), lambda i: (i, 0)),
        out_shape=jax.ShapeDtypeStruct((t, d), F32),
        scratch_shapes=[pltpu.VMEM((tm, d), F32)],
        compiler_params=_cparams(("parallel",)),
        name="mm_postadd",
    )(a, w, b.reshape(1, d), x, m_l, g1.reshape(1, d))


def _dft_tables(L):
    idx = np.arange(L, dtype=np.int64)
    ang = np.pi * ((idx[:, None] * idx[None, :]) % (2 * L)).astype(np.float64) / L
    c = np.cos(ang)
    s = np.sin(ang)
    alt = np.where(idx % 2 == 0, 1.0, -1.0)
    sp = s.copy()
    sp[0, :] = alt
    wf = np.full((L,), 2.0)
    wf[0] = 1.0
    ci = c * wf[None, :] / (2 * L)
    sip = -s * 2.0 / (2 * L)
    sip[:, 0] = alt / (2 * L)
    f32 = lambda a: jnp.asarray(a.astype(np.float32))
    return f32(c), f32(s), f32(sp), f32(ci), f32(sip)


def _hy_pe(L):
    t = np.arange(L, dtype=np.float64)
    periods = HY_PE_MIN_PERIOD * (HY_PE_MAX_PERIOD / HY_PE_MIN_PERIOD) ** (
        np.arange(HY_PE_BANDS, dtype=np.float64) / (HY_PE_BANDS - 1))
    ang = t[:, None] * (2.0 * math.pi / periods)[None]
    return jnp.asarray(np.concatenate([np.sin(ang), np.cos(ang)], axis=-1).astype(np.float32))


def _dot_bf16x3(a_hi, a_lo, b):
    b_hi, b_lo = _split_bf16(b)
    n = b.shape[1]
    both = jnp.dot(a_hi, jnp.concatenate([b_hi, b_lo], axis=1), preferred_element_type=F32)
    return both[:, :n] + both[:, n:] + jnp.dot(a_lo, b_hi, preferred_element_type=F32)


def _hy_filter_kernel(pe_ref, w1_ref, b1_ref, fr1_ref, w2_ref, b2_ref, fr2_ref, w3f_ref, w3b_ref,
                      la_ref, chi_ref, clo_ref, shi_ref, slo_ref, kre_ref, kim_ref, h_scr):
    L, dc = kre_ref.shape

    @pl.when(pl.program_id(0) == 0)
    def _():
        h1 = jnp.sin(fr1_ref[...] * (jnp.dot(pe_ref[...], w1_ref[...], precision=HIGHEST,
                                             preferred_element_type=F32) + b1_ref[...]))
        h_scr[...] = jnp.sin(fr2_ref[...] * (jnp.dot(h1, w2_ref[...], precision=HIGHEST,
                                                     preferred_element_type=F32) + b2_ref[...]))

    h = h_scr[...]
    row = lax.broadcasted_iota(jnp.int32, (L, dc), 0)
    dec = jnp.exp(-jnp.exp(la_ref[...]) * row.astype(F32))
    kf = jnp.dot(h, w3f_ref[...], precision=HIGHEST, preferred_element_type=F32) * dec
    kb = jnp.dot(h, w3b_ref[...], precision=HIGHEST, preferred_element_type=F32) * dec
    kb = jnp.where(row == 0, 0.0, kb)
    norm = jnp.sum(jnp.abs(kf) + jnp.abs(kb), axis=0, keepdims=True) + 1e-6
    inv = 1.0 / norm
    ks = (kf + kb) * inv
    kd = (kb - kf) * inv
    kre = _dot_bf16x3(chi_ref[...], clo_ref[...], ks)
    kim = _dot_bf16x3(shi_ref[...], slo_ref[...], kd)
    alt = jnp.where(row % 2 == 0, 1.0, -1.0)
    nyq = jnp.sum(alt * ks, axis=0, keepdims=True)
    kre_ref[...] = kre
    kim_ref[...] = jnp.where(row == 0, nyq, kim)


def _hy_filter(L, w1, b1, fr1, w2, b2, fr2, w3, log_alpha, cs_split):
    d = log_alpha.shape[-1]
    fw = w1.shape[1]
    dc = 512
    nd = d // dc
    full = lambda a: pl.BlockSpec(a.shape, lambda j: (0,) * a.ndim)
    pe = _hy_pe(L)
    b1, fr1, b2, fr2 = (a.reshape(1, fw) for a in (b1, fr1, b2, fr2))
    osh = jax.ShapeDtypeStruct((L, d), F32)
    return pl.pallas_call(
        _hy_filter_kernel,
        grid=(nd,),
        in_specs=[full(pe), full(w1), full(b1), full(fr1), full(w2), full(b2), full(fr2),
                  pl.BlockSpec((fw, dc), lambda j: (0, j)),
                  pl.BlockSpec((fw, dc), lambda j: (0, nd + j)),
                  pl.BlockSpec((1, dc), lambda j: (0, j))] + [full(a) for a in cs_split],
        out_specs=[pl.BlockSpec((L, dc), lambda j: (0, j))] * 2,
        out_shape=[osh, osh],
        scratch_shapes=[pltpu.VMEM((L, fw), F32)],
        compiler_params=_cparams(("arbitrary",)),
        name="hyena_filter",
    )(pe, w1, b1, fr1, w2, b2, fr2, w3, w3, log_alpha.reshape(1, d), *cs_split)


def _hy_inproj_kernel(x_ref, g_ref, sh_ref, sc_ref, w0_ref, w1_ref, wv_ref, b0_ref, b1_ref, bv_ref,
                      s0_ref, s1_ref, sv_ref, c0_ref, c1_ref, cv_ref, ox_ref, ov_ref, h_scr, *, seq_len):
    @pl.when(pl.program_id(1) == 0)
    def _():
        _premod_rows(x_ref, g_ref, sh_ref, sc_ref, h_scr)

    h = h_scr[...]
    tm, tn = ox_ref.shape
    pos = lax.broadcasted_iota(jnp.int32, (tm, tn), 0) % seq_len
    first, last = pos == 0, pos == seq_len - 1

    def section(w_ref, b_ref, s_ref, c_ref):
        z = jnp.dot(h, w_ref[...], preferred_element_type=F32) + b_ref[...]
        zm = jnp.where(first, 0.0, pltpu.roll(z, 1, 0))
        zp = jnp.where(last, 0.0, pltpu.roll(z, tm - 1, 0))
        return zm * s_ref[0:1, :] + z * s_ref[1:2, :] + zp * s_ref[2:3, :] + c_ref[...]

    ox_ref[...] = section(w0_ref, b0_ref, s0_ref, c0_ref).astype(ox_ref.dtype)
    x1 = section(w1_ref, b1_ref, s1_ref, c1_ref)
    ov_ref[...] = (section(wv_ref, bv_ref, sv_ref, cv_ref) * x1).astype(ov_ref.dtype)


def _hy_inproj(x, m_l, g, w, layer, b, w_sh, b_sh, *, seq_len, latent):
    t, d = x.shape
    tm = min(1024, seq_len if latent else t)
    assert tm % seq_len == 0, "row tiles hold whole sequences, so the short conv needs no halo"
    tn = PROJ_TN
    nj = d // tn
    rowfn = _row_fn(tm, seq_len, latent)
    wspec = lambda o: pl.BlockSpec((None, None, d, tn), lambda i, j: (layer, o * nj + j, 0, 0))
    vspec = lambda rows: (lambda o: pl.BlockSpec((rows, tn), lambda i, j: (0, o * nj + j)))
    bspec, sspec = vspec(1), vspec(3)
    ospec = pl.BlockSpec((tm, tn), lambda i, j: (i, j))
    osh = jax.ShapeDtypeStruct((t, d), BF16)
    b = b.reshape(1, 3 * d)
    b_sh = b_sh.reshape(1, 3 * d)
    return pl.pallas_call(
        functools.partial(_hy_inproj_kernel, seq_len=seq_len),
        grid=(t // tm, nj),
        in_specs=[
            pl.BlockSpec((tm, d), lambda i, j: (i, 0)),
            pl.BlockSpec((1, d), lambda i, j: (0, 0)),
            _mod_spec(d, 0, rowfn),
            _mod_spec(d, 1, rowfn),
            wspec(0), wspec(1), wspec(2), bspec(0), bspec(1), bspec(2),
            sspec(0), sspec(1), sspec(2), bspec(0), bspec(1), bspec(2),
        ],
        out_specs=[ospec, ospec],
        out_shape=[osh, osh],
        scratch_shapes=[pltpu.VMEM((tm, d), BF16)],
        compiler_params=_cparams(("parallel", "arbitrary")),
        name="hyena_inproj",
    )(x, g.reshape(1, d), m_l, m_l, w, w, w, b, b, b, w_sh, w_sh, w_sh, b_sh, b_sh, b_sh)


def _hy_conv_kernel(x0_ref, v_ref, kre_ref, kim_ref, skip_ref, c_ref, sp_ref, ci_ref, sip_ref, o_ref):
    sb, L, dc = o_ref.shape
    kre = kre_ref[...]
    kim = kim_ref[...]
    first = lax.broadcasted_iota(jnp.int32, (L, dc), 0) == 0
    for s in range(sb):
        vb = v_ref[s]
        a = jnp.dot(c_ref[...], vb, preferred_element_type=F32)
        bm = jnp.dot(sp_ref[...], vb, preferred_element_type=F32)
        bk = bm * kim
        yre = a * kre + jnp.where(first, 0.0, bk)
        yim = jnp.where(first, bk, a * kim - bm * kre)
        y = (jnp.dot(ci_ref[...], yre.astype(BF16), preferred_element_type=F32)
             + jnp.dot(sip_ref[...], yim.astype(BF16), preferred_element_type=F32))
        o_ref[s] = (x0_ref[s].astype(F32) * (y + vb.astype(F32) * skip_ref[...])).astype(BF16)


def _hy_conv(x0, v, kre, kim, skip, tables_bf16, *, seq_len):
    t, d = x0.shape
    L = seq_len
    nb = t // L
    sb = max(1, min(nb, 1024 // L))
    dc = 512
    nd = d // dc
    once = pl.Buffered(1)
    zspec = pl.BlockSpec((sb, L, dc), lambda j, b: (b, 0, j))
    kspec = pl.BlockSpec((L, dc), lambda j, b: (0, j), pipeline_mode=once)
    mspec = pl.BlockSpec((L, L), lambda j, b: (0, 0), pipeline_mode=once)
    return pl.pallas_call(
        _hy_conv_kernel,
        grid=(nd, nb // sb),
        in_specs=[zspec, zspec, kspec, kspec, pl.BlockSpec((1, dc), lambda j, b: (0, j)),
                  mspec, mspec, mspec, mspec],
        out_specs=zspec,
        out_shape=jax.ShapeDtypeStruct((nb, L, d), BF16),
        compiler_params=_cparams(("parallel", "arbitrary")),
        name="hyena_conv",
    )(x0.reshape(nb, L, d), v.reshape(nb, L, d), kre, kim, skip.reshape(1, d), *tables_bf16).reshape(t, d)


def _rope_tables(L, head_dim):
    rows = L // GRID_W
    row = np.repeat(np.arange(rows), GRID_W).astype(np.float64)
    col = np.tile(np.arange(GRID_W), rows).astype(np.float64)
    half = head_dim // 2
    inv = ROPE_BASE ** (-np.arange(0, half, 2, dtype=np.float64) / half)
    ar = row[:, None] * inv
    ac = col[:, None] * inv
    ang = np.concatenate([ar, ar, ac, ac], axis=-1)
    return (jnp.asarray(np.cos(ang).astype(np.float32)), jnp.asarray(np.sin(ang).astype(np.float32)))


def _rope(x, cos, sin):
    hd = x.shape[-1]
    q = hd // 4
    lane = lax.broadcasted_iota(jnp.int32, x.shape, 1)
    rot = jnp.where((lane % (2 * q)) < q, -pltpu.roll(x, hd - q, 1), pltpu.roll(x, q, 1))
    return x * cos + rot * sin


def _diff_lambda(lam_ref, lam_init):
    lp = lam_ref[...]
    s01 = jnp.sum(lp[0:1, :] * lp[1:2, :], axis=-1, keepdims=True)
    s23 = jnp.sum(lp[2:3, :] * lp[3:4, :], axis=-1, keepdims=True)
    return jnp.exp(s01) - jnp.exp(s23) + lam_init


def _diff_attend_head(q2, k_segs, v_segs, lam, g_sub, out_scale, hd):
    scale = hd ** -0.5
    outs = []
    for comp in range(2):
        qc = (q2[comp] * scale).astype(BF16)
        ss = [lax.dot_general(qc, ks[comp], (((1,), (1,)), ((), ())), preferred_element_type=F32)
              for ks in k_segs]
        m = ss[0].max(axis=-1, keepdims=True)
        for s in ss[1:]:
            m = jnp.maximum(m, s.max(axis=-1, keepdims=True))
        den = None
        pv = None
        for s, v in zip(ss, v_segs):
            e = jnp.exp(s - m)
            dsum = e.sum(axis=-1, keepdims=True)
            den = dsum if den is None else den + dsum
            part = jnp.dot(e.astype(BF16), v, preferred_element_type=F32)
            pv = part if pv is None else pv + part
        outs.append(pv * (1.0 / den))
    o = outs[0] - lam * outs[1]
    return o * _rms(o) * (g_sub * out_scale)


def _attn_ctx_kernel(q_ref, k_ref, v_ref, lam_ref, g_ref, o_ref, *, lam_init, hd):
    lam = _diff_lambda(lam_ref, lam_init)
    vd = 2 * hd
    for h in range(N_HEADS):
        c0 = h * vd
        q2 = [q_ref[:, c0:c0 + hd], q_ref[:, c0 + hd:c0 + vd]]
        k2 = (k_ref[:, c0:c0 + hd].astype(BF16), k_ref[:, c0 + hd:c0 + vd].astype(BF16))
        v = v_ref[:, c0:c0 + vd].astype(BF16)
        o = _diff_attend_head(q2, [k2], [v], lam, g_ref[...], 1.0 - lam_init, hd)
        o_ref[:, c0:c0 + vd] = o.astype(BF16)


def _attn_ctx(q, k, v, at_lam, g_sub, *, seq_len, lam_init):
    t, d = q.shape
    hd = d // N_HEADS // 2
    spec = pl.BlockSpec((seq_len, d), lambda b: (b, 0))
    return pl.pallas_call(
        functools.partial(_attn_ctx_kernel, lam_init=lam_init, hd=hd),
        grid=(t // seq_len,),
        in_specs=[spec, spec, spec,
                  pl.BlockSpec(at_lam.shape, lambda b: (0, 0)),
                  pl.BlockSpec((1, 2 * hd), lambda b: (0, 0))],
        out_specs=spec,
        out_shape=jax.ShapeDtypeStruct((t, d), BF16),
        compiler_params=_cparams(("parallel",)),
        name="attn_ctx",
    )(q, k, v, at_lam, g_sub.reshape(1, 2 * hd))


def _attn_lat_kernel(q_ref, k_ref, v_ref, ck_ref, cv_ref, cq_ref, sq_ref, ckk_ref, skk_ref,
                     lam_ref, g_ref, o_ref, kl_scr, kc_scr, vl_scr, vc_scr, *, lam_init, hd, sub):
    @pl.when(pl.program_id(2) == 0)
    def _():
        ckk, skk = ckk_ref[...], skk_ref[...]
        kl_scr[:, 0:hd] = _rope(k_ref[:, 0:hd], ckk, skk).astype(BF16)
        kl_scr[:, hd:2 * hd] = _rope(k_ref[:, hd:2 * hd], ckk, skk).astype(BF16)
        kc_scr[...] = ck_ref[...].astype(BF16)
        vl_scr[...] = v_ref[...].astype(BF16)
        vc_scr[...] = cv_ref[...].astype(BF16)

    lam = _diff_lambda(lam_ref, lam_init)
    kl = (kl_scr[:, 0:hd], kl_scr[:, hd:2 * hd])
    kc = (kc_scr[:, 0:hd], kc_scr[:, hd:2 * hd])
    for r0 in range(0, q_ref.shape[0], sub):
        rows = slice(r0, r0 + sub)
        cq, sq = cq_ref[rows, :], sq_ref[rows, :]
        q2 = [_rope(q_ref[rows, 0:hd], cq, sq), _rope(q_ref[rows, hd:2 * hd], cq, sq)]
        o = _diff_attend_head(q2, [kc, kl], [vc_scr[...], vl_scr[...]], lam, g_ref[...],
                              1.0 - lam_init, hd)
        o_ref[rows, :] = o.astype(BF16)


def _attn_lat(q, k, v, cache_k, cache_v, at_lam, g_sub, *, seq_len, lam_init):
    t, d = q.shape
    hd = d // N_HEADS // 2
    vd = 2 * hd
    nb = t // seq_len
    past = cache_k.shape[1]
    qb = min(512, seq_len)
    nq = seq_len // qb
    cos, sin = _rope_tables(seq_len, hd)
    return pl.pallas_call(
        functools.partial(_attn_lat_kernel, lam_init=lam_init, hd=hd, sub=min(256, qb)),
        grid=(nb, N_HEADS, nq),
        in_specs=[
            pl.BlockSpec((qb, vd), lambda b, h, i: (b * nq + i, h)),
            pl.BlockSpec((seq_len, vd), lambda b, h, i: (b, h)),
            pl.BlockSpec((seq_len, vd), lambda b, h, i: (b, h)),
            pl.BlockSpec((None, past, vd), lambda b, h, i: (b, 0, h)),
            pl.BlockSpec((None, past, vd), lambda b, h, i: (b, 0, h)),
            pl.BlockSpec((qb, hd), lambda b, h, i: (i, 0)),
            pl.BlockSpec((qb, hd), lambda b, h, i: (i, 0)),
            pl.BlockSpec((seq_len, hd), lambda b, h, i: (0, 0)),
            pl.BlockSpec((seq_len, hd), lambda b, h, i: (0, 0)),
            pl.BlockSpec(at_lam.shape, lambda b, h, i: (0, 0)),
            pl.BlockSpec((1, vd), lambda b, h, i: (0, 0)),
        ],
        out_specs=pl.BlockSpec((qb, vd), lambda b, h, i: (b * nq + i, h)),
        out_shape=jax.ShapeDtypeStruct((t, d), BF16),
        scratch_shapes=[pltpu.VMEM((seq_len, vd), BF16), pltpu.VMEM((past, vd), BF16),
                        pltpu.VMEM((seq_len, vd), BF16), pltpu.VMEM((past, vd), BF16)],
        compiler_params=_cparams(("parallel", "parallel", "arbitrary")),
        name="attn_lat",
    )(q, k, v, cache_k, cache_v, cos, sin, cos, sin, at_lam, g_sub.reshape(1, vd))


def _s5_prep_kernel(lre_r, lim_r, ldt_r, lre_c, lim_c, ldt_c, btre_ref, btim_ref, ctre_ref, ctim_ref,
                    wst_ref, t_ref, wout_ref, lam_ref):
    Q = S5_CHUNK
    H = S5_GROUP
    P2 = lre_r.shape[-1]
    P = P2 // 2
    N = Q * H

    def cexp(n, re_dt, im_dt):
        mag = jnp.exp(n * re_dt)
        return mag * jnp.cos(n * im_dt), mag * jnp.sin(n * im_dt)

    re = jnp.minimum(lre_r[...], -1e-4)
    im = lim_r[...]
    dt = jnp.exp(ldt_r[...])
    re_dt, im_dt = re * dt, im * dt
    lb_re, lb_im = cexp(1.0, re_dt, im_dt)
    den = re * re + im * im
    q_re = ((lb_re - 1.0) * re + lb_im * im) / den
    q_im = (lb_im * re - (lb_re - 1.0) * im) / den
    bt_re, bt_im = btre_ref[...], btim_ref[...]
    bb_re = q_re * bt_re - q_im * bt_im
    bb_im = q_re * bt_im + q_im * bt_re
    fwd_lane = lax.broadcasted_iota(jnp.int32, (Q, P2), 1) < P
    srow = lax.broadcasted_iota(jnp.int32, (Q, P2), 0)
    n = jnp.where(fwd_lane, (Q - 1) - srow, srow).astype(F32)
    pw_re, pw_im = cexp(n, re_dt, im_dt)
    for s in range(Q):
        pr, pi = pw_re[s:s + 1, :], pw_im[s:s + 1, :]
        wst_ref[s * H:(s + 1) * H, 0:P2] = (bb_re * pr - bb_im * pi).astype(BF16)
        wst_ref[s * H:(s + 1) * H, P2:2 * P2] = (bb_re * pi + bb_im * pr).astype(BF16)
    lq_re, lq_im = cexp(float(Q), re_dt, im_dt)
    lam_ref[0:1, :] = lq_re
    lam_ref[1:2, :] = lq_im

    rec = jnp.minimum(lre_c[...], -1e-4)
    imc = lim_c[...]
    dtc = jnp.exp(ldt_c[...])
    rec_dt, imc_dt = rec * dtc, imc * dtc
    l1_re, l1_im = cexp(1.0, rec_dt, imc_dt)
    tlane = lax.broadcasted_iota(jnp.int32, (P2, N), 1) // H
    fwd_row = lax.broadcasted_iota(jnp.int32, (P2, N), 0) < P
    nt = jnp.where(fwd_row, tlane, (Q - 1) - tlane)
    g_re = jnp.ones((P2, N), F32)
    g_im = jnp.zeros((P2, N), F32)
    b_re, b_im = l1_re, l1_im
    bit = 1
    while bit < Q:
        use = (nt & bit) != 0
        f_re = jnp.where(use, b_re, 1.0)
        f_im = jnp.where(use, b_im, 0.0)
        g_re, g_im = g_re * f_re - g_im * f_im, g_re * f_im + g_im * f_re
        b_re, b_im = b_re * b_re - b_im * b_im, 2.0 * b_re * b_im
        bit *= 2
    c_re, c_im = ctre_ref[...], ctim_ref[...]
    gx_re = c_re * g_re - c_im * g_im
    gx_im = c_re * g_im + c_im * g_re
    wout_ref[0:P2, :] = (gx_re * l1_re - gx_im * l1_im).astype(BF16)
    wout_ref[P2:2 * P2, :] = (-(gx_re * l1_im + gx_im * l1_re)).astype(BF16)

    gx = jnp.concatenate([gx_re, gx_im], axis=0)
    is_f = lax.broadcasted_iota(jnp.int32, (H, P2), 1) < P
    zero = jnp.zeros((H, P2), F32)
    lhs_f = jnp.concatenate([jnp.where(is_f, bb_re, zero), jnp.where(is_f, -bb_im, zero)], axis=1)
    lhs_b = jnp.concatenate([jnp.where(is_f, zero, bb_re), jnp.where(is_f, zero, -bb_im)], axis=1)
    m_f = jnp.dot(lhs_f, gx, precision=HIGHEST, preferred_element_type=F32)
    m_b = jnp.dot(lhs_b, gx, precision=HIGHEST, preferred_element_type=F32)
    lane_n = lax.broadcasted_iota(jnp.int32, (H, N), 1)
    for s in range(Q):
        tf = m_f if s == 0 else pltpu.roll(m_f, s * H, 1)
        tb = m_b if s == Q - 1 else pltpu.roll(m_b, (s + 1) * H, 1)
        slab = jnp.where(lane_n >= s * H, tf, 0.0) + jnp.where(lane_n < (s + 1) * H, tb, 0.0)
        t_ref[s * H:(s + 1) * H, :] = slab.astype(BF16)


def _s5_prep(lam_re, lam_im, log_dt, b_re, b_im, c_re, c_im):
    _, G, P = lam_re.shape
    H = S5_GROUP
    N = S5_CHUNK * H
    P2 = 2 * P
    fb_lanes = lambda a: jnp.concatenate([a[0], a[1]], axis=-1)
    ldt = jnp.broadcast_to(log_dt[..., None], (2, G, P))
    rows = [fb_lanes(a).reshape(G, 1, P2) for a in (lam_re, lam_im, ldt)]
    cols = [fb_lanes(a).reshape(G, P2, 1) for a in (lam_re, lam_im, ldt)]
    bt = [fb_lanes(jnp.swapaxes(a, -1, -2)) for a in (b_re, b_im)]
    ct = [jnp.tile(jnp.concatenate([jnp.swapaxes(a[0], -1, -2), jnp.swapaxes(a[1], -1, -2)], axis=1),
                   (1, 1, S5_CHUNK)) for a in (c_re, c_im)]
    ins = rows + cols + bt + ct
    gspec = lambda a: pl.BlockSpec((None,) + a.shape[1:], lambda g: (g, 0, 0))
    mspec = pl.BlockSpec((None, N, N), lambda g: (g, 0, 0))
    msh = jax.ShapeDtypeStruct((G, N, N), BF16)
    return pl.pallas_call(
        _s5_prep_kernel,
        grid=(G,),
        in_specs=[gspec(a) for a in ins],
        out_specs=[mspec, mspec, mspec, pl.BlockSpec((None, 2, P2), lambda g: (g, 0, 0))],
        out_shape=[msh, msh, msh, jax.ShapeDtypeStruct((G, 2, P2), F32)],
        compiler_params=_cparams(("parallel",)),
        name="s5_prep",
    )(*ins)


def _s5_core_kernel(h_ref, wst_ref, t_ref, wout_ref, lam_ref, s0_ref, y_ref, fin_ref,
                    u_scr, loc, sa, sb, *, nb, nc, gl):
    Q, H, GB = S5_CHUNK, S5_GROUP, S5_GROUPS_PER_STEP
    R = nb * nc
    P2 = lam_ref.shape[-1]
    lane_blk = lax.broadcasted_iota(jnp.int32, (R, LANES), 1) // H

    def gather_blocks(pieces, src_blk):
        acc = None
        for b, piece in enumerate(pieces):
            shift = ((b - src_blk) % GB) * H
            r = piece if shift == 0 else pltpu.roll(piece, shift, 1)
            acc = r if acc is None else jnp.where(lane_blk == b, r, acc)
        return acc

    hi_mask = jnp.int32(-65536)

    def pack2(a, b):
        abits = lax.bitcast_convert_type(a.astype(BF16).astype(F32), jnp.int32)
        bbits = lax.bitcast_convert_type(b.astype(BF16).astype(F32), jnp.int32)
        return (abits & hi_mask) | lax.shift_right_logical(bbits, 16)

    def unpack2(p):
        return [lax.bitcast_convert_type(p & hi_mask, F32),
                lax.bitcast_convert_type(lax.shift_left(p, 16), F32)]

    packed = [pack2(h_ref[:, s, :, :].reshape(R, LANES), h_ref[:, s + GB, :, :].reshape(R, LANES))
              for s in range(GB)]
    for g in range(GB):
        u = jnp.concatenate(unpack2(gather_blocks(packed, g)), axis=-1).astype(BF16)
        u_scr[g] = u
        loc[g] = jnp.dot(u, wst_ref[g], preferred_element_type=F32)

    fwd_half = lax.broadcasted_iota(jnp.int32, (nb, P2), 1) < P2 // 2
    for g0 in range(0, GB, gl):
        def body(k, carry, g0=g0):
            rf = pl.multiple_of(k * nb, nb)
            rb = pl.multiple_of((nc - 1 - k) * nb, nb)
            out = []
            for gi in range(gl):
                g = g0 + gi
                xr, xi = carry[2 * gi], carry[2 * gi + 1]
                sa[g, pl.ds(rf, nb), 0:P2] = xr
                sa[g, pl.ds(rf, nb), P2:2 * P2] = xi
                sb[g, pl.ds(rb, nb), 0:P2] = xr
                sb[g, pl.ds(rb, nb), P2:2 * P2] = xi
                lr = jnp.where(fwd_half, loc[g, pl.ds(rf, nb), 0:P2], loc[g, pl.ds(rb, nb), 0:P2])
                li = jnp.where(fwd_half, loc[g, pl.ds(rf, nb), P2:2 * P2],
                               loc[g, pl.ds(rb, nb), P2:2 * P2])
                ar, ai = lam_ref[g, 0:1, :], lam_ref[g, 1:2, :]
                out += [ar * xr - ai * xi + lr, ar * xi + ai * xr + li]
            return tuple(out)

        init = []
        for gi in range(gl):
            init += [s0_ref[g0 + gi, :, 0:P2], s0_ref[g0 + gi, :, P2:2 * P2]]
        fin = lax.fori_loop(0, nc, body, tuple(init))
        for gi in range(gl):
            fin_ref[g0 + gi, :, 0:P2] = fin[2 * gi]
            fin_ref[g0 + gi, :, P2:2 * P2] = fin[2 * gi + 1]

    fsel = (lax.broadcasted_iota(jnp.int32, (R, 2 * P2), 1) % P2) < P2 // 2
    for g in range(GB):
        s_in = jnp.where(fsel, sa[g], sb[g]).astype(BF16)
        loc[g] = (jnp.dot(u_scr[g], t_ref[g], preferred_element_type=F32)
                  + jnp.dot(s_in, wout_ref[g], preferred_element_type=F32))
    ys = [pack2(loc[g, :, 0:LANES], loc[g, :, LANES:2 * LANES]) for g in range(GB)]
    for tl in range(GB):
        lo, hi = unpack2(gather_blocks(ys, tl))
        y_ref[:, tl, :, :] = lo.reshape(nc, nb, LANES)
        y_ref[:, tl + GB, :, :] = hi.reshape(nc, nb, LANES)


def _s5_core(h_tm, s0, wst, tmat, wout, lam, *, nb, seq_len):
    d = h_tm.shape[1] // nb
    H, Q, GB = S5_GROUP, S5_CHUNK, S5_GROUPS_PER_STEP
    G = d // H
    nc = seq_len // Q
    R = nb * nc
    N = Q * H
    P4 = s0.shape[-1]
    gl = max(1, min(GB, (8 * GB) // nb))
    h4 = h_tm.reshape(nc, Q, nb, d)
    hspec = pl.BlockSpec((nc, Q, nb, LANES), lambda gb: (0, 0, 0, gb))
    mspec = pl.BlockSpec((GB, N, N), lambda gb: (gb, 0, 0))
    sspec = pl.BlockSpec((GB, nb, P4), lambda gb: (gb, 0, 0))
    y, fin = pl.pallas_call(
        functools.partial(_s5_core_kernel, nb=nb, nc=nc, gl=gl),
        grid=(G // GB,),
        in_specs=[hspec, mspec, mspec, mspec,
                  pl.BlockSpec((GB, 2, P4 // 2), lambda gb: (gb, 0, 0)), sspec],
        out_specs=[hspec, sspec],
        out_shape=[jax.ShapeDtypeStruct((nc, Q, nb, d), F32), jax.ShapeDtypeStruct((G, nb, P4), F32)],
        scratch_shapes=[pltpu.VMEM((GB, R, N), BF16), pltpu.VMEM((GB, R, N), F32),
                        pltpu.VMEM((GB, R, P4), F32), pltpu.VMEM((GB, R, P4), F32)],
        compiler_params=_cparams(("parallel",)),
        name="s5_core",
    )(h4, wst, tmat, wout, lam, s0)
    return y.reshape(seq_len, nb * d), fin


def _glu_postadd_kernel(*refs, nsub):
    h_refs, y_refs = refs[:nsub], refs[nsub:2 * nsub]
    (d_ref, wa_ref, wg_ref, ba_ref, bg_ref, x_ref, gate_ref, g_ref, o_ref, acc_a, acc_g, u_scr) = refs[2 * nsub:]
    k = pl.program_id(1)

    @pl.when(k == 0)
    def _():
        acc_a[...] = jnp.zeros_like(acc_a)
        acc_g[...] = jnp.zeros_like(acc_g)

    rows = h_refs[0].shape[0]
    for s in range(nsub):
        u_scr[s * rows:(s + 1) * rows, :] = jax.nn.gelu(
            d_ref[...] * h_refs[s][...] + y_refs[s][...]).astype(BF16)
    u = u_scr[...]
    acc_a[...] += jnp.dot(u, wa_ref[...], preferred_element_type=F32)
    acc_g[...] += jnp.dot(u, wg_ref[...], preferred_element_type=F32)

    @pl.when(k == pl.num_programs(1) - 1)
    def _():
        ba, bg = ba_ref[...], bg_ref[...]
        _post_add_rows(x_ref, lambda rows: (acc_a[rows, :] + ba) * jax.nn.sigmoid(acc_g[rows, :] + bg),
                       g_ref, gate_ref, o_ref)


def _glu_postadd(h_tm, y_tm, dskip, w_glu, layer, b_glu, x, m_l, g1, *, seq_len, latent):
    t, d = x.shape
    tm = min(512, t)
    nsub = max(1, tm // seq_len)
    per_seq = max(1, seq_len // tm)
    sub_rows = tm // nsub
    tk = 1024
    nk = d // tk
    rowfn = _row_fn(tm, seq_len, latent)
    b_glu = b_glu.reshape(1, 2 * d)
    tspecs = [pl.BlockSpec((sub_rows, tk), lambda i, k, s=s: (i % per_seq, ((i // per_seq) * nsub + s) * nk + k))
              for s in range(nsub)]
    return pl.pallas_call(
        functools.partial(_glu_postadd_kernel, nsub=nsub),
        grid=(t // tm, nk),
        in_specs=tspecs + tspecs + [
            pl.BlockSpec((1, tk), lambda i, k: (0, k)),
            pl.BlockSpec((None, None, tk, d), lambda i, k: (layer, 0, k, 0)),
            pl.BlockSpec((None, None, tk, d), lambda i, k: (layer, 1, k, 0)),
            pl.BlockSpec((1, d), lambda i, k: (0, 0)),
            pl.BlockSpec((1, d), lambda i, k: (0, 1)),
            pl.BlockSpec((tm, d), lambda i, k: (i, 0)),
            _mod_spec(d, 2, rowfn),
            pl.BlockSpec((1, d), lambda i, k: (0, 0)),
        ],
        out_specs=pl.BlockSpec((tm, d), lambda i, k: (i, 0)),
        out_shape=jax.ShapeDtypeStruct((t, d), F32),
        scratch_shapes=[pltpu.VMEM((tm, d), F32), pltpu.VMEM((tm, d), F32), pltpu.VMEM((tm, tk), BF16)],
        compiler_params=_cparams(("parallel", "arbitrary")),
        name="glu_postadd",
    )(*([h_tm] * nsub), *([y_tm] * nsub), dskip.reshape(1, d), w_glu, w_glu, b_glu, b_glu, x, m_l,
      g1.reshape(1, d))


def kernel(x_prompt, x_sample, cache_attn_k, cache_attn_v, state_s5_re, state_s5_im, c, c_ctx, w_mod, b_mod, g_norm, w_mlp_in, w_mlp_out, hy_w_in, hy_b_in, hy_w_short, hy_b_short, hy_f_w1, hy_f_b1, hy_f_freq1, hy_f_w2, hy_f_b2, hy_f_freq2, hy_f_w3, hy_log_alpha, hy_skip, hy_w_out, hy_b_out, at_w_qkv, at_lam, at_g_sub, at_w_o, s5_lam_re, s5_lam_im, s5_log_dt, s5_b_re, s5_b_im, s5_c_re, s5_c_im, s5_d, s5_w_glu, s5_b_glu):
    bc, lc, d = x_prompt.shape
    bl, ll, _ = x_sample.shape
    depth = w_mod.shape[0]
    assert 1 + bl <= MOD_ROWS
    assert cache_attn_k.shape[1] == 1 and state_s5_re.shape[1] == 1, "one attention and one S5 layer"
    hd = d // N_HEADS // 2
    G = d // S5_GROUP
    P = s5_lam_re.shape[-1]

    cond = jnp.concatenate([c_ctx[None], c, jnp.zeros((MOD_ROWS - 1 - bl, d), F32)], axis=0)
    mod = _modulation(cond, w_mod, b_mod).reshape(depth, MOD_ROWS, N_MOD, 1, d)

    streams = [dict(seq_len=lc, latent=False), dict(seq_len=ll, latent=True)]
    xs = [x_prompt.reshape(bc * lc, d), x_sample.reshape(bl * ll, d)]

    tables = {}
    for L in {lc, ll}:
        cm, sm, sp, ci, sip = _dft_tables(L)
        tables[L] = (_split_bf16(cm) + _split_bf16(sm), tuple(a.astype(BF16) for a in (cm, sp, ci, sip)))

    w_mlp_in_b = _cast_tiles(w_mlp_in, MLP_TF)
    w_mlp_out_b = _cast_tiles(w_mlp_out.reshape(-1, MLP_TF, d), d).reshape(w_mlp_out.shape)
    hy_w_in_b = _cast_tiles(hy_w_in, PROJ_TN)
    at_w_qkv_b = _cast_tiles(at_w_qkv, PROJ_TN)
    hy_w_out_b = _cast_tiles(hy_w_out, d).reshape(hy_w_out.shape)
    at_w_o_b = _cast_tiles(at_w_o, d).reshape(at_w_o.shape)
    s5_w_glu_b = _cast_tiles(s5_w_glu, d)

    new_k = new_v = None
    fin_ctx = None
    for i in range(depth):
        kind, j = i % 3, i // 3
        m_l = mod[i]
        g = g_norm[i]
        if kind == 0:
            filt = {}
            for L in {lc, ll}:
                filt[L] = _hy_filter(L, hy_f_w1[j], hy_f_b1[j], hy_f_freq1[j], hy_f_w2[j], hy_f_b2[j],
                                     hy_f_freq2[j], hy_f_w3[j], hy_log_alpha[j], tables[L][0])
            for si, st in enumerate(streams):
                L = st["seq_len"]
                x0, vg = _hy_inproj(xs[si], m_l, g[0], hy_w_in_b, j, hy_b_in[j], hy_w_short[j],
                                    hy_b_short[j], **st)
                a = _hy_conv(x0, vg, filt[L][0], filt[L][1], hy_skip[j], tables[L][1], seq_len=L)
                xs[si] = _mm_postadd(a, hy_w_out_b, j, hy_b_out[j], xs[si], m_l, g[1], **st)
        elif kind == 1:
            lam_init = 0.8 - 0.6 * math.exp(-0.3 * i)
            for si, st in enumerate(streams):
                L = st["seq_len"]
                q, k, v = _premod_mm3(xs[si], m_l, g[0], at_w_qkv_b, j, **st)
                if not st["latent"]:
                    new_k = k.reshape(bc, 1, lc, N_HEADS, 2, hd)
                    new_v = v.reshape(bc, 1, lc, N_HEADS, 2 * hd)
                    a = _attn_ctx(q, k, v, at_lam[j], at_g_sub[j], seq_len=L, lam_init=lam_init)
                else:
                    ck = cache_attn_k[:, j].reshape(bl, -1, d)
                    cv = cache_attn_v[:, j].reshape(bl, -1, d)
                    a = _attn_lat(q, k, v, ck, cv, at_lam[j], at_g_sub[j], seq_len=L, lam_init=lam_init)
                xs[si] = _mm_postadd(a, at_w_o_b, j, jnp.zeros((d,), F32), xs[si], m_l, g[1], **st)
        else:
            wst, tmat, wout, lam_q = _s5_prep(s5_lam_re[j], s5_lam_im[j], s5_log_dt[j], s5_b_re[j],
                                              s5_b_im[j], s5_c_re[j], s5_c_im[j])
            for si, st in enumerate(streams):
                L = st["seq_len"]
                nb = xs[si].shape[0] // L
                if st["latent"]:
                    sre, sim = state_s5_re[:, j], state_s5_im[:, j]
                    s0 = jnp.concatenate([sre[:, 0], sre[:, 1], sim[:, 0], sim[:, 1]], axis=-1)
                    s0 = s0.transpose(1, 0, 2)
                else:
                    s0 = jnp.zeros((G, nb, 4 * P), F32)
                h_tm = _premod_time_major(xs[si], m_l, g[0], **st)
                y_tm, fin = _s5_core(h_tm, s0, wst, tmat, wout, lam_q, nb=nb, seq_len=L)
                if not st["latent"]:
                    fin_ctx = fin.reshape(G, nb, 2, 2, P).transpose(1, 2, 3, 0, 4)
                xs[si] = _glu_postadd(h_tm, y_tm, s5_d[j], s5_w_glu_b, j, s5_b_glu[j], xs[si], m_l, g[1], **st)
        for si, st in enumerate(streams):
            xs[si] = _mlp(xs[si], m_l, g[2], g[3], w_mlp_in_b, w_mlp_out_b, i, **st)

    new_s_re = fin_ctx[:, 0][:, None]
    new_s_im = fin_ctx[:, 1][:, None]
    return (xs[0].reshape(bc, lc, d), xs[1].reshape(bl, ll, d), new_k, new_v, new_s_re, new_s_im)
```

```python
import functools
import math

import numpy as np
import jax
import jax.numpy as jnp
from jax import lax
from jax.experimental import pallas as pl
from jax.experimental.pallas import tpu as pltpu

F32 = jnp.float32
BF16 = jnp.bfloat16
HIGHEST = lax.Precision.HIGHEST

NORM_EPS = 1e-6
N_MOD = 6
N_HEADS = 8
GRID_W = 64
ROPE_BASE = 10000.0
HY_PE_BANDS = 16
HY_PE_MIN_PERIOD = 2.0
HY_PE_MAX_PERIOD = 4096.0
S5_GROUP = 16
S5_CHUNK = 16
LANES = 128
S5_GROUPS_PER_STEP = LANES // S5_GROUP
MOD_ROWS = 16
MLP_TF = 1024
PROJ_TN = 512

VMEM_LIMIT = 56 * 1024 * 1024


def _cparams(sem):
    return pltpu.CompilerParams(dimension_semantics=sem, vmem_limit_bytes=VMEM_LIMIT)


def _rms(x):
    return lax.rsqrt(jnp.mean(x * x, axis=-1, keepdims=True) + NORM_EPS)


def _split_bf16(a):
    hi = a.astype(BF16)
    return hi, (a - hi.astype(F32)).astype(BF16)


def _cast_kernel(w_ref, o_ref):
    o_ref[...] = w_ref[...].astype(BF16)


def _cast_tiles(w, tc):
    n, k, m = w.shape
    rk = min(k, (2 * 1024 * 1024) // tc)
    return pl.pallas_call(
        _cast_kernel,
        grid=(n, m // tc, k // rk),
        in_specs=[pl.BlockSpec((None, rk, tc), lambda l, j, r: (l, r, j))],
        out_specs=pl.BlockSpec((None, None, rk, tc), lambda l, j, r: (l, j, r, 0)),
        out_shape=jax.ShapeDtypeStruct((n, m // tc, k, tc), BF16),
        compiler_params=_cparams(("parallel", "parallel", "parallel")),
        name="cast_bf16",
    )(w)


def _mod_kernel(c_ref, w_ref, b_ref, o_ref):
    c = c_ref[...]
    s_hi, s_lo = _split_bf16(c * jax.nn.sigmoid(c))
    both = jnp.dot(jnp.concatenate([s_hi, s_lo], axis=0), w_ref[...].astype(BF16),
                   preferred_element_type=F32)
    o_ref[...] = both[:MOD_ROWS] + both[MOD_ROWS:] + b_ref[...]


def _modulation(cond, w_mod, b_mod):
    depth, d, n = w_mod.shape
    tn = 1024
    return pl.pallas_call(
        _mod_kernel,
        grid=(depth, n // tn),
        in_specs=[
            pl.BlockSpec((MOD_ROWS, d), lambda l, j: (0, 0)),
            pl.BlockSpec((None, d, tn), lambda l, j: (l, 0, j)),
            pl.BlockSpec((None, 1, tn), lambda l, j: (l, 0, j)),
        ],
        out_specs=pl.BlockSpec((None, MOD_ROWS, tn), lambda l, j: (l, 0, j)),
        out_shape=jax.ShapeDtypeStruct((depth, MOD_ROWS, n), F32),
        compiler_params=_cparams(("parallel", "parallel")),
        name="adaln_mod",
    )(cond, w_mod, b_mod.reshape(depth, 1, n))


def _mod_spec(d, which, rowfn):
    return pl.BlockSpec((None, None, 1, d), lambda i, *_: (rowfn(i), which, 0, 0))


def _lookahead_rows_spec(tm, d, nt):
    return pl.BlockSpec((tm, d), lambda i, j: (jnp.where(j >= 1, jnp.minimum(i + 1, nt - 1), i), 0))


def _row_fn(tm, seq_len, latent):
    if latent:
        assert seq_len % tm == 0, "a latent row tile must sit inside one sequence"
        return lambda i: 1 + (i * tm) // seq_len
    return lambda i: 0


ROW_CHUNK = 16


def _for_row_chunks(n_rows, fn):
    def body(c, carry):
        fn(pl.ds(pl.multiple_of(c * ROW_CHUNK, ROW_CHUNK), ROW_CHUNK))
        return carry

    lax.fori_loop(0, n_rows // ROW_CHUNK, body, 0, unroll=16)


def _premod_rows(x_ref, g_ref, sh_ref, sc_ref, out_ref, keep_ref=None):
    gs, sh = g_ref[...] * (1.0 + sc_ref[...]), sh_ref[...]

    def rows_fn(rows):
        x = x_ref[rows, :]
        if keep_ref is not None:
            keep_ref[rows, :] = x
        out_ref[rows, :] = ((x * _rms(x)) * gs + sh).astype(out_ref.dtype)

    _for_row_chunks(x_ref.shape[0], rows_fn)


def _post_add_rows(x_ref, o_fn, g_ref, gate_ref, out_ref):
    gg = gate_ref[...] * g_ref[...]

    def rows_fn(rows):
        o = o_fn(rows)
        out_ref[rows, :] = x_ref[rows, :] + (o * _rms(o)) * gg

    _for_row_chunks(x_ref.shape[0], rows_fn)


def _premod_mm3_kernel(x_ref, g_ref, sh_ref, sc_ref, wa_ref, wb_ref, wc_ref, oa_ref, ob_ref, oc_ref, h_scr):
    @pl.when(pl.program_id(1) == 0)
    def _():
        _premod_rows(x_ref, g_ref, sh_ref, sc_ref, h_scr)

    h = h_scr[...]
    for w_ref, o_ref in ((wa_ref, oa_ref), (wb_ref, ob_ref), (wc_ref, oc_ref)):
        o_ref[...] = jnp.dot(h, w_ref[...], preferred_element_type=F32)


def _premod_mm3(x, m_l, g, w, layer, *, seq_len, latent):
    t, d = x.shape
    tm = min(1024, seq_len if latent else t)
    tn = PROJ_TN
    nj = d // tn
    rowfn = _row_fn(tm, seq_len, latent)
    wspec = lambda o: pl.BlockSpec((None, None, d, tn), lambda i, j: (layer, o * nj + j, 0, 0))
    ospec = pl.BlockSpec((tm, tn), lambda i, j: (i, j))
    osh = jax.ShapeDtypeStruct((t, d), F32)
    return pl.pallas_call(
        _premod_mm3_kernel,
        grid=(t // tm, nj),
        in_specs=[
            _lookahead_rows_spec(tm, d, t // tm),
            pl.BlockSpec((1, d), lambda i, j: (0, 0)),
            _mod_spec(d, 0, rowfn),
            _mod_spec(d, 1, rowfn),
            wspec(0), wspec(1), wspec(2),
        ],
        out_specs=[ospec, ospec, ospec],
        out_shape=[osh, osh, osh],
        scratch_shapes=[pltpu.VMEM((tm, d), BF16)],
        compiler_params=_cparams(("parallel", "arbitrary")),
        name="premod_mm3",
    )(x, g.reshape(1, d), m_l, m_l, w, w, w)


def _premod_kernel(x_ref, g_ref, sh_ref, sc_ref, o_ref):
    _premod_rows(x_ref, g_ref, sh_ref, sc_ref, o_ref)


def _premod_time_major(x, m_l, g, *, seq_len, latent):
    t, d = x.shape
    nb = t // seq_len
    tm = min(512, seq_len)
    per_seq = seq_len // tm
    rowfn = _row_fn(tm, seq_len, latent)
    return pl.pallas_call(
        _premod_kernel,
        grid=(t // tm,),
        in_specs=[
            pl.BlockSpec((tm, d), lambda i: (i, 0)),
            pl.BlockSpec((1, d), lambda i: (0, 0)),
            _mod_spec(d, 0, rowfn),
            _mod_spec(d, 1, rowfn),
        ],
        out_specs=pl.BlockSpec((tm, d), lambda i: (i % per_seq, i // per_seq)),
        out_shape=jax.ShapeDtypeStruct((seq_len, nb * d), F32),
        compiler_params=_cparams(("parallel",)),
        name="premod",
    )(x, g.reshape(1, d), m_l, m_l)


def _mlp_kernel(x_ref, g2_ref, sh_ref, sc_ref, gate_ref, g3_ref, w1_ref, w2_ref, o_ref, h_scr, acc):
    f = pl.program_id(1)

    @pl.when(f == 0)
    def _():
        _premod_rows(x_ref, g2_ref, sh_ref, sc_ref, h_scr, keep_ref=o_ref)
        acc[...] = jnp.zeros_like(acc)

    a = jnp.dot(h_scr[...], w1_ref[...], preferred_element_type=F32)
    a = jnp.square(jnp.maximum(a, 0.0)).astype(BF16)
    acc[...] += jnp.dot(a, w2_ref[...], preferred_element_type=F32)

    @pl.when(f == pl.num_programs(1) - 1)
    def _():
        _post_add_rows(o_ref, lambda rows: acc[rows, :], g3_ref, gate_ref, o_ref)


def _mlp(x, m_l, g2, g3, w1, w2, layer, *, seq_len, latent):
    t, d = x.shape
    tf = MLP_TF
    dff = w1.shape[1] * tf
    tm = min(512, t)
    nt = t // tm
    rowfn = _row_fn(tm, seq_len, latent)
    return pl.pallas_call(
        _mlp_kernel,
        grid=(nt, dff // tf),
        in_specs=[
            _lookahead_rows_spec(tm, d, nt),
            pl.BlockSpec((1, d), lambda i, f: (0, 0)),
            _mod_spec(d, 3, rowfn),
            _mod_spec(d, 4, rowfn),
            _mod_spec(d, 5, rowfn),
            pl.BlockSpec((1, d), lambda i, f: (0, 0)),
            pl.BlockSpec((None, None, d, tf), lambda i, f: (layer, f, 0, 0)),
            pl.BlockSpec((None, tf, d), lambda i, f: (layer, f, 0)),
        ],
        out_specs=pl.BlockSpec((tm, d), lambda i, f: (i, 0)),
        out_shape=jax.ShapeDtypeStruct((t, d), F32),
        scratch_shapes=[pltpu.VMEM((tm, d), BF16), pltpu.VMEM((tm, d), F32)],
        compiler_params=_cparams(("parallel", "arbitrary")),
        name="mlp",
    )(x, g2.reshape(1, d), m_l, m_l, m_l, g3.reshape(1, d), w1, w2)


def _mm_postadd_kernel(a_ref, w_ref, b_ref, x_ref, gate_ref, g_ref, o_ref, acc):
    acc[...] = jnp.dot(a_ref[...], w_ref[...], preferred_element_type=F32)
    b = b_ref[...]
    _post_add_rows(x_ref, lambda rows: acc[rows, :] + b, g_ref, gate_ref, o_ref)


def _mm_postadd(a, w, layer, b, x, m_l, g1, *, seq_len, latent):
    t, d = x.shape
    tm = min(512, t)
    rowfn = _row_fn(tm, seq_len, latent)
    return pl.pallas_call(
        _mm_postadd_kernel,
        grid=(t // tm,),
        in_specs=[
            pl.BlockSpec((tm, d), lambda i: (i, 0)),
            pl.BlockSpec((None, d, d), lambda i: (layer, 0, 0)),
            pl.BlockSpec((1, d), lambda i: (0, 0)),
            pl.BlockSpec((tm, d), lambda i: (i, 0)),
            _mod_spec(d, 2, rowfn),
            pl.BlockSpec((1, d), lambda i: (0, 0)),
        ],
        out_specs=pl.BlockSpec((tm, d), lambda i: (i, 0)),
        out_shape=jax.ShapeDtypeStruct((t, d), F32),
        scratch_shapes=[pltpu.VMEM((tm, d), F32)],
        compiler_params=_cparams(("parallel",)),
        name="mm_postadd",
    )(a, w, b.reshape(1, d), x, m_l, g1.reshape(1, d))


def _dft_tables(L):
    idx = np.arange(L, dtype=np.int64)
    ang = np.pi * ((idx[:, None] * idx[None, :]) % (2 * L)).astype(np.float64) / L
    c = np.cos(ang)
    s = np.sin(ang)
    alt = np.where(idx % 2 == 0, 1.0, -1.0)
    sp = s.copy()
    sp[0, :] = alt
    wf = np.full((L,), 2.0)
    wf[0] = 1.0
    ci = c * wf[None, :] / (2 * L)
    sip = -s * 2.0 / (2 * L)
    sip[:, 0] = alt / (2 * L)
    f32 = lambda a: jnp.asarray(a.astype(np.float32))
    return f32(c), f32(s), f32(sp), f32(ci), f32(sip)


def _hy_pe(L):
    t = np.arange(L, dtype=np.float64)
    periods = HY_PE_MIN_PERIOD * (HY_PE_MAX_PERIOD / HY_PE_MIN_PERIOD) ** (
        np.arange(HY_PE_BANDS, dtype=np.float64) / (HY_PE_BANDS - 1))
    ang = t[:, None] * (2.0 * math.pi / periods)[None]
    return jnp.asarray(np.concatenate([np.sin(ang), np.cos(ang)], axis=-1).astype(np.float32))


def _dot_bf16x3(a_hi, a_lo, b):
    b_hi, b_lo = _split_bf16(b)
    n = b.shape[1]
    both = jnp.dot(a_hi, jnp.concatenate([b_hi, b_lo], axis=1), preferred_element_type=F32)
    return both[:, :n] + both[:, n:] + jnp.dot(a_lo, b_hi, preferred_element_type=F32)


def _hy_filter_kernel(pe_ref, w1_ref, b1_ref, fr1_ref, w2_ref, b2_ref, fr2_ref, w3f_ref, w3b_ref,
                      la_ref, chi_ref, clo_ref, shi_ref, slo_ref, kre_ref, kim_ref, h_scr):
    L, dc = kre_ref.shape

    @pl.when(pl.program_id(0) == 0)
    def _():
        h1 = jnp.sin(fr1_ref[...] * (jnp.dot(pe_ref[...], w1_ref[...], precision=HIGHEST,
                                             preferred_element_type=F32) + b1_ref[...]))
        h_scr[...] = jnp.sin(fr2_ref[...] * (jnp.dot(h1, w2_ref[...], precision=HIGHEST,
                                                     preferred_element_type=F32) + b2_ref[...]))

    h = h_scr[...]
    row = lax.broadcasted_iota(jnp.int32, (L, dc), 0)
    dec = jnp.exp(-jnp.exp(la_ref[...]) * row.astype(F32))
    kf = jnp.dot(h, w3f_ref[...], precision=HIGHEST, preferred_element_type=F32) * dec
    kb = jnp.dot(h, w3b_ref[...], precision=HIGHEST, preferred_element_type=F32) * dec
    kb = jnp.where(row == 0, 0.0, kb)
    norm = jnp.sum(jnp.abs(kf) + jnp.abs(kb), axis=0, keepdims=True) + 1e-6
    inv = 1.0 / norm
    ks = (kf + kb) * inv
    kd = (kb - kf) * inv
    kre = _dot_bf16x3(chi_ref[...], clo_ref[...], ks)
    kim = _dot_bf16x3(shi_ref[...], slo_ref[...], kd)
    alt = jnp.where(row % 2 == 0, 1.0, -1.0)
    nyq = jnp.sum(alt * ks, axis=0, keepdims=True)
    kre_ref[...] = kre
    kim_ref[...] = jnp.where(row == 0, nyq, kim)


def _hy_filter(L, w1, b1, fr1, w2, b2, fr2, w3, log_alpha, cs_split):
    d = log_alpha.shape[-1]
    fw = w1.shape[1]
    dc = 512
    nd = d // dc
    full = lambda a: pl.BlockSpec(a.shape, lambda j: (0,) * a.ndim)
    pe = _hy_pe(L)
    b1, fr1, b2, fr2 = (a.reshape(1, fw) for a in (b1, fr1, b2, fr2))
    osh = jax.ShapeDtypeStruct((L, d), F32)
    return pl.pallas_call(
        _hy_filter_kernel,
        grid=(nd,),
        in_specs=[full(pe), full(w1), full(b1), full(fr1), full(w2), full(b2), full(fr2),
                  pl.BlockSpec((fw, dc), lambda j: (0, j)),
                  pl.BlockSpec((fw, dc), lambda j: (0, nd + j)),
                  pl.BlockSpec((1, dc), lambda j: (0, j))] + [full(a) for a in cs_split],
        out_specs=[pl.BlockSpec((L, dc), lambda j: (0, j))] * 2,
        out_shape=[osh, osh],
        scratch_shapes=[pltpu.VMEM((L, fw), F32)],
        compiler_params=_cparams(("arbitrary",)),
        name="hyena_filter",
    )(pe, w1, b1, fr1, w2, b2, fr2, w3, w3, log_alpha.reshape(1, d), *cs_split)


def _hy_inproj_kernel(x_ref, g_ref, sh_ref, sc_ref, w0_ref, w1_ref, wv_ref, b0_ref, b1_ref, bv_ref,
                      s0_ref, s1_ref, sv_ref, c0_ref, c1_ref, cv_ref, ox_ref, ov_ref, h_scr, *, seq_len):
    @pl.when(pl.program_id(1) == 0)
    def _():
        _premod_rows(x_ref, g_ref, sh_ref, sc_ref, h_scr)

    h = h_scr[...]
    tm, tn = ox_ref.shape
    pos = lax.broadcasted_iota(jnp.int32, (tm, tn), 0) % seq_len
    first, last = pos == 0, pos == seq_len - 1

    def section(w_ref, b_ref, s_ref, c_ref):
        z = jnp.dot(h, w_ref[...], preferred_element_type=F32) + b_ref[...]
        zm = jnp.where(first, 0.0, pltpu.roll(z, 1, 0))
        zp = jnp.where(last, 0.0, pltpu.roll(z, tm - 1, 0))
        return zm * s_ref[0:1, :] + z * s_ref[1:2, :] + zp * s_ref[2:3, :] + c_ref[...]

    ox_ref[...] = section(w0_ref, b0_ref, s0_ref, c0_ref).astype(ox_ref.dtype)
    x1 = section(w1_ref, b1_ref, s1_ref, c1_ref)
    ov_ref[...] = (section(wv_ref, bv_ref, sv_ref, cv_ref) * x1).astype(ov_ref.dtype)


def _hy_inproj(x, m_l, g, w, layer, b, w_sh, b_sh, *, seq_len, latent):
    t, d = x.shape
    tm = min(1024, seq_len if latent else t)
    assert tm % seq_len == 0, "row tiles hold whole sequences, so the short conv needs no halo"
    tn = PROJ_TN
    nj = d // tn
    rowfn = _row_fn(tm, seq_len, latent)
    wspec = lambda o: pl.BlockSpec((None, None, d, tn), lambda i, j: (layer, o * nj + j, 0, 0))
    vspec = lambda rows: (lambda o: pl.BlockSpec((rows, tn), lambda i, j: (0, o * nj + j)))
    bspec, sspec = vspec(1), vspec(3)
    ospec = pl.BlockSpec((tm, tn), lambda i, j: (i, j))
    osh = jax.ShapeDtypeStruct((t, d), BF16)
    b = b.reshape(1, 3 * d)
    b_sh = b_sh.reshape(1, 3 * d)
    return pl.pallas_call(
        functools.partial(_hy_inproj_kernel, seq_len=seq_len),
        grid=(t // tm, nj),
        in_specs=[
            _lookahead_rows_spec(tm, d, t // tm),
            pl.BlockSpec((1, d), lambda i, j: (0, 0)),
            _mod_spec(d, 0, rowfn),
            _mod_spec(d, 1, rowfn),
            wspec(0), wspec(1), wspec(2), bspec(0), bspec(1), bspec(2),
            sspec(0), sspec(1), sspec(2), bspec(0), bspec(1), bspec(2),
        ],
        out_specs=[ospec, ospec],
        out_shape=[osh, osh],
        scratch_shapes=[pltpu.VMEM((tm, d), BF16)],
        compiler_params=_cparams(("parallel", "arbitrary")),
        name="hyena_inproj",
    )(x, g.reshape(1, d), m_l, m_l, w, w, w, b, b, b, w_sh, w_sh, w_sh, b_sh, b_sh, b_sh)


def _hy_conv_kernel(x0_ref, v_ref, kre_ref, kim_ref, skip_ref, c_ref, sp_ref, ci_ref, sip_ref, o_ref):
    sb, L, dc = o_ref.shape
    kre = kre_ref[...]
    kim = kim_ref[...]
    first = lax.broadcasted_iota(jnp.int32, (L, dc), 0) == 0
    for s in range(sb):
        vb = v_ref[s]
        a = jnp.dot(c_ref[...], vb, preferred_element_type=F32)
        bm = jnp.dot(sp_ref[...], vb, preferred_element_type=F32)
        bk = bm * kim
        yre = a * kre + jnp.where(first, 0.0, bk)
        yim = jnp.where(first, bk, a * kim - bm * kre)
        y = (jnp.dot(ci_ref[...], yre.astype(BF16), preferred_element_type=F32)
             + jnp.dot(sip_ref[...], yim.astype(BF16), preferred_element_type=F32))
        o_ref[s] = (x0_ref[s].astype(F32) * (y + vb.astype(F32) * skip_ref[...])).astype(BF16)


def _hy_conv(x0, v, kre, kim, skip, tables_bf16, *, seq_len):
    t, d = x0.shape
    L = seq_len
    nb = t // L
    sb = max(1, min(nb, 1024 // L))
    dc = 512
    nd = d // dc
    once = pl.Buffered(1)
    zspec = pl.BlockSpec((sb, L, dc), lambda j, b: (b, 0, j))
    kspec = pl.BlockSpec((L, dc), lambda j, b: (0, j), pipeline_mode=once)
    mspec = pl.BlockSpec((L, L), lambda j, b: (0, 0), pipeline_mode=once)
    return pl.pallas_call(
        _hy_conv_kernel,
        grid=(nd, nb // sb),
        in_specs=[zspec, zspec, kspec, kspec, pl.BlockSpec((1, dc), lambda j, b: (0, j)),
                  mspec, mspec, mspec, mspec],
        out_specs=zspec,
        out_shape=jax.ShapeDtypeStruct((nb, L, d), BF16),
        compiler_params=_cparams(("parallel", "arbitrary")),
        name="hyena_conv",
    )(x0.reshape(nb, L, d), v.reshape(nb, L, d), kre, kim, skip.reshape(1, d), *tables_bf16).reshape(t, d)


def _rope_tables(L, head_dim):
    rows = L // GRID_W
    row = np.repeat(np.arange(rows), GRID_W).astype(np.float64)
    col = np.tile(np.arange(GRID_W), rows).astype(np.float64)
    half = head_dim // 2
    inv = ROPE_BASE ** (-np.arange(0, half, 2, dtype=np.float64) / half)
    ar = row[:, None] * inv
    ac = col[:, None] * inv
    ang = np.concatenate([ar, ar, ac, ac], axis=-1)
    return (jnp.asarray(np.cos(ang).astype(np.float32)), jnp.asarray(np.sin(ang).astype(np.float32)))


def _rope(x, cos, sin):
    hd = x.shape[-1]
    q = hd // 4
    lane = lax.broadcasted_iota(jnp.int32, x.shape, 1)
    rot = jnp.where((lane % (2 * q)) < q, -pltpu.roll(x, hd - q, 1), pltpu.roll(x, q, 1))
    return x * cos + rot * sin


def _diff_lambda(lam_ref, lam_init):
    lp = lam_ref[...]
    s01 = jnp.sum(lp[0:1, :] * lp[1:2, :], axis=-1, keepdims=True)
    s23 = jnp.sum(lp[2:3, :] * lp[3:4, :], axis=-1, keepdims=True)
    return jnp.exp(s01) - jnp.exp(s23) + lam_init


def _diff_attend_head(q2, k_segs, v_segs, lam, g_sub, out_scale, hd):
    scale = hd ** -0.5
    outs = []
    for comp in range(2):
        qc = (q2[comp] * scale).astype(BF16)
        ss = [lax.dot_general(qc, ks[comp], (((1,), (1,)), ((), ())), preferred_element_type=F32)
              for ks in k_segs]
        m = ss[0].max(axis=-1, keepdims=True)
        for s in ss[1:]:
            m = jnp.maximum(m, s.max(axis=-1, keepdims=True))
        den = None
        pv = None
        for s, v in zip(ss, v_segs):
            e = jnp.exp(s - m)
            dsum = e.sum(axis=-1, keepdims=True)
            den = dsum if den is None else den + dsum
            part = jnp.dot(e.astype(BF16), v, preferred_element_type=F32)
            pv = part if pv is None else pv + part
        outs.append(pv * (1.0 / den))
    o = outs[0] - lam * outs[1]
    return o * _rms(o) * (g_sub * out_scale)


def _attn_ctx_kernel(q_ref, k_ref, v_ref, lam_ref, g_ref, o_ref, *, lam_init, hd):
    lam = _diff_lambda(lam_ref, lam_init)
    vd = 2 * hd
    for h in range(N_HEADS):
        c0 = h * vd
        q2 = [q_ref[:, c0:c0 + hd], q_ref[:, c0 + hd:c0 + vd]]
        k2 = (k_ref[:, c0:c0 + hd].astype(BF16), k_ref[:, c0 + hd:c0 + vd].astype(BF16))
        v = v_ref[:, c0:c0 + vd].astype(BF16)
        o = _diff_attend_head(q2, [k2], [v], lam, g_ref[...], 1.0 - lam_init, hd)
        o_ref[:, c0:c0 + vd] = o.astype(BF16)


def _attn_ctx(q, k, v, at_lam, g_sub, *, seq_len, lam_init):
    t, d = q.shape
    hd = d // N_HEADS // 2
    spec = pl.BlockSpec((seq_len, d), lambda b: (b, 0))
    return pl.pallas_call(
        functools.partial(_attn_ctx_kernel, lam_init=lam_init, hd=hd),
        grid=(t // seq_len,),
        in_specs=[spec, spec, spec,
                  pl.BlockSpec(at_lam.shape, lambda b: (0, 0)),
                  pl.BlockSpec((1, 2 * hd), lambda b: (0, 0))],
        out_specs=spec,
        out_shape=jax.ShapeDtypeStruct((t, d), BF16),
        compiler_params=_cparams(("parallel",)),
        name="attn_ctx",
    )(q, k, v, at_lam, g_sub.reshape(1, 2 * hd))


def _attn_lat_kernel(q_ref, k_ref, v_ref, ck_ref, cv_ref, cq_ref, sq_ref, ckk_ref, skk_ref,
                     lam_ref, g_ref, o_ref, kl_scr, kc_scr, vl_scr, vc_scr, *, lam_init, hd, sub):
    @pl.when(pl.program_id(2) == 0)
    def _():
        ckk, skk = ckk_ref[...], skk_ref[...]
        kl_scr[:, 0:hd] = _rope(k_ref[:, 0:hd], ckk, skk).astype(BF16)
        kl_scr[:, hd:2 * hd] = _rope(k_ref[:, hd:2 * hd], ckk, skk).astype(BF16)
        kc_scr[...] = ck_ref[...].astype(BF16)
        vl_scr[...] = v_ref[...].astype(BF16)
        vc_scr[...] = cv_ref[...].astype(BF16)

    lam = _diff_lambda(lam_ref, lam_init)
    kl = (kl_scr[:, 0:hd], kl_scr[:, hd:2 * hd])
    kc = (kc_scr[:, 0:hd], kc_scr[:, hd:2 * hd])
    for r0 in range(0, q_ref.shape[0], sub):
        rows = slice(r0, r0 + sub)
        cq, sq = cq_ref[rows, :], sq_ref[rows, :]
        q2 = [_rope(q_ref[rows, 0:hd], cq, sq), _rope(q_ref[rows, hd:2 * hd], cq, sq)]
        o = _diff_attend_head(q2, [kc, kl], [vc_scr[...], vl_scr[...]], lam, g_ref[...],
                              1.0 - lam_init, hd)
        o_ref[rows, :] = o.astype(BF16)


def _attn_lat(q, k, v, cache_k, cache_v, at_lam, g_sub, *, seq_len, lam_init):
    t, d = q.shape
    hd = d // N_HEADS // 2
    vd = 2 * hd
    nb = t // seq_len
    past = cache_k.shape[1]
    qb = min(512, seq_len)
    nq = seq_len // qb
    cos, sin = _rope_tables(seq_len, hd)
    return pl.pallas_call(
        functools.partial(_attn_lat_kernel, lam_init=lam_init, hd=hd, sub=min(256, qb)),
        grid=(nb, N_HEADS, nq),
        in_specs=[
            pl.BlockSpec((qb, vd), lambda b, h, i: (b * nq + i, h)),
            pl.BlockSpec((seq_len, vd), lambda b, h, i: (b, h)),
            pl.BlockSpec((seq_len, vd), lambda b, h, i: (b, h)),
            pl.BlockSpec((None, past, vd), lambda b, h, i: (b, 0, h)),
            pl.BlockSpec((None, past, vd), lambda b, h, i: (b, 0, h)),
            pl.BlockSpec((qb, hd), lambda b, h, i: (i, 0)),
            pl.BlockSpec((qb, hd), lambda b, h, i: (i, 0)),
            pl.BlockSpec((seq_len, hd), lambda b, h, i: (0, 0)),
            pl.BlockSpec((seq_len, hd), lambda b, h, i: (0, 0)),
            pl.BlockSpec(at_lam.shape, lambda b, h, i: (0, 0)),
            pl.BlockSpec((1, vd), lambda b, h, i: (0, 0)),
        ],
        out_specs=pl.BlockSpec((qb, vd), lambda b, h, i: (b * nq + i, h)),
        out_shape=jax.ShapeDtypeStruct((t, d), BF16),
        scratch_shapes=[pltpu.VMEM((seq_len, vd), BF16), pltpu.VMEM((past, vd), BF16),
                        pltpu.VMEM((seq_len, vd), BF16), pltpu.VMEM((past, vd), BF16)],
        compiler_params=_cparams(("parallel", "parallel", "arbitrary")),
        name="attn_lat",
    )(q, k, v, cache_k, cache_v, cos, sin, cos, sin, at_lam, g_sub.reshape(1, vd))


def _s5_prep_kernel(lre_r, lim_r, ldt_r, lre_c, lim_c, ldt_c, btre_ref, btim_ref, ctre_ref, ctim_ref,
                    wst_ref, t_ref, wout_ref, lam_ref):
    Q = S5_CHUNK
    H = S5_GROUP
    P2 = lre_r.shape[-1]
    P = P2 // 2
    N = Q * H

    def cexp(n, re_dt, im_dt):
        mag = jnp.exp(n * re_dt)
        return mag * jnp.cos(n * im_dt), mag * jnp.sin(n * im_dt)

    re = jnp.minimum(lre_r[...], -1e-4)
    im = lim_r[...]
    dt = jnp.exp(ldt_r[...])
    re_dt, im_dt = re * dt, im * dt
    lb_re, lb_im = cexp(1.0, re_dt, im_dt)
    den = re * re + im * im
    q_re = ((lb_re - 1.0) * re + lb_im * im) / den
    q_im = (lb_im * re - (lb_re - 1.0) * im) / den
    bt_re, bt_im = btre_ref[...], btim_ref[...]
    bb_re = q_re * bt_re - q_im * bt_im
    bb_im = q_re * bt_im + q_im * bt_re
    fwd_lane = lax.broadcasted_iota(jnp.int32, (Q, P2), 1) < P
    srow = lax.broadcasted_iota(jnp.int32, (Q, P2), 0)
    n = jnp.where(fwd_lane, (Q - 1) - srow, srow).astype(F32)
    pw_re, pw_im = cexp(n, re_dt, im_dt)
    for s in range(Q):
        pr, pi = pw_re[s:s + 1, :], pw_im[s:s + 1, :]
        wst_ref[s * H:(s + 1) * H, 0:P2] = (bb_re * pr - bb_im * pi).astype(BF16)
        wst_ref[s * H:(s + 1) * H, P2:2 * P2] = (bb_re * pi + bb_im * pr).astype(BF16)
    lq_re, lq_im = cexp(float(Q), re_dt, im_dt)
    lam_ref[0:1, :] = lq_re
    lam_ref[1:2, :] = lq_im

    rec = jnp.minimum(lre_c[...], -1e-4)
    imc = lim_c[...]
    dtc = jnp.exp(ldt_c[...])
    rec_dt, imc_dt = rec * dtc, imc * dtc
    l1_re, l1_im = cexp(1.0, rec_dt, imc_dt)
    tlane = lax.broadcasted_iota(jnp.int32, (P2, N), 1) // H
    fwd_row = lax.broadcasted_iota(jnp.int32, (P2, N), 0) < P
    nt = jnp.where(fwd_row, tlane, (Q - 1) - tlane)
    g_re = jnp.ones((P2, N), F32)
    g_im = jnp.zeros((P2, N), F32)
    b_re, b_im = l1_re, l1_im
    bit = 1
    while bit < Q:
        use = (nt & bit) != 0
        f_re = jnp.where(use, b_re, 1.0)
        f_im = jnp.where(use, b_im, 0.0)
        g_re, g_im = g_re * f_re - g_im * f_im, g_re * f_im + g_im * f_re
        b_re, b_im = b_re * b_re - b_im * b_im, 2.0 * b_re * b_im
        bit *= 2
    c_re, c_im = ctre_ref[...], ctim_ref[...]
    gx_re = c_re * g_re - c_im * g_im
    gx_im = c_re * g_im + c_im * g_re
    wout_ref[0:P2, :] = (gx_re * l1_re - gx_im * l1_im).astype(BF16)
    wout_ref[P2:2 * P2, :] = (-(gx_re * l1_im + gx_im * l1_re)).astype(BF16)

    gx = jnp.concatenate([gx_re, gx_im], axis=0)
    is_f = lax.broadcasted_iota(jnp.int32, (H, P2), 1) < P
    zero = jnp.zeros((H, P2), F32)
    lhs_f = jnp.concatenate([jnp.where(is_f, bb_re, zero), jnp.where(is_f, -bb_im, zero)], axis=1)
    lhs_b = jnp.concatenate([jnp.where(is_f, zero, bb_re), jnp.where(is_f, zero, -bb_im)], axis=1)
    m_f = jnp.dot(lhs_f, gx, precision=HIGHEST, preferred_element_type=F32)
    m_b = jnp.dot(lhs_b, gx, precision=HIGHEST, preferred_element_type=F32)
    lane_n = lax.broadcasted_iota(jnp.int32, (H, N), 1)
    for s in range(Q):
        tf = m_f if s == 0 else pltpu.roll(m_f, s * H, 1)
        tb = m_b if s == Q - 1 else pltpu.roll(m_b, (s + 1) * H, 1)
        slab = jnp.where(lane_n >= s * H, tf, 0.0) + jnp.where(lane_n < (s + 1) * H, tb, 0.0)
        t_ref[s * H:(s + 1) * H, :] = slab.astype(BF16)


def _s5_prep(lam_re, lam_im, log_dt, b_re, b_im, c_re, c_im):
    _, G, P = lam_re.shape
    H = S5_GROUP
    N = S5_CHUNK * H
    P2 = 2 * P
    fb_lanes = lambda a: jnp.concatenate([a[0], a[1]], axis=-1)
    ldt = jnp.broadcast_to(log_dt[..., None], (2, G, P))
    rows = [fb_lanes(a).reshape(G, 1, P2) for a in (lam_re, lam_im, ldt)]
    cols = [fb_lanes(a).reshape(G, P2, 1) for a in (lam_re, lam_im, ldt)]
    bt = [fb_lanes(jnp.swapaxes(a, -1, -2)) for a in (b_re, b_im)]
    ct = [jnp.tile(jnp.concatenate([jnp.swapaxes(a[0], -1, -2), jnp.swapaxes(a[1], -1, -2)], axis=1),
                   (1, 1, S5_CHUNK)) for a in (c_re, c_im)]
    ins = rows + cols + bt + ct
    gspec = lambda a: pl.BlockSpec((None,) + a.shape[1:], lambda g: (g, 0, 0))
    mspec = pl.BlockSpec((None, N, N), lambda g: (g, 0, 0))
    msh = jax.ShapeDtypeStruct((G, N, N), BF16)
    return pl.pallas_call(
        _s5_prep_kernel,
        grid=(G,),
        in_specs=[gspec(a) for a in ins],
        out_specs=[mspec, mspec, mspec, pl.BlockSpec((None, 2, P2), lambda g: (g, 0, 0))],
        out_shape=[msh, msh, msh, jax.ShapeDtypeStruct((G, 2, P2), F32)],
        compiler_params=_cparams(("parallel",)),
        name="s5_prep",
    )(*ins)


def _s5_core_kernel(h_ref, wst_ref, t_ref, wout_ref, lam_ref, s0_ref, y_ref, fin_ref,
                    u_scr, loc, sa, sb, *, nb, nc, gl):
    Q, H, GB = S5_CHUNK, S5_GROUP, S5_GROUPS_PER_STEP
    R = nb * nc
    P2 = lam_ref.shape[-1]
    lane_blk = lax.broadcasted_iota(jnp.int32, (R, LANES), 1) // H

    def gather_blocks(pieces, src_blk):
        acc = None
        for b, piece in enumerate(pieces):
            shift = ((b - src_blk) % GB) * H
            r = piece if shift == 0 else pltpu.roll(piece, shift, 1)
            acc = r if acc is None else jnp.where(lane_blk == b, r, acc)
        return acc

    hi_mask = jnp.int32(-65536)

    def pack2(a, b):
        abits = lax.bitcast_convert_type(a.astype(BF16).astype(F32), jnp.int32)
        bbits = lax.bitcast_convert_type(b.astype(BF16).astype(F32), jnp.int32)
        return (abits & hi_mask) | lax.shift_right_logical(bbits, 16)

    def unpack2(p):
        return [lax.bitcast_convert_type(p & hi_mask, F32),
                lax.bitcast_convert_type(lax.shift_left(p, 16), F32)]

    packed = [pack2(h_ref[:, s, :, :].reshape(R, LANES), h_ref[:, s + GB, :, :].reshape(R, LANES))
              for s in range(GB)]
    for g in range(GB):
        u = jnp.concatenate(unpack2(gather_blocks(packed, g)), axis=-1).astype(BF16)
        u_scr[g] = u
        loc[g] = jnp.dot(u, wst_ref[g], preferred_element_type=F32)

    fwd_half = lax.broadcasted_iota(jnp.int32, (nb, P2), 1) < P2 // 2
    for g0 in range(0, GB, gl):
        def body(k, carry, g0=g0):
            rf = pl.multiple_of(k * nb, nb)
            rb = pl.multiple_of((nc - 1 - k) * nb, nb)
            out = []
            for gi in range(gl):
                g = g0 + gi
                xr, xi = carry[2 * gi], carry[2 * gi + 1]
                sa[g, pl.ds(rf, nb), 0:P2] = xr
                sa[g, pl.ds(rf, nb), P2:2 * P2] = xi
                sb[g, pl.ds(rb, nb), 0:P2] = xr
                sb[g, pl.ds(rb, nb), P2:2 * P2] = xi
                lr = jnp.where(fwd_half, loc[g, pl.ds(rf, nb), 0:P2], loc[g, pl.ds(rb, nb), 0:P2])
                li = jnp.where(fwd_half, loc[g, pl.ds(rf, nb), P2:2 * P2],
                               loc[g, pl.ds(rb, nb), P2:2 * P2])
                ar, ai = lam_ref[g, 0:1, :], lam_ref[g, 1:2, :]
                out += [ar * xr - ai * xi + lr, ar * xi + ai * xr + li]
            return tuple(out)

        init = []
        for gi in range(gl):
            init += [s0_ref[g0 + gi, :, 0:P2], s0_ref[g0 + gi, :, P2:2 * P2]]
        fin = lax.fori_loop(0, nc, body, tuple(init))
        for gi in range(gl):
            fin_ref[g0 + gi, :, 0:P2] = fin[2 * gi]
            fin_ref[g0 + gi, :, P2:2 * P2] = fin[2 * gi + 1]

    fsel = (lax.broadcasted_iota(jnp.int32, (R, 2 * P2), 1) % P2) < P2 // 2
    for g in range(GB):
        s_in = jnp.where(fsel, sa[g], sb[g]).astype(BF16)
        loc[g] = (jnp.dot(u_scr[g], t_ref[g], preferred_element_type=F32)
                  + jnp.dot(s_in, wout_ref[g], preferred_element_type=F32))
    ys = [pack2(loc[g, :, 0:LANES], loc[g, :, LANES:2 * LANES]) for g in range(GB)]
    for tl in range(GB):
        lo, hi = unpack2(gather_blocks(ys, tl))
        y_ref[:, tl, :, :] = lo.reshape(nc, nb, LANES)
        y_ref[:, tl + GB, :, :] = hi.reshape(nc, nb, LANES)


def _s5_core(h_tm, s0, wst, tmat, wout, lam, *, nb, seq_len):
    d = h_tm.shape[1] // nb
    H, Q, GB = S5_GROUP, S5_CHUNK, S5_GROUPS_PER_STEP
    G = d // H
    nc = seq_len // Q
    R = nb * nc
    N = Q * H
    P4 = s0.shape[-1]
    gl = max(1, min(GB, (8 * GB) // nb))
    h4 = h_tm.reshape(nc, Q, nb, d)
    hspec = pl.BlockSpec((nc, Q, nb, LANES), lambda gb: (0, 0, 0, gb))
    mspec = pl.BlockSpec((GB, N, N), lambda gb: (gb, 0, 0))
    sspec = pl.BlockSpec((GB, nb, P4), lambda gb: (gb, 0, 0))
    y, fin = pl.pallas_call(
        functools.partial(_s5_core_kernel, nb=nb, nc=nc, gl=gl),
        grid=(G // GB,),
        in_specs=[hspec, mspec, mspec, mspec,
                  pl.BlockSpec((GB, 2, P4 // 2), lambda gb: (gb, 0, 0)), sspec],
        out_specs=[hspec, sspec],
        out_shape=[jax.ShapeDtypeStruct((nc, Q, nb, d), F32), jax.ShapeDtypeStruct((G, nb, P4), F32)],
        scratch_shapes=[pltpu.VMEM((GB, R, N), BF16), pltpu.VMEM((GB, R, N), F32),
                        pltpu.VMEM((GB, R, P4), F32), pltpu.VMEM((GB, R, P4), F32)],
        compiler_params=_cparams(("parallel",)),
        name="s5_core",
    )(h4, wst, tmat, wout, lam, s0)
    return y.reshape(seq_len, nb * d), fin


def _glu_postadd_kernel(*refs, nsub):
    h_refs, y_refs = refs[:nsub], refs[nsub:2 * nsub]
    (d_ref, wa_ref, wg_ref, ba_ref, bg_ref, x_ref, gate_ref, g_ref, o_ref, acc_a, acc_g, u_scr) = refs[2 * nsub:]
    k = pl.program_id(1)

    @pl.when(k == 0)
    def _():
        acc_a[...] = jnp.zeros_like(acc_a)
        acc_g[...] = jnp.zeros_like(acc_g)

    rows = h_refs[0].shape[0]
    for s in range(nsub):
        u_scr[s * rows:(s + 1) * rows, :] = jax.nn.gelu(
            d_ref[...] * h_refs[s][...] + y_refs[s][...]).astype(BF16)
    u = u_scr[...]
    acc_a[...] += jnp.dot(u, wa_ref[...], preferred_element_type=F32)
    acc_g[...] += jnp.dot(u, wg_ref[...], preferred_element_type=F32)

    @pl.when(k == pl.num_programs(1) - 1)
    def _():
        ba, bg = ba_ref[...], bg_ref[...]
        _post_add_rows(x_ref, lambda rows: (acc_a[rows, :] + ba) * jax.nn.sigmoid(acc_g[rows, :] + bg),
                       g_ref, gate_ref, o_ref)


def _glu_postadd(h_tm, y_tm, dskip, w_glu, layer, b_glu, x, m_l, g1, *, seq_len, latent):
    t, d = x.shape
    tm = min(512, t)
    nsub = max(1, tm // seq_len)
    per_seq = max(1, seq_len // tm)
    sub_rows = tm // nsub
    tk = 1024
    nk = d // tk
    rowfn = _row_fn(tm, seq_len, latent)
    b_glu = b_glu.reshape(1, 2 * d)
    tspecs = [pl.BlockSpec((sub_rows, tk), lambda i, k, s=s: (i % per_seq, ((i // per_seq) * nsub + s) * nk + k))
              for s in range(nsub)]
    return pl.pallas_call(
        functools.partial(_glu_postadd_kernel, nsub=nsub),
        grid=(t // tm, nk),
        in_specs=tspecs + tspecs + [
            pl.BlockSpec((1, tk), lambda i, k: (0, k)),
            pl.BlockSpec((None, None, tk, d), lambda i, k: (layer, 0, k, 0)),
            pl.BlockSpec((None, None, tk, d), lambda i, k: (layer, 1, k, 0)),
            pl.BlockSpec((1, d), lambda i, k: (0, 0)),
            pl.BlockSpec((1, d), lambda i, k: (0, 1)),
            pl.BlockSpec((tm, d), lambda i, k: (i, 0)),
            _mod_spec(d, 2, rowfn),
            pl.BlockSpec((1, d), lambda i, k: (0, 0)),
        ],
        out_specs=pl.BlockSpec((tm, d), lambda i, k: (i, 0)),
        out_shape=jax.ShapeDtypeStruct((t, d), F32),
        scratch_shapes=[pltpu.VMEM((tm, d), F32), pltpu.VMEM((tm, d), F32), pltpu.VMEM((tm, tk), BF16)],
        compiler_params=_cparams(("parallel", "arbitrary")),
        name="glu_postadd",
    )(*([h_tm] * nsub), *([y_tm] * nsub), dskip.reshape(1, d), w_glu, w_glu, b_glu, b_glu, x, m_l,
      g1.reshape(1, d))


def kernel(x_prompt, x_sample, cache_attn_k, cache_attn_v, state_s5_re, state_s5_im, c, c_ctx, w_mod, b_mod, g_norm, w_mlp_in, w_mlp_out, hy_w_in, hy_b_in, hy_w_short, hy_b_short, hy_f_w1, hy_f_b1, hy_f_freq1, hy_f_w2, hy_f_b2, hy_f_freq2, hy_f_w3, hy_log_alpha, hy_skip, hy_w_out, hy_b_out, at_w_qkv, at_lam, at_g_sub, at_w_o, s5_lam_re, s5_lam_im, s5_log_dt, s5_b_re, s5_b_im, s5_c_re, s5_c_im, s5_d, s5_w_glu, s5_b_glu):
    bc, lc, d = x_prompt.shape
    bl, ll, _ = x_sample.shape
    depth = w_mod.shape[0]
    assert 1 + bl <= MOD_ROWS
    assert cache_attn_k.shape[1] == 1 and state_s5_re.shape[1] == 1, "one attention and one S5 layer"
    hd = d // N_HEADS // 2
    G = d // S5_GROUP
    P = s5_lam_re.shape[-1]

    cond = jnp.concatenate([c_ctx[None], c, jnp.zeros((MOD_ROWS - 1 - bl, d), F32)], axis=0)
    mod = _modulation(cond, w_mod, b_mod).reshape(depth, MOD_ROWS, N_MOD, 1, d)

    streams = [dict(seq_len=lc, latent=False), dict(seq_len=ll, latent=True)]
    xs = [x_prompt.reshape(bc * lc, d), x_sample.reshape(bl * ll, d)]

    tables = {}
    for L in {lc, ll}:
        cm, sm, sp, ci, sip = _dft_tables(L)
        tables[L] = (_split_bf16(cm) + _split_bf16(sm), tuple(a.astype(BF16) for a in (cm, sp, ci, sip)))

    w_mlp_in_b = _cast_tiles(w_mlp_in, MLP_TF)
    w_mlp_out_b = _cast_tiles(w_mlp_out.reshape(-1, MLP_TF, d), d).reshape(w_mlp_out.shape)
    hy_w_in_b = _cast_tiles(hy_w_in, PROJ_TN)
    at_w_qkv_b = _cast_tiles(at_w_qkv, PROJ_TN)
    hy_w_out_b = _cast_tiles(hy_w_out, d).reshape(hy_w_out.shape)
    at_w_o_b = _cast_tiles(at_w_o, d).reshape(at_w_o.shape)
    s5_w_glu_b = _cast_tiles(s5_w_glu, d)

    new_k = new_v = None
    fin_ctx = None
    for i in range(depth):
        kind, j = i % 3, i // 3
        m_l = mod[i]
        g = g_norm[i]
        if kind == 0:
            filt = {}
            for L in {lc, ll}:
                filt[L] = _hy_filter(L, hy_f_w1[j], hy_f_b1[j], hy_f_freq1[j], hy_f_w2[j], hy_f_b2[j],
                                     hy_f_freq2[j], hy_f_w3[j], hy_log_alpha[j], tables[L][0])
            for si, st in enumerate(streams):
                L = st["seq_len"]
                x0, vg = _hy_inproj(xs[si], m_l, g[0], hy_w_in_b, j, hy_b_in[j], hy_w_short[j],
                                    hy_b_short[j], **st)
                a = _hy_conv(x0, vg, filt[L][0], filt[L][1], hy_skip[j], tables[L][1], seq_len=L)
                xs[si] = _mm_postadd(a, hy_w_out_b, j, hy_b_out[j], xs[si], m_l, g[1], **st)
        elif kind == 1:
            lam_init = 0.8 - 0.6 * math.exp(-0.3 * i)
            for si, st in enumerate(streams):
                L = st["seq_len"]
                q, k, v = _premod_mm3(xs[si], m_l, g[0], at_w_qkv_b, j, **st)
                if not st["latent"]:
                    new_k = k.reshape(bc, 1, lc, N_HEADS, 2, hd)
                    new_v = v.reshape(bc, 1, lc, N_HEADS, 2 * hd)
                    a = _attn_ctx(q, k, v, at_lam[j], at_g_sub[j], seq_len=L, lam_init=lam_init)
                else:
                    ck = cache_attn_k[:, j].reshape(bl, -1, d)
                    cv = cache_attn_v[:, j].reshape(bl, -1, d)
                    a = _attn_lat(q, k, v, ck, cv, at_lam[j], at_g_sub[j], seq_len=L, lam_init=lam_init)
                xs[si] = _mm_postadd(a, at_w_o_b, j, jnp.zeros((d,), F32), xs[si], m_l, g[1], **st)
        else:
            wst, tmat, wout, lam_q = _s5_prep(s5_lam_re[j], s5_lam_im[j], s5_log_dt[j], s5_b_re[j],
                                              s5_b_im[j], s5_c_re[j], s5_c_im[j])
            for si, st in enumerate(streams):
                L = st["seq_len"]
                nb = xs[si].shape[0] // L
                if st["latent"]:
                    sre, sim = state_s5_re[:, j], state_s5_im[:, j]
                    s0 = jnp.concatenate([sre[:, 0], sre[:, 1], sim[:, 0], sim[:, 1]], axis=-1)
                    s0 = s0.transpose(1, 0, 2)
                else:
                    s0 = jnp.zeros((G, nb, 4 * P), F32)
                h_tm = _premod_time_major(xs[si], m_l, g[0], **st)
                y_tm, fin = _s5_core(h_tm, s0, wst, tmat, wout, lam_q, nb=nb, seq_len=L)
                if not st["latent"]:
                    fin_ctx = fin.reshape(G, nb, 2, 2, P).transpose(1, 2, 3, 0, 4)
                xs[si] = _glu_postadd(h_tm, y_tm, s5_d[j], s5_w_glu_b, j, s5_b_glu[j], xs[si], m_l, g[1], **st)
        for si, st in enumerate(streams):
            xs[si] = _mlp(xs[si], m_l, g[2], g[3], w_mlp_in_b, w_mlp_out_b, i, **st)

    new_s_re = fin_ctx[:, 0][:, None]
    new_s_im = fin_ctx[:, 1][:, None]
    return (xs[0].reshape(bc, lc, d), xs[1].reshape(bl, ll, d), new_k, new_v, new_s_re, new_s_im)
```

```python
import functools
import math

import numpy as np
import jax
import jax.numpy as jnp
from jax import lax
from jax.experimental import pallas as pl
from jax.experimental.pallas import tpu as pltpu

F32 = jnp.float32
BF16 = jnp.bfloat16
HIGHEST = lax.Precision.HIGHEST

NORM_EPS = 1e-6
N_MOD = 6
N_HEADS = 8
GRID_W = 64
ROPE_BASE = 10000.0
HY_PE_BANDS = 16
HY_PE_MIN_PERIOD = 2.0
HY_PE_MAX_PERIOD = 4096.0
S5_GROUP = 16
S5_CHUNK = 16
LANES = 128
S5_GROUPS_PER_STEP = LANES // S5_GROUP
MOD_ROWS = 16
MLP_TF = 1024
PROJ_TN = 512

VMEM_LIMIT = 56 * 1024 * 1024


def _cparams(sem):
    return pltpu.CompilerParams(dimension_semantics=sem, vmem_limit_bytes=VMEM_LIMIT)


def _rms(x):
    return lax.rsqrt(jnp.mean(x * x, axis=-1, keepdims=True) + NORM_EPS)


def _split_bf16(a):
    hi = a.astype(BF16)
    return hi, (a - hi.astype(F32)).astype(BF16)


def _cast_kernel(w_ref, o_ref):
    o_ref[...] = w_ref[...].astype(BF16)


def _cast_tiles(w, tc):
    n, k, m = w.shape
    rk = min(k, (2 * 1024 * 1024) // tc)
    return pl.pallas_call(
        _cast_kernel,
        grid=(n, m // tc, k // rk),
        in_specs=[pl.BlockSpec((None, rk, tc), lambda l, j, r: (l, r, j))],
        out_specs=pl.BlockSpec((None, None, rk, tc), lambda l, j, r: (l, j, r, 0)),
        out_shape=jax.ShapeDtypeStruct((n, m // tc, k, tc), BF16),
        compiler_params=_cparams(("parallel", "parallel", "parallel")),
        name="cast_bf16",
    )(w)


def _mod_kernel(c_ref, w_ref, b_ref, o_ref):
    c = c_ref[...]
    s_hi, s_lo = _split_bf16(c * jax.nn.sigmoid(c))
    both = jnp.dot(jnp.concatenate([s_hi, s_lo], axis=0), w_ref[...].astype(BF16),
                   preferred_element_type=F32)
    o_ref[...] = both[:MOD_ROWS] + both[MOD_ROWS:] + b_ref[...]


def _modulation(cond, w_mod, b_mod):
    depth, d, n = w_mod.shape
    tn = 1024
    return pl.pallas_call(
        _mod_kernel,
        grid=(depth, n // tn),
        in_specs=[
            pl.BlockSpec((MOD_ROWS, d), lambda l, j: (0, 0)),
            pl.BlockSpec((None, d, tn), lambda l, j: (l, 0, j)),
            pl.BlockSpec((None, 1, tn), lambda l, j: (l, 0, j)),
        ],
        out_specs=pl.BlockSpec((None, MOD_ROWS, tn), lambda l, j: (l, 0, j)),
        out_shape=jax.ShapeDtypeStruct((depth, MOD_ROWS, n), F32),
        compiler_params=_cparams(("parallel", "parallel")),
        name="adaln_mod",
    )(cond, w_mod, b_mod.reshape(depth, 1, n))


def _mod_spec(d, which, rowfn):
    return pl.BlockSpec((None, None, 1, d), lambda i, *_: (rowfn(i), which, 0, 0))


def _lookahead_rows_spec(tm, d, nt):
    return pl.BlockSpec((tm, d), lambda i, j: (jnp.where(j >= 1, jnp.minimum(i + 1, nt - 1), i), 0))


def _row_fn(tm, seq_len, latent):
    if latent:
        assert seq_len % tm == 0, "a latent row tile must sit inside one sequence"
        return lambda i: 1 + (i * tm) // seq_len
    return lambda i: 0


ROW_CHUNK = 16


def _for_row_chunks(n_rows, fn):
    def body(c, carry):
        fn(pl.ds(pl.multiple_of(c * ROW_CHUNK, ROW_CHUNK), ROW_CHUNK))
        return carry

    lax.fori_loop(0, n_rows // ROW_CHUNK, body, 0, unroll=16)


def _premod_rows(x_ref, g_ref, sh_ref, sc_ref, out_ref):
    gs, sh = g_ref[...] * (1.0 + sc_ref[...]), sh_ref[...]

    def rows_fn(rows):
        x = x_ref[rows, :]
        out_ref[rows, :] = ((x * _rms(x)) * gs + sh).astype(out_ref.dtype)

    _for_row_chunks(x_ref.shape[0], rows_fn)


def _post_add_rows(x_ref, o_fn, g_ref, gate_ref, out_ref):
    gg = gate_ref[...] * g_ref[...]

    def rows_fn(rows):
        o = o_fn(rows)
        out_ref[rows, :] = x_ref[rows, :] + (o * _rms(o)) * gg

    _for_row_chunks(x_ref.shape[0], rows_fn)


def _premod_mm3_kernel(x_ref, g_ref, sh_ref, sc_ref, wa_ref, wb_ref, wc_ref, oa_ref, ob_ref, oc_ref, h_scr):
    @pl.when(pl.program_id(1) == 0)
    def _():
        _premod_rows(x_ref, g_ref, sh_ref, sc_ref, h_scr)

    h = h_scr[...]
    for w_ref, o_ref in ((wa_ref, oa_ref), (wb_ref, ob_ref), (wc_ref, oc_ref)):
        o_ref[...] = jnp.dot(h, w_ref[...], preferred_element_type=F32)


def _premod_mm3(x, m_l, g, w, layer, *, seq_len, latent):
    t, d = x.shape
    tm = min(1024, seq_len if latent else t)
    tn = PROJ_TN
    nj = d // tn
    rowfn = _row_fn(tm, seq_len, latent)
    wspec = lambda o: pl.BlockSpec((None, None, d, tn), lambda i, j: (layer, o * nj + j, 0, 0))
    ospec = pl.BlockSpec((tm, tn), lambda i, j: (i, j))
    osh = jax.ShapeDtypeStruct((t, d), F32)
    return pl.pallas_call(
        _premod_mm3_kernel,
        grid=(t // tm, nj),
        in_specs=[
            _lookahead_rows_spec(tm, d, t // tm),
            pl.BlockSpec((1, d), lambda i, j: (0, 0)),
            _mod_spec(d, 0, rowfn),
            _mod_spec(d, 1, rowfn),
            wspec(0), wspec(1), wspec(2),
        ],
        out_specs=[ospec, ospec, ospec],
        out_shape=[osh, osh, osh],
        scratch_shapes=[pltpu.VMEM((tm, d), BF16)],
        compiler_params=_cparams(("parallel", "arbitrary")),
        name="premod_mm3",
    )(x, g.reshape(1, d), m_l, m_l, w, w, w)


def _premod_kernel(x_ref, g_ref, sh_ref, sc_ref, o_ref):
    _premod_rows(x_ref, g_ref, sh_ref, sc_ref, o_ref)


def _premod_time_major(x, m_l, g, *, seq_len, latent):
    t, d = x.shape
    nb = t // seq_len
    tm = min(512, seq_len)
    per_seq = seq_len // tm
    rowfn = _row_fn(tm, seq_len, latent)
    return pl.pallas_call(
        _premod_kernel,
        grid=(t // tm,),
        in_specs=[
            pl.BlockSpec((tm, d), lambda i: (i, 0)),
            pl.BlockSpec((1, d), lambda i: (0, 0)),
            _mod_spec(d, 0, rowfn),
            _mod_spec(d, 1, rowfn),
        ],
        out_specs=pl.BlockSpec((tm, d), lambda i: (i % per_seq, i // per_seq)),
        out_shape=jax.ShapeDtypeStruct((seq_len, nb * d), F32),
        compiler_params=_cparams(("parallel",)),
        name="premod",
    )(x, g.reshape(1, d), m_l, m_l)


def _mlp_kernel(x_ref, g2_ref, sh_ref, sc_ref, gate_ref, g3_ref, w1_ref, w2_ref, o_ref, h_scr, acc):
    f = pl.program_id(1)

    @pl.when(f == 0)
    def _():
        _premod_rows(x_ref, g2_ref, sh_ref, sc_ref, h_scr)
        acc[...] = jnp.zeros_like(acc)

    a = jnp.dot(h_scr[...], w1_ref[...], preferred_element_type=F32)
    a = jnp.square(jnp.maximum(a, 0.0)).astype(BF16)
    acc[...] += jnp.dot(a, w2_ref[...], preferred_element_type=F32)

    @pl.when(f == pl.num_programs(1) - 1)
    def _():
        _post_add_rows(x_ref, lambda rows: acc[rows, :], g3_ref, gate_ref, o_ref)


def _mlp(x, m_l, g2, g3, w1, w2, layer, *, seq_len, latent):
    t, d = x.shape
    tf = MLP_TF
    dff = w1.shape[1] * tf
    tm = min(512, t)
    rowfn = _row_fn(tm, seq_len, latent)
    return pl.pallas_call(
        _mlp_kernel,
        grid=(t // tm, dff // tf),
        in_specs=[
            pl.BlockSpec((tm, d), lambda i, f: (i, 0)),
            pl.BlockSpec((1, d), lambda i, f: (0, 0)),
            _mod_spec(d, 3, rowfn),
            _mod_spec(d, 4, rowfn),
            _mod_spec(d, 5, rowfn),
            pl.BlockSpec((1, d), lambda i, f: (0, 0)),
            pl.BlockSpec((None, None, d, tf), lambda i, f: (layer, f, 0, 0)),
            pl.BlockSpec((None, tf, d), lambda i, f: (layer, f, 0)),
        ],
        out_specs=pl.BlockSpec((tm, d), lambda i, f: (i, 0)),
        out_shape=jax.ShapeDtypeStruct((t, d), F32),
        scratch_shapes=[pltpu.VMEM((tm, d), BF16), pltpu.VMEM((tm, d), F32)],
        compiler_params=_cparams(("parallel", "arbitrary")),
        name="mlp",
    )(x, g2.reshape(1, d), m_l, m_l, m_l, g3.reshape(1, d), w1, w2)


def _mm_postadd_kernel(a_ref, w_ref, b_ref, x_ref, gate_ref, g_ref, o_ref, acc):
    acc[...] = jnp.dot(a_ref[...], w_ref[...], preferred_element_type=F32)
    b = b_ref[...]
    _post_add_rows(x_ref, lambda rows: acc[rows, :] + b, g_ref, gate_ref, o_ref)


def _mm_postadd(a, w, layer, b, x, m_l, g1, *, seq_len, latent):
    t, d = x.shape
    tm = min(512, t)
    rowfn = _row_fn(tm, seq_len, latent)
    return pl.pallas_call(
        _mm_postadd_kernel,
        grid=(t // tm,),
        in_specs=[
            pl.BlockSpec((tm, d), lambda i: (i, 0)),
            pl.BlockSpec((None, d, d), lambda i: (layer, 0, 0)),
            pl.BlockSpec((1, d), lambda i: (0, 0)),
            pl.BlockSpec((tm, d), lambda i: (i, 0)),
            _mod_spec(d, 2, rowfn),
            pl.BlockSpec((1, d), lambda i: (0, 0)),
        ],
        out_specs=pl.BlockSpec((tm, d), lambda i: (i, 0)),
        out_shape=jax.ShapeDtypeStruct((t, d), F32),
        scratch_shapes=[pltpu.VMEM((tm, d), F32)],
        compiler_params=_cparams(("parallel",)),
        name="mm_postadd",
    )(a, w, b.reshape(1, d), x, m_l, g1.reshape(1, d))


def _dft_tables(L):
    idx = np.arange(L, dtype=np.int64)
    ang = np.pi * ((idx[:, None] * idx[None, :]) % (2 * L)).astype(np.float64) / L
    c = np.cos(ang)
    s = np.sin(ang)
    alt = np.where(idx % 2 == 0, 1.0, -1.0)
    sp = s.copy()
    sp[0, :] = alt
    wf = np.full((L,), 2.0)
    wf[0] = 1.0
    ci = c * wf[None, :] / (2 * L)
    sip = -s * 2.0 / (2 * L)
    sip[:, 0] = alt / (2 * L)
    f32 = lambda a: jnp.asarray(a.astype(np.float32))
    return f32(c), f32(s), f32(sp), f32(ci), f32(sip)


def _hy_pe(L):
    t = np.arange(L, dtype=np.float64)
    periods = HY_PE_MIN_PERIOD * (HY_PE_MAX_PERIOD / HY_PE_MIN_PERIOD) ** (
        np.arange(HY_PE_BANDS, dtype=np.float64) / (HY_PE_BANDS - 1))
    ang = t[:, None] * (2.0 * math.pi / periods)[None]
    return jnp.asarray(np.concatenate([np.sin(ang), np.cos(ang)], axis=-1).astype(np.float32))


def _dot_bf16x3(a_hi, a_lo, b):
    b_hi, b_lo = _split_bf16(b)
    n = b.shape[1]
    both = jnp.dot(a_hi, jnp.concatenate([b_hi, b_lo], axis=1), preferred_element_type=F32)
    return both[:, :n] + both[:, n:] + jnp.dot(a_lo, b_hi, preferred_element_type=F32)


def _hy_filter_kernel(pe_ref, w1_ref, b1_ref, fr1_ref, w2_ref, b2_ref, fr2_ref, w3f_ref, w3b_ref,
                      la_ref, chi_ref, clo_ref, shi_ref, slo_ref, kre_ref, kim_ref, h_scr):
    L, dc = kre_ref.shape

    @pl.when(pl.program_id(0) == 0)
    def _():
        h1 = jnp.sin(fr1_ref[...] * (jnp.dot(pe_ref[...], w1_ref[...], precision=HIGHEST,
                                             preferred_element_type=F32) + b1_ref[...]))
        h_scr[...] = jnp.sin(fr2_ref[...] * (jnp.dot(h1, w2_ref[...], precision=HIGHEST,
                                                     preferred_element_type=F32) + b2_ref[...]))

    h = h_scr[...]
    row = lax.broadcasted_iota(jnp.int32, (L, dc), 0)
    dec = jnp.exp(-jnp.exp(la_ref[...]) * row.astype(F32))
    kf = jnp.dot(h, w3f_ref[...], precision=HIGHEST, preferred_element_type=F32) * dec
    kb = jnp.dot(h, w3b_ref[...], precision=HIGHEST, preferred_element_type=F32) * dec
    kb = jnp.where(row == 0, 0.0, kb)
    norm = jnp.sum(jnp.abs(kf) + jnp.abs(kb), axis=0, keepdims=True) + 1e-6
    inv = 1.0 / norm
    ks = (kf + kb) * inv
    kd = (kb - kf) * inv
    kre = _dot_bf16x3(chi_ref[...], clo_ref[...], ks)
    kim = _dot_bf16x3(shi_ref[...], slo_ref[...], kd)
    alt = jnp.where(row % 2 == 0, 1.0, -1.0)
    nyq = jnp.sum(alt * ks, axis=0, keepdims=True)
    kre_ref[...] = kre
    kim_ref[...] = jnp.where(row == 0, nyq, kim)


def _hy_filter(L, w1, b1, fr1, w2, b2, fr2, w3, log_alpha, cs_split):
    d = log_alpha.shape[-1]
    fw = w1.shape[1]
    dc = 512
    nd = d // dc
    full = lambda a: pl.BlockSpec(a.shape, lambda j: (0,) * a.ndim)
    pe = _hy_pe(L)
    b1, fr1, b2, fr2 = (a.reshape(1, fw) for a in (b1, fr1, b2, fr2))
    osh = jax.ShapeDtypeStruct((L, d), F32)
    return pl.pallas_call(
        _hy_filter_kernel,
        grid=(nd,),
        in_specs=[full(pe), full(w1), full(b1), full(fr1), full(w2), full(b2), full(fr2),
                  pl.BlockSpec((fw, dc), lambda j: (0, j)),
                  pl.BlockSpec((fw, dc), lambda j: (0, nd + j)),
                  pl.BlockSpec((1, dc), lambda j: (0, j))] + [full(a) for a in cs_split],
        out_specs=[pl.BlockSpec((L, dc), lambda j: (0, j))] * 2,
        out_shape=[osh, osh],
        scratch_shapes=[pltpu.VMEM((L, fw), F32)],
        compiler_params=_cparams(("arbitrary",)),
        name="hyena_filter",
    )(pe, w1, b1, fr1, w2, b2, fr2, w3, w3, log_alpha.reshape(1, d), *cs_split)


def _hy_inproj_kernel(x_ref, g_ref, sh_ref, sc_ref, w0_ref, w1_ref, wv_ref, b0_ref, b1_ref, bv_ref,
                      s0_ref, s1_ref, sv_ref, c0_ref, c1_ref, cv_ref, ox_ref, ov_ref, h_scr, *, seq_len):
    @pl.when(pl.program_id(1) == 0)
    def _():
        _premod_rows(x_ref, g_ref, sh_ref, sc_ref, h_scr)

    h = h_scr[...]
    tm, tn = ox_ref.shape
    pos = lax.broadcasted_iota(jnp.int32, (tm, tn), 0) % seq_len
    first, last = pos == 0, pos == seq_len - 1

    def section(w_ref, b_ref, s_ref, c_ref):
        z = jnp.dot(h, w_ref[...], preferred_element_type=F32) + b_ref[...]
        zm = jnp.where(first, 0.0, pltpu.roll(z, 1, 0))
        zp = jnp.where(last, 0.0, pltpu.roll(z, tm - 1, 0))
        return zm * s_ref[0:1, :] + z * s_ref[1:2, :] + zp * s_ref[2:3, :] + c_ref[...]

    ox_ref[...] = section(w0_ref, b0_ref, s0_ref, c0_ref).astype(ox_ref.dtype)
    x1 = section(w1_ref, b1_ref, s1_ref, c1_ref)
    ov_ref[...] = (section(wv_ref, bv_ref, sv_ref, cv_ref) * x1).astype(ov_ref.dtype)


def _hy_inproj(x, m_l, g, w, layer, b, w_sh, b_sh, *, seq_len, latent):
    t, d = x.shape
    tm = min(1024, seq_len if latent else t)
    assert tm % seq_len == 0, "row tiles hold whole sequences, so the short conv needs no halo"
    tn = PROJ_TN
    nj = d // tn
    rowfn = _row_fn(tm, seq_len, latent)
    wspec = lambda o: pl.BlockSpec((None, None, d, tn), lambda i, j: (layer, o * nj + j, 0, 0))
    vspec = lambda rows: (lambda o: pl.BlockSpec((rows, tn), lambda i, j: (0, o * nj + j)))
    bspec, sspec = vspec(1), vspec(3)
    ospec = pl.BlockSpec((tm, tn), lambda i, j: (i, j))
    osh = jax.ShapeDtypeStruct((t, d), BF16)
    b = b.reshape(1, 3 * d)
    b_sh = b_sh.reshape(1, 3 * d)
    return pl.pallas_call(
        functools.partial(_hy_inproj_kernel, seq_len=seq_len),
        grid=(t // tm, nj),
        in_specs=[
            _lookahead_rows_spec(tm, d, t // tm),
            pl.BlockSpec((1, d), lambda i, j: (0, 0)),
            _mod_spec(d, 0, rowfn),
            _mod_spec(d, 1, rowfn),
            wspec(0), wspec(1), wspec(2), bspec(0), bspec(1), bspec(2),
            sspec(0), sspec(1), sspec(2), bspec(0), bspec(1), bspec(2),
        ],
        out_specs=[ospec, ospec],
        out_shape=[osh, osh],
        scratch_shapes=[pltpu.VMEM((tm, d), BF16)],
        compiler_params=_cparams(("parallel", "arbitrary")),
        name="hyena_inproj",
    )(x, g.reshape(1, d), m_l, m_l, w, w, w, b, b, b, w_sh, w_sh, w_sh, b_sh, b_sh, b_sh)


def _hy_conv_kernel(x0_ref, v_ref, kre_ref, kim_ref, skip_ref, c_ref, sp_ref, ci_ref, sip_ref, o_ref):
    sb, L, dc = o_ref.shape
    kre = kre_ref[...]
    kim = kim_ref[...]
    first = lax.broadcasted_iota(jnp.int32, (L, dc), 0) == 0
    for s in range(sb):
        vb = v_ref[s]
        a = jnp.dot(c_ref[...], vb, preferred_element_type=F32)
        bm = jnp.dot(sp_ref[...], vb, preferred_element_type=F32)
        bk = bm * kim
        yre = a * kre + jnp.where(first, 0.0, bk)
        yim = jnp.where(first, bk, a * kim - bm * kre)
        y = (jnp.dot(ci_ref[...], yre.astype(BF16), preferred_element_type=F32)
             + jnp.dot(sip_ref[...], yim.astype(BF16), preferred_element_type=F32))
        o_ref[s] = (x0_ref[s].astype(F32) * (y + vb.astype(F32) * skip_ref[...])).astype(BF16)


def _hy_conv(x0, v, kre, kim, skip, tables_bf16, *, seq_len):
    t, d = x0.shape
    L = seq_len
    nb = t // L
    sb = max(1, min(nb, 1024 // L))
    dc = 512
    nd = d // dc
    once = pl.Buffered(1)
    zspec = pl.BlockSpec((sb, L, dc), lambda j, b: (b, 0, j))
    kspec = pl.BlockSpec((L, dc), lambda j, b: (0, j), pipeline_mode=once)
    mspec = pl.BlockSpec((L, L), lambda j, b: (0, 0), pipeline_mode=once)
    return pl.pallas_call(
        _hy_conv_kernel,
        grid=(nd, nb // sb),
        in_specs=[zspec, zspec, kspec, kspec, pl.BlockSpec((1, dc), lambda j, b: (0, j)),
                  mspec, mspec, mspec, mspec],
        out_specs=zspec,
        out_shape=jax.ShapeDtypeStruct((nb, L, d), BF16),
        compiler_params=_cparams(("parallel", "arbitrary")),
        name="hyena_conv",
    )(x0.reshape(nb, L, d), v.reshape(nb, L, d), kre, kim, skip.reshape(1, d), *tables_bf16).reshape(t, d)


def _rope_tables(L, head_dim):
    rows = L // GRID_W
    row = np.repeat(np.arange(rows), GRID_W).astype(np.float64)
    col = np.tile(np.arange(GRID_W), rows).astype(np.float64)
    half = head_dim // 2
    inv = ROPE_BASE ** (-np.arange(0, half, 2, dtype=np.float64) / half)
    ar = row[:, None] * inv
    ac = col[:, None] * inv
    ang = np.concatenate([ar, ar, ac, ac], axis=-1)
    return (jnp.asarray(np.cos(ang).astype(np.float32)), jnp.asarray(np.sin(ang).astype(np.float32)))


def _rope(x, cos, sin):
    hd = x.shape[-1]
    q = hd // 4
    lane = lax.broadcasted_iota(jnp.int32, x.shape, 1)
    rot = jnp.where((lane % (2 * q)) < q, -pltpu.roll(x, hd - q, 1), pltpu.roll(x, q, 1))
    return x * cos + rot * sin


def _diff_lambda(lam_ref, lam_init):
    lp = lam_ref[...]
    s01 = jnp.sum(lp[0:1, :] * lp[1:2, :], axis=-1, keepdims=True)
    s23 = jnp.sum(lp[2:3, :] * lp[3:4, :], axis=-1, keepdims=True)
    return jnp.exp(s01) - jnp.exp(s23) + lam_init


def _diff_attend_head(q2, k_segs, v_segs, lam, g_sub, out_scale, hd):
    scale = hd ** -0.5
    outs = []
    for comp in range(2):
        qc = (q2[comp] * scale).astype(BF16)
        ss = [lax.dot_general(qc, ks[comp], (((1,), (1,)), ((), ())), preferred_element_type=F32)
              for ks in k_segs]
        m = ss[0].max(axis=-1, keepdims=True)
        for s in ss[1:]:
            m = jnp.maximum(m, s.max(axis=-1, keepdims=True))
        den = None
        pv = None
        for s, v in zip(ss, v_segs):
            e = jnp.exp(s - m)
            dsum = e.sum(axis=-1, keepdims=True)
            den = dsum if den is None else den + dsum
            part = jnp.dot(e.astype(BF16), v, preferred_element_type=F32)
            pv = part if pv is None else pv + part
        outs.append(pv * (1.0 / den))
    o = outs[0] - lam * outs[1]
    return o * _rms(o) * (g_sub * out_scale)


def _attn_ctx_kernel(q_ref, k_ref, v_ref, lam_ref, g_ref, o_ref, *, lam_init, hd):
    lam = _diff_lambda(lam_ref, lam_init)
    vd = 2 * hd
    for h in range(N_HEADS):
        c0 = h * vd
        q2 = [q_ref[:, c0:c0 + hd], q_ref[:, c0 + hd:c0 + vd]]
        k2 = (k_ref[:, c0:c0 + hd].astype(BF16), k_ref[:, c0 + hd:c0 + vd].astype(BF16))
        v = v_ref[:, c0:c0 + vd].astype(BF16)
        o = _diff_attend_head(q2, [k2], [v], lam, g_ref[...], 1.0 - lam_init, hd)
        o_ref[:, c0:c0 + vd] = o.astype(BF16)


def _attn_ctx(q, k, v, at_lam, g_sub, *, seq_len, lam_init):
    t, d = q.shape
    hd = d // N_HEADS // 2
    spec = pl.BlockSpec((seq_len, d), lambda b: (b, 0))
    return pl.pallas_call(
        functools.partial(_attn_ctx_kernel, lam_init=lam_init, hd=hd),
        grid=(t // seq_len,),
        in_specs=[spec, spec, spec,
                  pl.BlockSpec(at_lam.shape, lambda b: (0, 0)),
                  pl.BlockSpec((1, 2 * hd), lambda b: (0, 0))],
        out_specs=spec,
        out_shape=jax.ShapeDtypeStruct((t, d), BF16),
        compiler_params=_cparams(("parallel",)),
        name="attn_ctx",
    )(q, k, v, at_lam, g_sub.reshape(1, 2 * hd))


def _attn_lat_kernel(q_ref, k_ref, v_ref, ck_ref, cv_ref, cq_ref, sq_ref, ckk_ref, skk_ref,
                     lam_ref, g_ref, o_ref, kl_scr, kc_scr, vl_scr, vc_scr, *, lam_init, hd, sub):
    @pl.when(pl.program_id(2) == 0)
    def _():
        ckk, skk = ckk_ref[...], skk_ref[...]
        kl_scr[:, 0:hd] = _rope(k_ref[:, 0:hd], ckk, skk).astype(BF16)
        kl_scr[:, hd:2 * hd] = _rope(k_ref[:, hd:2 * hd], ckk, skk).astype(BF16)
        kc_scr[...] = ck_ref[...].astype(BF16)
        vl_scr[...] = v_ref[...].astype(BF16)
        vc_scr[...] = cv_ref[...].astype(BF16)

    lam = _diff_lambda(lam_ref, lam_init)
    kl = (kl_scr[:, 0:hd], kl_scr[:, hd:2 * hd])
    kc = (kc_scr[:, 0:hd], kc_scr[:, hd:2 * hd])
    for r0 in range(0, q_ref.shape[0], sub):
        rows = slice(r0, r0 + sub)
        cq, sq = cq_ref[rows, :], sq_ref[rows, :]
        q2 = [_rope(q_ref[rows, 0:hd], cq, sq), _rope(q_ref[rows, hd:2 * hd], cq, sq)]
        o = _diff_attend_head(q2, [kc, kl], [vc_scr[...], vl_scr[...]], lam, g_ref[...],
                              1.0 - lam_init, hd)
        o_ref[rows, :] = o.astype(BF16)


def _attn_lat(q, k, v, cache_k, cache_v, at_lam, g_sub, *, seq_len, lam_init):
    t, d = q.shape
    hd = d // N_HEADS // 2
    vd = 2 * hd
    nb = t // seq_len
    past = cache_k.shape[1]
    qb = min(1024, seq_len)
    nq = seq_len // qb
    cos, sin = _rope_tables(seq_len, hd)
    return pl.pallas_call(
        functools.partial(_attn_lat_kernel, lam_init=lam_init, hd=hd, sub=min(256, qb)),
        grid=(nb, N_HEADS, nq),
        in_specs=[
            pl.BlockSpec((qb, vd), lambda b, h, i: (b * nq + i, h)),
            pl.BlockSpec((seq_len, vd), lambda b, h, i: (b, h)),
            pl.BlockSpec((seq_len, vd), lambda b, h, i: (b, h)),
            pl.BlockSpec((None, past, vd), lambda b, h, i: (b, 0, h)),
            pl.BlockSpec((None, past, vd), lambda b, h, i: (b, 0, h)),
            pl.BlockSpec((qb, hd), lambda b, h, i: (i, 0)),
            pl.BlockSpec((qb, hd), lambda b, h, i: (i, 0)),
            pl.BlockSpec((seq_len, hd), lambda b, h, i: (0, 0)),
            pl.BlockSpec((seq_len, hd), lambda b, h, i: (0, 0)),
            pl.BlockSpec(at_lam.shape, lambda b, h, i: (0, 0)),
            pl.BlockSpec((1, vd), lambda b, h, i: (0, 0)),
        ],
        out_specs=pl.BlockSpec((qb, vd), lambda b, h, i: (b * nq + i, h)),
        out_shape=jax.ShapeDtypeStruct((t, d), BF16),
        scratch_shapes=[pltpu.VMEM((seq_len, vd), BF16), pltpu.VMEM((past, vd), BF16),
                        pltpu.VMEM((seq_len, vd), BF16), pltpu.VMEM((past, vd), BF16)],
        compiler_params=_cparams(("parallel", "parallel", "arbitrary")),
        name="attn_lat",
    )(q, k, v, cache_k, cache_v, cos, sin, cos, sin, at_lam, g_sub.reshape(1, vd))


def _s5_prep_kernel(lre_r, lim_r, ldt_r, btre_ref, btim_ref, ctre_ref, ctim_ref,
                    wst_ref, t_ref, wout_ref, lam_ref):
    Q = S5_CHUNK
    H = S5_GROUP
    P2 = lre_r.shape[-1]
    P = P2 // 2
    N = Q * H

    def cexp(n, re_dt, im_dt):
        mag = jnp.exp(n * re_dt)
        return mag * jnp.cos(n * im_dt), mag * jnp.sin(n * im_dt)

    re = jnp.minimum(lre_r[...], -1e-4)
    im = lim_r[...]
    dt = jnp.exp(ldt_r[...])
    re_dt, im_dt = re * dt, im * dt
    lb_re, lb_im = cexp(1.0, re_dt, im_dt)
    den = re * re + im * im
    q_re = ((lb_re - 1.0) * re + lb_im * im) / den
    q_im = (lb_im * re - (lb_re - 1.0) * im) / den
    bt_re, bt_im = btre_ref[...], btim_ref[...]
    bb_re = q_re * bt_re - q_im * bt_im
    bb_im = q_re * bt_im + q_im * bt_re
    fwd_lane = lax.broadcasted_iota(jnp.int32, (Q, P2), 1) < P
    srow = lax.broadcasted_iota(jnp.int32, (Q, P2), 0)
    n = jnp.where(fwd_lane, (Q - 1) - srow, srow).astype(F32)
    pw_re, pw_im = cexp(n, re_dt, im_dt)
    for s in range(Q):
        pr, pi = pw_re[s:s + 1, :], pw_im[s:s + 1, :]
        wst_ref[s * H:(s + 1) * H, 0:P2] = (bb_re * pr - bb_im * pi).astype(BF16)
        wst_ref[s * H:(s + 1) * H, P2:2 * P2] = (bb_re * pi + bb_im * pr).astype(BF16)
    lq_re, lq_im = cexp(float(Q), re_dt, im_dt)
    lam_ref[0:1, :] = lq_re
    lam_ref[1:2, :] = lq_im

    l1_re = jnp.transpose(jnp.broadcast_to(lb_re, (8, P2)))[:, 0:1]
    l1_im = jnp.transpose(jnp.broadcast_to(lb_im, (8, P2)))[:, 0:1]
    tlane = lax.broadcasted_iota(jnp.int32, (P2, N), 1) // H
    fwd_row = lax.broadcasted_iota(jnp.int32, (P2, N), 0) < P
    nt = jnp.where(fwd_row, tlane, (Q - 1) - tlane)
    g_re = jnp.ones((P2, N), F32)
    g_im = jnp.zeros((P2, N), F32)
    b_re, b_im = l1_re, l1_im
    bit = 1
    while bit < Q:
        use = (nt & bit) != 0
        f_re = jnp.where(use, b_re, 1.0)
        f_im = jnp.where(use, b_im, 0.0)
        g_re, g_im = g_re * f_re - g_im * f_im, g_re * f_im + g_im * f_re
        b_re, b_im = b_re * b_re - b_im * b_im, 2.0 * b_re * b_im
        bit *= 2
    c_re, c_im = ctre_ref[...], ctim_ref[...]
    gx_re = c_re * g_re - c_im * g_im
    gx_im = c_re * g_im + c_im * g_re
    wout_ref[0:P2, :] = (gx_re * l1_re - gx_im * l1_im).astype(BF16)
    wout_ref[P2:2 * P2, :] = (-(gx_re * l1_im + gx_im * l1_re)).astype(BF16)

    gx = jnp.concatenate([gx_re, gx_im], axis=0)
    is_f = lax.broadcasted_iota(jnp.int32, (H, P2), 1) < P
    zero = jnp.zeros((H, P2), F32)
    lhs_f = jnp.concatenate([jnp.where(is_f, bb_re, zero), jnp.where(is_f, -bb_im, zero)], axis=1)
    lhs_b = jnp.concatenate([jnp.where(is_f, zero, bb_re), jnp.where(is_f, zero, -bb_im)], axis=1)
    m_f = jnp.dot(lhs_f, gx, precision=HIGHEST, preferred_element_type=F32)
    m_b = jnp.dot(lhs_b, gx, precision=HIGHEST, preferred_element_type=F32)
    lane_n = lax.broadcasted_iota(jnp.int32, (H, N), 1)
    for s in range(Q):
        tf = m_f if s == 0 else pltpu.roll(m_f, s * H, 1)
        tb = m_b if s == Q - 1 else pltpu.roll(m_b, (s + 1) * H, 1)
        slab = jnp.where(lane_n >= s * H, tf, 0.0) + jnp.where(lane_n < (s + 1) * H, tb, 0.0)
        t_ref[s * H:(s + 1) * H, :] = slab.astype(BF16)


def _s5_prep(lam_re, lam_im, log_dt, b_re, b_im, c_re, c_im):
    _, G, P = lam_re.shape
    H = S5_GROUP
    N = S5_CHUNK * H
    P2 = 2 * P
    fb_lanes = lambda a: jnp.concatenate([a[0], a[1]], axis=-1)
    ldt = jnp.broadcast_to(log_dt[..., None], (2, G, P))
    rows = [fb_lanes(a).reshape(G, 1, P2) for a in (lam_re, lam_im, ldt)]
    bt = [fb_lanes(jnp.swapaxes(a, -1, -2)) for a in (b_re, b_im)]
    ct = [jnp.tile(jnp.concatenate([jnp.swapaxes(a[0], -1, -2), jnp.swapaxes(a[1], -1, -2)], axis=1),
                   (1, 1, S5_CHUNK)) for a in (c_re, c_im)]
    ins = rows + bt + ct
    gspec = lambda a: pl.BlockSpec((None,) + a.shape[1:], lambda g: (g, 0, 0))
    mspec = pl.BlockSpec((None, N, N), lambda g: (g, 0, 0))
    msh = jax.ShapeDtypeStruct((G, N, N), BF16)
    return pl.pallas_call(
        _s5_prep_kernel,
        grid=(G,),
        in_specs=[gspec(a) for a in ins],
        out_specs=[mspec, mspec, mspec, pl.BlockSpec((None, 2, P2), lambda g: (g, 0, 0))],
        out_shape=[msh, msh, msh, jax.ShapeDtypeStruct((G, 2, P2), F32)],
        compiler_params=_cparams(("parallel",)),
        name="s5_prep",
    )(*ins)


def _s5_core_kernel(h_ref, wst_ref, t_ref, wout_ref, lam_ref, s0_ref, y_ref, fin_ref,
                    u_scr, loc, sa, sb, *, nb, nc, gl):
    Q, H, GB = S5_CHUNK, S5_GROUP, S5_GROUPS_PER_STEP
    R = nb * nc
    P2 = lam_ref.shape[-1]
    lane_blk = lax.broadcasted_iota(jnp.int32, (R, LANES), 1) // H

    def gather_blocks(pieces, src_blk):
        acc = None
        for b, piece in enumerate(pieces):
            shift = ((b - src_blk) % GB) * H
            r = piece if shift == 0 else pltpu.roll(piece, shift, 1)
            acc = r if acc is None else jnp.where(lane_blk == b, r, acc)
        return acc

    hi_mask = jnp.int32(-65536)

    def pack2(a, b):
        abits = lax.bitcast_convert_type(a.astype(BF16).astype(F32), jnp.int32)
        bbits = lax.bitcast_convert_type(b.astype(BF16).astype(F32), jnp.int32)
        return (abits & hi_mask) | lax.shift_right_logical(bbits, 16)

    def unpack2(p):
        return [lax.bitcast_convert_type(p & hi_mask, F32),
                lax.bitcast_convert_type(lax.shift_left(p, 16), F32)]

    packed = [pack2(h_ref[:, s, :, :].reshape(R, LANES), h_ref[:, s + GB, :, :].reshape(R, LANES))
              for s in range(GB)]
    for g in range(GB):
        u = jnp.concatenate(unpack2(gather_blocks(packed, g)), axis=-1).astype(BF16)
        u_scr[g] = u
        loc[g] = jnp.dot(u, wst_ref[g], preferred_element_type=F32)

    fwd_half = lax.broadcasted_iota(jnp.int32, (nb, P2), 1) < P2 // 2
    for g0 in range(0, GB, gl):
        def body(k, carry, g0=g0):
            rf = pl.multiple_of(k * nb, nb)
            rb = pl.multiple_of((nc - 1 - k) * nb, nb)
            out = []
            for gi in range(gl):
                g = g0 + gi
                xr, xi = carry[2 * gi], carry[2 * gi + 1]
                sa[g, pl.ds(rf, nb), 0:P2] = xr
                sa[g, pl.ds(rf, nb), P2:2 * P2] = xi
                sb[g, pl.ds(rb, nb), 0:P2] = xr
                sb[g, pl.ds(rb, nb), P2:2 * P2] = xi
                lr = jnp.where(fwd_half, loc[g, pl.ds(rf, nb), 0:P2], loc[g, pl.ds(rb, nb), 0:P2])
                li = jnp.where(fwd_half, loc[g, pl.ds(rf, nb), P2:2 * P2],
                               loc[g, pl.ds(rb, nb), P2:2 * P2])
                ar, ai = lam_ref[g, 0:1, :], lam_ref[g, 1:2, :]
                out += [ar * xr - ai * xi + lr, ar * xi + ai * xr + li]
            return tuple(out)

        init = []
        for gi in range(gl):
            init += [s0_ref[g0 + gi, :, 0:P2], s0_ref[g0 + gi, :, P2:2 * P2]]
        fin = lax.fori_loop(0, nc, body, tuple(init))
        for gi in range(gl):
            fin_ref[g0 + gi, :, 0:P2] = fin[2 * gi]
            fin_ref[g0 + gi, :, P2:2 * P2] = fin[2 * gi + 1]

    fsel = (lax.broadcasted_iota(jnp.int32, (R, 2 * P2), 1) % P2) < P2 // 2
    for g in range(GB):
        s_in = jnp.where(fsel, sa[g], sb[g]).astype(BF16)
        loc[g] = (jnp.dot(u_scr[g], t_ref[g], preferred_element_type=F32)
                  + jnp.dot(s_in, wout_ref[g], preferred_element_type=F32))
    ys = [pack2(loc[g, :, 0:LANES], loc[g, :, LANES:2 * LANES]) for g in range(GB)]
    for tl in range(GB):
        lo, hi = unpack2(gather_blocks(ys, tl))
        y_ref[:, tl, :, :] = lo.reshape(nc, nb, LANES)
        y_ref[:, tl + GB, :, :] = hi.reshape(nc, nb, LANES)


def _s5_core(h_tm, s0, wst, tmat, wout, lam, *, nb, seq_len):
    d = h_tm.shape[1] // nb
    H, Q, GB = S5_GROUP, S5_CHUNK, S5_GROUPS_PER_STEP
    G = d // H
    nc = seq_len // Q
    R = nb * nc
    N = Q * H
    P4 = s0.shape[-1]
    gl = max(1, min(GB, (8 * GB) // nb))
    h4 = h_tm.reshape(nc, Q, nb, d)
    hspec = pl.BlockSpec((nc, Q, nb, LANES), lambda gb: (0, 0, 0, gb))
    mspec = pl.BlockSpec((GB, N, N), lambda gb: (gb, 0, 0))
    sspec = pl.BlockSpec((GB, nb, P4), lambda gb: (gb, 0, 0))
    y, fin = pl.pallas_call(
        functools.partial(_s5_core_kernel, nb=nb, nc=nc, gl=gl),
        grid=(G // GB,),
        in_specs=[hspec, mspec, mspec, mspec,
                  pl.BlockSpec((GB, 2, P4 // 2), lambda gb: (gb, 0, 0)), sspec],
        out_specs=[hspec, sspec],
        out_shape=[jax.ShapeDtypeStruct((nc, Q, nb, d), F32), jax.ShapeDtypeStruct((G, nb, P4), F32)],
        scratch_shapes=[pltpu.VMEM((GB, R, N), BF16), pltpu.VMEM((GB, R, N), F32),
                        pltpu.VMEM((GB, R, P4), F32), pltpu.VMEM((GB, R, P4), F32)],
        compiler_params=_cparams(("parallel",)),
        name="s5_core",
    )(h4, wst, tmat, wout, lam, s0)
    return y.reshape(seq_len, nb * d), fin


def _glu_postadd_kernel(*refs, nsub):
    h_refs, y_refs = refs[:nsub], refs[nsub:2 * nsub]
    (d_ref, wa_ref, wg_ref, ba_ref, bg_ref, x_ref, gate_ref, g_ref, o_ref, acc_a, acc_g, u_scr) = refs[2 * nsub:]
    k = pl.program_id(1)

    @pl.when(k == 0)
    def _():
        acc_a[...] = jnp.zeros_like(acc_a)
        acc_g[...] = jnp.zeros_like(acc_g)

    rows = h_refs[0].shape[0]
    for s in range(nsub):
        u_scr[s * rows:(s + 1) * rows, :] = jax.nn.gelu(
            d_ref[...] * h_refs[s][...] + y_refs[s][...]).astype(BF16)
    u = u_scr[...]
    acc_a[...] += jnp.dot(u, wa_ref[...], preferred_element_type=F32)
    acc_g[...] += jnp.dot(u, wg_ref[...], preferred_element_type=F32)

    @pl.when(k == pl.num_programs(1) - 1)
    def _():
        ba, bg = ba_ref[...], bg_ref[...]
        _post_add_rows(x_ref, lambda rows: (acc_a[rows, :] + ba) * jax.nn.sigmoid(acc_g[rows, :] + bg),
                       g_ref, gate_ref, o_ref)


def _glu_postadd(h_tm, y_tm, dskip, w_glu, layer, b_glu, x, m_l, g1, *, seq_len, latent):
    t, d = x.shape
    tm = min(512, t)
    nsub = max(1, tm // seq_len)
    per_seq = max(1, seq_len // tm)
    sub_rows = tm // nsub
    tk = 1024
    nk = d // tk
    rowfn = _row_fn(tm, seq_len, latent)
    b_glu = b_glu.reshape(1, 2 * d)
    tspecs = [pl.BlockSpec((sub_rows, tk), lambda i, k, s=s: (i % per_seq, ((i // per_seq) * nsub + s) * nk + k))
              for s in range(nsub)]
    return pl.pallas_call(
        functools.partial(_glu_postadd_kernel, nsub=nsub),
        grid=(t // tm, nk),
        in_specs=tspecs + tspecs + [
            pl.BlockSpec((1, tk), lambda i, k: (0, k)),
            pl.BlockSpec((None, None, tk, d), lambda i, k: (layer, 0, k, 0)),
            pl.BlockSpec((None, None, tk, d), lambda i, k: (layer, 1, k, 0)),
            pl.BlockSpec((1, d), lambda i, k: (0, 0)),
            pl.BlockSpec((1, d), lambda i, k: (0, 1)),
            pl.BlockSpec((tm, d), lambda i, k: (i, 0)),
            _mod_spec(d, 2, rowfn),
            pl.BlockSpec((1, d), lambda i, k: (0, 0)),
        ],
        out_specs=pl.BlockSpec((tm, d), lambda i, k: (i, 0)),
        out_shape=jax.ShapeDtypeStruct((t, d), F32),
        scratch_shapes=[pltpu.VMEM((tm, d), F32), pltpu.VMEM((tm, d), F32), pltpu.VMEM((tm, tk), BF16)],
        compiler_params=_cparams(("parallel", "arbitrary")),
        name="glu_postadd",
    )(*([h_tm] * nsub), *([y_tm] * nsub), dskip.reshape(1, d), w_glu, w_glu, b_glu, b_glu, x, m_l,
      g1.reshape(1, d))


def kernel(x_prompt, x_sample, cache_attn_k, cache_attn_v, state_s5_re, state_s5_im, c, c_ctx, w_mod, b_mod, g_norm, w_mlp_in, w_mlp_out, hy_w_in, hy_b_in, hy_w_short, hy_b_short, hy_f_w1, hy_f_b1, hy_f_freq1, hy_f_w2, hy_f_b2, hy_f_freq2, hy_f_w3, hy_log_alpha, hy_skip, hy_w_out, hy_b_out, at_w_qkv, at_lam, at_g_sub, at_w_o, s5_lam_re, s5_lam_im, s5_log_dt, s5_b_re, s5_b_im, s5_c_re, s5_c_im, s5_d, s5_w_glu, s5_b_glu):
    bc, lc, d = x_prompt.shape
    bl, ll, _ = x_sample.shape
    depth = w_mod.shape[0]
    assert 1 + bl <= MOD_ROWS
    assert cache_attn_k.shape[1] == 1 and state_s5_re.shape[1] == 1, "one attention and one S5 layer"
    hd = d // N_HEADS // 2
    G = d // S5_GROUP
    P = s5_lam_re.shape[-1]

    cond = jnp.concatenate([c_ctx[None], c, jnp.zeros((MOD_ROWS - 1 - bl, d), F32)], axis=0)
    mod = _modulation(cond, w_mod, b_mod).reshape(depth, MOD_ROWS, N_MOD, 1, d)

    streams = [dict(seq_len=lc, latent=False), dict(seq_len=ll, latent=True)]
    xs = [x_prompt.reshape(bc * lc, d), x_sample.reshape(bl * ll, d)]

    tables = {}
    for L in {lc, ll}:
        cm, sm, sp, ci, sip = _dft_tables(L)
        tables[L] = (_split_bf16(cm) + _split_bf16(sm), tuple(a.astype(BF16) for a in (cm, sp, ci, sip)))

    w_mlp_in_b = _cast_tiles(w_mlp_in, MLP_TF)
    w_mlp_out_b = _cast_tiles(w_mlp_out.reshape(-1, MLP_TF, d), d).reshape(w_mlp_out.shape)
    hy_w_in_b = _cast_tiles(hy_w_in, PROJ_TN)
    at_w_qkv_b = _cast_tiles(at_w_qkv, PROJ_TN)
    hy_w_out_b = _cast_tiles(hy_w_out, d).reshape(hy_w_out.shape)
    at_w_o_b = _cast_tiles(at_w_o, d).reshape(at_w_o.shape)
    s5_w_glu_b = _cast_tiles(s5_w_glu, d)

    new_k = new_v = None
    fin_ctx = None
    for i in range(depth):
        kind, j = i % 3, i // 3
        m_l = mod[i]
        g = g_norm[i]
        if kind == 0:
            filt = {}
            for L in {lc, ll}:
                filt[L] = _hy_filter(L, hy_f_w1[j], hy_f_b1[j], hy_f_freq1[j], hy_f_w2[j], hy_f_b2[j],
                                     hy_f_freq2[j], hy_f_w3[j], hy_log_alpha[j], tables[L][0])
            for si, st in enumerate(streams):
                L = st["seq_len"]
                x0, vg = _hy_inproj(xs[si], m_l, g[0], hy_w_in_b, j, hy_b_in[j], hy_w_short[j],
                                    hy_b_short[j], **st)
                a = _hy_conv(x0, vg, filt[L][0], filt[L][1], hy_skip[j], tables[L][1], seq_len=L)
                xs[si] = _mm_postadd(a, hy_w_out_b, j, hy_b_out[j], xs[si], m_l, g[1], **st)
        elif kind == 1:
            lam_init = 0.8 - 0.6 * math.exp(-0.3 * i)
            for si, st in enumerate(streams):
                L = st["seq_len"]
                q, k, v = _premod_mm3(xs[si], m_l, g[0], at_w_qkv_b, j, **st)
                if not st["latent"]:
                    new_k = k.reshape(bc, 1, lc, N_HEADS, 2, hd)
                    new_v = v.reshape(bc, 1, lc, N_HEADS, 2 * hd)
                    a = _attn_ctx(q, k, v, at_lam[j], at_g_sub[j], seq_len=L, lam_init=lam_init)
                else:
                    ck = cache_attn_k[:, j].reshape(bl, -1, d)
                    cv = cache_attn_v[:, j].reshape(bl, -1, d)
                    a = _attn_lat(q, k, v, ck, cv, at_lam[j], at_g_sub[j], seq_len=L, lam_init=lam_init)
                xs[si] = _mm_postadd(a, at_w_o_b, j, jnp.zeros((d,), F32), xs[si], m_l, g[1], **st)
        else:
            wst, tmat, wout, lam_q = _s5_prep(s5_lam_re[j], s5_lam_im[j], s5_log_dt[j], s5_b_re[j],
                                              s5_b_im[j], s5_c_re[j], s5_c_im[j])
            for si, st in enumerate(streams):
                L = st["seq_len"]
                nb = xs[si].shape[0] // L
                if st["latent"]:
                    sre, sim = state_s5_re[:, j], state_s5_im[:, j]
                    s0 = jnp.concatenate([sre[:, 0], sre[:, 1], sim[:, 0], sim[:, 1]], axis=-1)
                    s0 = s0.transpose(1, 0, 2)
                else:
                    s0 = jnp.zeros((G, nb, 4 * P), F32)
                h_tm = _premod_time_major(xs[si], m_l, g[0], **st)
                y_tm, fin = _s5_core(h_tm, s0, wst, tmat, wout, lam_q, nb=nb, seq_len=L)
                if not st["latent"]:
                    fin_ctx = fin.reshape(G, nb, 2, 2, P).transpose(1, 2, 3, 0, 4)
                xs[si] = _glu_postadd(h_tm, y_tm, s5_d[j], s5_w_glu_b, j, s5_b_glu[j], xs[si], m_l, g[1], **st)
        for si, st in enumerate(streams):
            xs[si] = _mlp(xs[si], m_l, g[2], g[3], w_mlp_in_b, w_mlp_out_b, i, **st)

    new_s_re = fin_ctx[:, 0][:, None]
    new_s_im = fin_ctx[:, 1][:, None]
    return (xs[0].reshape(bc, lc, d), xs[1].reshape(bl, ll, d), new_k, new_v, new_s_re, new_s_im)
```

```python
import functools
import math

import numpy as np
import jax
import jax.numpy as jnp
from jax import lax
from jax.experimental import pallas as pl
from jax.experimental.pallas import tpu as pltpu

F32 = jnp.float32
BF16 = jnp.bfloat16
HIGHEST = lax.Precision.HIGHEST

NORM_EPS = 1e-6
N_MOD = 6
N_HEADS = 8
GRID_W = 64
ROPE_BASE = 10000.0
HY_PE_BANDS = 16
HY_PE_MIN_PERIOD = 2.0
HY_PE_MAX_PERIOD = 4096.0
S5_GROUP = 16
S5_CHUNK = 16
LANES = 128
S5_GROUPS_PER_STEP = LANES // S5_GROUP
MOD_ROWS = 16
MLP_TF = 1024
PROJ_TN = 512

VMEM_LIMIT = 56 * 1024 * 1024


def _cparams(sem):
    return pltpu.CompilerParams(dimension_semantics=sem, vmem_limit_bytes=VMEM_LIMIT)


def _rms(x):
    return lax.rsqrt(jnp.mean(x * x, axis=-1, keepdims=True) + NORM_EPS)


def _split_bf16(a):
    hi = a.astype(BF16)
    return hi, (a - hi.astype(F32)).astype(BF16)


def _cast_kernel(w_ref, o_ref):
    o_ref[...] = w_ref[...].astype(BF16)


def _cast_tiles(w, tc):
    n, k, m = w.shape
    rk = min(k, (2 * 1024 * 1024) // tc)
    return pl.pallas_call(
        _cast_kernel,
        grid=(n, m // tc, k // rk),
        in_specs=[pl.BlockSpec((None, rk, tc), lambda l, j, r: (l, r, j))],
        out_specs=pl.BlockSpec((None, None, rk, tc), lambda l, j, r: (l, j, r, 0)),
        out_shape=jax.ShapeDtypeStruct((n, m // tc, k, tc), BF16),
        compiler_params=_cparams(("parallel", "parallel", "parallel")),
        name="cast_bf16",
    )(w)


def _mod_kernel(c_ref, w_ref, b_ref, o_ref):
    c = c_ref[...]
    s_hi, s_lo = _split_bf16(c * jax.nn.sigmoid(c))
    both = jnp.dot(jnp.concatenate([s_hi, s_lo], axis=0), w_ref[...].astype(BF16),
                   preferred_element_type=F32)
    o_ref[...] = both[:MOD_ROWS] + both[MOD_ROWS:] + b_ref[...]


def _modulation(cond, w_mod, b_mod):
    depth, d, n = w_mod.shape
    tn = 1024
    return pl.pallas_call(
        _mod_kernel,
        grid=(depth, n // tn),
        in_specs=[
            pl.BlockSpec((MOD_ROWS, d), lambda l, j: (0, 0)),
            pl.BlockSpec((None, d, tn), lambda l, j: (l, 0, j)),
            pl.BlockSpec((None, 1, tn), lambda l, j: (l, 0, j)),
        ],
        out_specs=pl.BlockSpec((None, MOD_ROWS, tn), lambda l, j: (l, 0, j)),
        out_shape=jax.ShapeDtypeStruct((depth, MOD_ROWS, n), F32),
        compiler_params=_cparams(("parallel", "parallel")),
        name="adaln_mod",
    )(cond, w_mod, b_mod.reshape(depth, 1, n))


def _mod_spec(d, which, rowfn):
    return pl.BlockSpec((None, None, 1, d), lambda i, *_: (rowfn(i), which, 0, 0))


def _lookahead_rows_spec(tm, d, nt):
    return pl.BlockSpec((tm, d), lambda i, j: (jnp.where(j >= 1, jnp.minimum(i + 1, nt - 1), i), 0))


def _row_fn(tm, seq_len, latent):
    if latent:
        assert seq_len % tm == 0, "a latent row tile must sit inside one sequence"
        return lambda i: 1 + (i * tm) // seq_len
    return lambda i: 0


ROW_CHUNK = 16


def _for_row_chunks(n_rows, fn):
    for c in range(n_rows // ROW_CHUNK):
        fn(pl.ds(c * ROW_CHUNK, ROW_CHUNK))


def _premod_rows(x_ref, g_ref, sh_ref, sc_ref, out_ref):
    gs, sh = g_ref[...] * (1.0 + sc_ref[...]), sh_ref[...]

    def rows_fn(rows):
        x = x_ref[rows, :]
        out_ref[rows, :] = ((x * _rms(x)) * gs + sh).astype(out_ref.dtype)

    _for_row_chunks(x_ref.shape[0], rows_fn)


def _post_add_rows(x_ref, o_fn, g_ref, gate_ref, out_ref):
    gg = gate_ref[...] * g_ref[...]

    def rows_fn(rows):
        o = o_fn(rows)
        out_ref[rows, :] = x_ref[rows, :] + (o * _rms(o)) * gg

    _for_row_chunks(x_ref.shape[0], rows_fn)


def _premod_mm3_kernel(x_ref, g_ref, sh_ref, sc_ref, wa_ref, wb_ref, wc_ref, oa_ref, ob_ref, oc_ref, h_scr):
    @pl.when(pl.program_id(1) == 0)
    def _():
        _premod_rows(x_ref, g_ref, sh_ref, sc_ref, h_scr)

    h = h_scr[...]
    for w_ref, o_ref in ((wa_ref, oa_ref), (wb_ref, ob_ref), (wc_ref, oc_ref)):
        o_ref[...] = jnp.dot(h, w_ref[...], preferred_element_type=F32)


def _premod_mm3(x, m_l, g, w, layer, *, seq_len, latent):
    t, d = x.shape
    tm = min(1024, seq_len if latent else t)
    tn = PROJ_TN
    nj = d // tn
    rowfn = _row_fn(tm, seq_len, latent)
    wspec = lambda o: pl.BlockSpec((None, None, d, tn), lambda i, j: (layer, o * nj + j, 0, 0))
    ospec = pl.BlockSpec((tm, tn), lambda i, j: (i, j))
    osh = jax.ShapeDtypeStruct((t, d), F32)
    return pl.pallas_call(
        _premod_mm3_kernel,
        grid=(t // tm, nj),
        in_specs=[
            _lookahead_rows_spec(tm, d, t // tm),
            pl.BlockSpec((1, d), lambda i, j: (0, 0)),
            _mod_spec(d, 0, rowfn),
            _mod_spec(d, 1, rowfn),
            wspec(0), wspec(1), wspec(2),
        ],
        out_specs=[ospec, ospec, ospec],
        out_shape=[osh, osh, osh],
        scratch_shapes=[pltpu.VMEM((tm, d), BF16)],
        compiler_params=_cparams(("parallel", "arbitrary")),
        name="premod_mm3",
    )(x, g.reshape(1, d), m_l, m_l, w, w, w)


def _premod_kernel(x_ref, g_ref, sh_ref, sc_ref, o_ref):
    _premod_rows(x_ref, g_ref, sh_ref, sc_ref, o_ref)


def _premod_time_major(x, m_l, g, *, seq_len, latent):
    t, d = x.shape
    nb = t // seq_len
    tm = min(512, seq_len)
    per_seq = seq_len // tm
    rowfn = _row_fn(tm, seq_len, latent)
    return pl.pallas_call(
        _premod_kernel,
        grid=(t // tm,),
        in_specs=[
            pl.BlockSpec((tm, d), lambda i: (i, 0)),
            pl.BlockSpec((1, d), lambda i: (0, 0)),
            _mod_spec(d, 0, rowfn),
            _mod_spec(d, 1, rowfn),
        ],
        out_specs=pl.BlockSpec((tm, d), lambda i: (i % per_seq, i // per_seq)),
        out_shape=jax.ShapeDtypeStruct((seq_len, nb * d), F32),
        compiler_params=_cparams(("parallel",)),
        name="premod",
    )(x, g.reshape(1, d), m_l, m_l)


def _mlp_kernel(x_ref, g2_ref, sh_ref, sc_ref, gate_ref, g3_ref, w1_ref, w2_ref, o_ref, h_scr, acc):
    f = pl.program_id(1)

    @pl.when(f == 0)
    def _():
        _premod_rows(x_ref, g2_ref, sh_ref, sc_ref, h_scr)
        acc[...] = jnp.zeros_like(acc)

    a = jnp.dot(h_scr[...], w1_ref[...], preferred_element_type=F32)
    a = jnp.square(jnp.maximum(a, 0.0)).astype(BF16)
    acc[...] += jnp.dot(a, w2_ref[...], preferred_element_type=F32)

    @pl.when(f == pl.num_programs(1) - 1)
    def _():
        _post_add_rows(x_ref, lambda rows: acc[rows, :], g3_ref, gate_ref, o_ref)


def _mlp(x, m_l, g2, g3, w1, w2, layer, *, seq_len, latent):
    t, d = x.shape
    tf = MLP_TF
    dff = w1.shape[1] * tf
    tm = min(512, t)
    rowfn = _row_fn(tm, seq_len, latent)
    return pl.pallas_call(
        _mlp_kernel,
        grid=(t // tm, dff // tf),
        in_specs=[
            pl.BlockSpec((tm, d), lambda i, f: (i, 0)),
            pl.BlockSpec((1, d), lambda i, f: (0, 0)),
            _mod_spec(d, 3, rowfn),
            _mod_spec(d, 4, rowfn),
            _mod_spec(d, 5, rowfn),
            pl.BlockSpec((1, d), lambda i, f: (0, 0)),
            pl.BlockSpec((None, None, d, tf), lambda i, f: (layer, f, 0, 0)),
            pl.BlockSpec((None, tf, d), lambda i, f: (layer, f, 0)),
        ],
        out_specs=pl.BlockSpec((tm, d), lambda i, f: (i, 0)),
        out_shape=jax.ShapeDtypeStruct((t, d), F32),
        scratch_shapes=[pltpu.VMEM((tm, d), BF16), pltpu.VMEM((tm, d), F32)],
        compiler_params=_cparams(("parallel", "arbitrary")),
        name="mlp",
    )(x, g2.reshape(1, d), m_l, m_l, m_l, g3.reshape(1, d), w1, w2)


def _mm_postadd_kernel(a_ref, w_ref, b_ref, x_ref, gate_ref, g_ref, o_ref, acc):
    acc[...] = jnp.dot(a_ref[...], w_ref[...], preferred_element_type=F32)
    b = b_ref[...]
    _post_add_rows(x_ref, lambda rows: acc[rows, :] + b, g_ref, gate_ref, o_ref)


def _mm_postadd(a, w, layer, b, x, m_l, g1, *, seq_len, latent):
    t, d = x.shape
    tm = min(512, t)
    rowfn = _row_fn(tm, seq_len, latent)
    return pl.pallas_call(
        _mm_postadd_kernel,
        grid=(t // tm,),
        in_specs=[
            pl.BlockSpec((tm, d), lambda i: (i, 0)),
            pl.BlockSpec((None, d, d), lambda i: (layer, 0, 0)),
            pl.BlockSpec((1, d), lambda i: (0, 0)),
            pl.BlockSpec((tm, d), lambda i: (i, 0)),
            _mod_spec(d, 2, rowfn),
            pl.BlockSpec((1, d), lambda i: (0, 0)),
        ],
        out_specs=pl.BlockSpec((tm, d), lambda i: (i, 0)),
        out_shape=jax.ShapeDtypeStruct((t, d), F32),
        scratch_shapes=[pltpu.VMEM((tm, d), F32)],
        compiler_params=_cparams(("parallel",)),
        name="mm_postadd",
    )(a, w, b.reshape(1, d), x, m_l, g1.reshape(1, d))


def _dft_tables(L):
    idx = np.arange(L, dtype=np.int64)
    ang = np.pi * ((idx[:, None] * idx[None, :]) % (2 * L)).astype(np.float64) / L
    c = np.cos(ang)
    s = np.sin(ang)
    alt = np.where(idx % 2 == 0, 1.0, -1.0)
    sp = s.copy()
    sp[0, :] = alt
    wf = np.full((L,), 2.0)
    wf[0] = 1.0
    ci = c * wf[None, :] / (2 * L)
    sip = -s * 2.0 / (2 * L)
    sip[:, 0] = alt / (2 * L)
    f32 = lambda a: jnp.asarray(a.astype(np.float32))
    return f32(c), f32(s), f32(sp), f32(ci), f32(sip)


def _hy_pe(L):
    t = np.arange(L, dtype=np.float64)
    periods = HY_PE_MIN_PERIOD * (HY_PE_MAX_PERIOD / HY_PE_MIN_PERIOD) ** (
        np.arange(HY_PE_BANDS, dtype=np.float64) / (HY_PE_BANDS - 1))
    ang = t[:, None] * (2.0 * math.pi / periods)[None]
    return jnp.asarray(np.concatenate([np.sin(ang), np.cos(ang)], axis=-1).astype(np.float32))


def _dot_bf16x3(a_hi, a_lo, b):
    b_hi, b_lo = _split_bf16(b)
    n = b.shape[1]
    both = jnp.dot(a_hi, jnp.concatenate([b_hi, b_lo], axis=1), preferred_element_type=F32)
    return both[:, :n] + both[:, n:] + jnp.dot(a_lo, b_hi, preferred_element_type=F32)


def _hy_filter_kernel(pe_ref, w1_ref, b1_ref, fr1_ref, w2_ref, b2_ref, fr2_ref, w3f_ref, w3b_ref,
                      la_ref, chi_ref, clo_ref, shi_ref, slo_ref, kre_ref, kim_ref, h_scr):
    L, dc = kre_ref.shape

    @pl.when(pl.program_id(0) == 0)
    def _():
        h1 = jnp.sin(fr1_ref[...] * (jnp.dot(pe_ref[...], w1_ref[...], precision=HIGHEST,
                                             preferred_element_type=F32) + b1_ref[...]))
        h_scr[...] = jnp.sin(fr2_ref[...] * (jnp.dot(h1, w2_ref[...], precision=HIGHEST,
                                                     preferred_element_type=F32) + b2_ref[...]))

    h = h_scr[...]
    row = lax.broadcasted_iota(jnp.int32, (L, dc), 0)
    dec = jnp.exp(-jnp.exp(la_ref[...]) * row.astype(F32))
    kf = jnp.dot(h, w3f_ref[...], precision=HIGHEST, preferred_element_type=F32) * dec
    kb = jnp.dot(h, w3b_ref[...], precision=HIGHEST, preferred_element_type=F32) * dec
    kb = jnp.where(row == 0, 0.0, kb)
    norm = jnp.sum(jnp.abs(kf) + jnp.abs(kb), axis=0, keepdims=True) + 1e-6
    inv = 1.0 / norm
    ks = (kf + kb) * inv
    kd = (kb - kf) * inv
    kre = _dot_bf16x3(chi_ref[...], clo_ref[...], ks)
    kim = _dot_bf16x3(shi_ref[...], slo_ref[...], kd)
    alt = jnp.where(row % 2 == 0, 1.0, -1.0)
    nyq = jnp.sum(alt * ks, axis=0, keepdims=True)
    kre_ref[...] = kre
    kim_ref[...] = jnp.where(row == 0, nyq, kim)


def _hy_filter(L, w1, b1, fr1, w2, b2, fr2, w3, log_alpha, cs_split):
    d = log_alpha.shape[-1]
    fw = w1.shape[1]
    dc = 512
    nd = d // dc
    full = lambda a: pl.BlockSpec(a.shape, lambda j: (0,) * a.ndim)
    pe = _hy_pe(L)
    b1, fr1, b2, fr2 = (a.reshape(1, fw) for a in (b1, fr1, b2, fr2))
    osh = jax.ShapeDtypeStruct((L, d), F32)
    return pl.pallas_call(
        _hy_filter_kernel,
        grid=(nd,),
        in_specs=[full(pe), full(w1), full(b1), full(fr1), full(w2), full(b2), full(fr2),
                  pl.BlockSpec((fw, dc), lambda j: (0, j)),
                  pl.BlockSpec((fw, dc), lambda j: (0, nd + j)),
                  pl.BlockSpec((1, dc), lambda j: (0, j))] + [full(a) for a in cs_split],
        out_specs=[pl.BlockSpec((L, dc), lambda j: (0, j))] * 2,
        out_shape=[osh, osh],
        scratch_shapes=[pltpu.VMEM((L, fw), F32)],
        compiler_params=_cparams(("arbitrary",)),
        name="hyena_filter",
    )(pe, w1, b1, fr1, w2, b2, fr2, w3, w3, log_alpha.reshape(1, d), *cs_split)


def _hy_inproj_kernel(x_ref, g_ref, sh_ref, sc_ref, w0_ref, w1_ref, wv_ref, b0_ref, b1_ref, bv_ref,
                      s0_ref, s1_ref, sv_ref, c0_ref, c1_ref, cv_ref, ox_ref, ov_ref, h_scr, *, seq_len):
    @pl.when(pl.program_id(1) == 0)
    def _():
        _premod_rows(x_ref, g_ref, sh_ref, sc_ref, h_scr)

    h = h_scr[...]
    tm, tn = ox_ref.shape
    pos = lax.broadcasted_iota(jnp.int32, (tm, tn), 0) % seq_len
    first, last = pos == 0, pos == seq_len - 1

    def section(w_ref, b_ref, s_ref, c_ref):
        z = jnp.dot(h, w_ref[...], preferred_element_type=F32) + b_ref[...]
        zm = jnp.where(first, 0.0, pltpu.roll(z, 1, 0))
        zp = jnp.where(last, 0.0, pltpu.roll(z, tm - 1, 0))
        return zm * s_ref[0:1, :] + z * s_ref[1:2, :] + zp * s_ref[2:3, :] + c_ref[...]

    ox_ref[...] = section(w0_ref, b0_ref, s0_ref, c0_ref).astype(ox_ref.dtype)
    x1 = section(w1_ref, b1_ref, s1_ref, c1_ref)
    ov_ref[...] = (section(wv_ref, bv_ref, sv_ref, cv_ref) * x1).astype(ov_ref.dtype)


def _hy_inproj(x, m_l, g, w, layer, b, w_sh, b_sh, *, seq_len, latent):
    t, d = x.shape
    tm = min(1024, seq_len if latent else t)
    assert tm % seq_len == 0, "row tiles hold whole sequences, so the short conv needs no halo"
    tn = PROJ_TN
    nj = d // tn
    rowfn = _row_fn(tm, seq_len, latent)
    wspec = lambda o: pl.BlockSpec((None, None, d, tn), lambda i, j: (layer, o * nj + j, 0, 0))
    vspec = lambda rows: (lambda o: pl.BlockSpec((rows, tn), lambda i, j: (0, o * nj + j)))
    bspec, sspec = vspec(1), vspec(3)
    ospec = pl.BlockSpec((tm, tn), lambda i, j: (i, j))
    osh = jax.ShapeDtypeStruct((t, d), BF16)
    b = b.reshape(1, 3 * d)
    b_sh = b_sh.reshape(1, 3 * d)
    return pl.pallas_call(
        functools.partial(_hy_inproj_kernel, seq_len=seq_len),
        grid=(t // tm, nj),
        in_specs=[
            _lookahead_rows_spec(tm, d, t // tm),
            pl.BlockSpec((1, d), lambda i, j: (0, 0)),
            _mod_spec(d, 0, rowfn),
            _mod_spec(d, 1, rowfn),
            wspec(0), wspec(1), wspec(2), bspec(0), bspec(1), bspec(2),
            sspec(0), sspec(1), sspec(2), bspec(0), bspec(1), bspec(2),
        ],
        out_specs=[ospec, ospec],
        out_shape=[osh, osh],
        scratch_shapes=[pltpu.VMEM((tm, d), BF16)],
        compiler_params=_cparams(("parallel", "arbitrary")),
        name="hyena_inproj",
    )(x, g.reshape(1, d), m_l, m_l, w, w, w, b, b, b, w_sh, w_sh, w_sh, b_sh, b_sh, b_sh)


def _hy_conv_kernel(x0_ref, v_ref, kre_ref, kim_ref, skip_ref, c_ref, sp_ref, ci_ref, sip_ref, o_ref):
    sb, L, dc = o_ref.shape
    kre = kre_ref[...]
    kim = kim_ref[...]
    first = lax.broadcasted_iota(jnp.int32, (L, dc), 0) == 0
    for s in range(sb):
        vb = v_ref[s]
        a = jnp.dot(c_ref[...], vb, preferred_element_type=F32)
        bm = jnp.dot(sp_ref[...], vb, preferred_element_type=F32)
        bk = bm * kim
        yre = a * kre + jnp.where(first, 0.0, bk)
        yim = jnp.where(first, bk, a * kim - bm * kre)
        y = (jnp.dot(ci_ref[...], yre.astype(BF16), preferred_element_type=F32)
             + jnp.dot(sip_ref[...], yim.astype(BF16), preferred_element_type=F32))
        o_ref[s] = (x0_ref[s].astype(F32) * (y + vb.astype(F32) * skip_ref[...])).astype(BF16)


def _hy_conv(x0, v, kre, kim, skip, tables_bf16, *, seq_len):
    t, d = x0.shape
    L = seq_len
    nb = t // L
    sb = max(1, min(nb, 1024 // L))
    dc = 512
    nd = d // dc
    once = pl.Buffered(1)
    zspec = pl.BlockSpec((sb, L, dc), lambda j, b: (b, 0, j))
    kspec = pl.BlockSpec((L, dc), lambda j, b: (0, j), pipeline_mode=once)
    mspec = pl.BlockSpec((L, L), lambda j, b: (0, 0), pipeline_mode=once)
    return pl.pallas_call(
        _hy_conv_kernel,
        grid=(nd, nb // sb),
        in_specs=[zspec, zspec, kspec, kspec, pl.BlockSpec((1, dc), lambda j, b: (0, j)),
                  mspec, mspec, mspec, mspec],
        out_specs=zspec,
        out_shape=jax.ShapeDtypeStruct((nb, L, d), BF16),
        compiler_params=_cparams(("parallel", "arbitrary")),
        name="hyena_conv",
    )(x0.reshape(nb, L, d), v.reshape(nb, L, d), kre, kim, skip.reshape(1, d), *tables_bf16).reshape(t, d)


def _rope_tables(L, head_dim):
    rows = L // GRID_W
    row = np.repeat(np.arange(rows), GRID_W).astype(np.float64)
    col = np.tile(np.arange(GRID_W), rows).astype(np.float64)
    half = head_dim // 2
    inv = ROPE_BASE ** (-np.arange(0, half, 2, dtype=np.float64) / half)
    ar = row[:, None] * inv
    ac = col[:, None] * inv
    ang = np.concatenate([ar, ar, ac, ac], axis=-1)
    return (jnp.asarray(np.cos(ang).astype(np.float32)), jnp.asarray(np.sin(ang).astype(np.float32)))


def _rope(x, cos, sin):
    hd = x.shape[-1]
    q = hd // 4
    lane = lax.broadcasted_iota(jnp.int32, x.shape, 1)
    rot = jnp.where((lane % (2 * q)) < q, -pltpu.roll(x, hd - q, 1), pltpu.roll(x, q, 1))
    return x * cos + rot * sin


def _diff_lambda(lam_ref, lam_init):
    lp = lam_ref[...]
    s01 = jnp.sum(lp[0:1, :] * lp[1:2, :], axis=-1, keepdims=True)
    s23 = jnp.sum(lp[2:3, :] * lp[3:4, :], axis=-1, keepdims=True)
    return jnp.exp(s01) - jnp.exp(s23) + lam_init


def _diff_attend_head(q2, k_segs, v_segs, lam, g_sub, out_scale, hd):
    scale = hd ** -0.5
    outs = []
    for comp in range(2):
        qc = (q2[comp] * scale).astype(BF16)
        ss = [lax.dot_general(qc, ks[comp], (((1,), (1,)), ((), ())), preferred_element_type=F32)
              for ks in k_segs]
        m = ss[0].max(axis=-1, keepdims=True)
        for s in ss[1:]:
            m = jnp.maximum(m, s.max(axis=-1, keepdims=True))
        den = None
        pv = None
        for s, v in zip(ss, v_segs):
            e = jnp.exp(s - m)
            dsum = e.sum(axis=-1, keepdims=True)
            den = dsum if den is None else den + dsum
            part = jnp.dot(e.astype(BF16), v, preferred_element_type=F32)
            pv = part if pv is None else pv + part
        outs.append(pv * (1.0 / den))
    o = outs[0] - lam * outs[1]
    return o * _rms(o) * (g_sub * out_scale)


def _attn_ctx_kernel(q_ref, k_ref, v_ref, lam_ref, g_ref, o_ref, *, lam_init, hd):
    lam = _diff_lambda(lam_ref, lam_init)
    vd = 2 * hd
    for h in range(N_HEADS):
        c0 = h * vd
        q2 = [q_ref[:, c0:c0 + hd], q_ref[:, c0 + hd:c0 + vd]]
        k2 = (k_ref[:, c0:c0 + hd].astype(BF16), k_ref[:, c0 + hd:c0 + vd].astype(BF16))
        v = v_ref[:, c0:c0 + vd].astype(BF16)
        o = _diff_attend_head(q2, [k2], [v], lam, g_ref[...], 1.0 - lam_init, hd)
        o_ref[:, c0:c0 + vd] = o.astype(BF16)


def _attn_ctx(q, k, v, at_lam, g_sub, *, seq_len, lam_init):
    t, d = q.shape
    hd = d // N_HEADS // 2
    spec = pl.BlockSpec((seq_len, d), lambda b: (b, 0))
    return pl.pallas_call(
        functools.partial(_attn_ctx_kernel, lam_init=lam_init, hd=hd),
        grid=(t // seq_len,),
        in_specs=[spec, spec, spec,
                  pl.BlockSpec(at_lam.shape, lambda b: (0, 0)),
                  pl.BlockSpec((1, 2 * hd), lambda b: (0, 0))],
        out_specs=spec,
        out_shape=jax.ShapeDtypeStruct((t, d), BF16),
        compiler_params=_cparams(("parallel",)),
        name="attn_ctx",
    )(q, k, v, at_lam, g_sub.reshape(1, 2 * hd))


def _attn_lat_kernel(q_ref, k_ref, v_ref, ck_ref, cv_ref, cq_ref, sq_ref, ckk_ref, skk_ref,
                     lam_ref, g_ref, o_ref, kl_scr, kc_scr, vl_scr, vc_scr, *, lam_init, hd, sub):
    @pl.when(pl.program_id(2) == 0)
    def _():
        ckk, skk = ckk_ref[...], skk_ref[...]
        kl_scr[:, 0:hd] = _rope(k_ref[:, 0:hd], ckk, skk).astype(BF16)
        kl_scr[:, hd:2 * hd] = _rope(k_ref[:, hd:2 * hd], ckk, skk).astype(BF16)
        kc_scr[...] = ck_ref[...].astype(BF16)
        vl_scr[...] = v_ref[...].astype(BF16)
        vc_scr[...] = cv_ref[...].astype(BF16)

    lam = _diff_lambda(lam_ref, lam_init)
    kl = (kl_scr[:, 0:hd], kl_scr[:, hd:2 * hd])
    kc = (kc_scr[:, 0:hd], kc_scr[:, hd:2 * hd])
    for r0 in range(0, q_ref.shape[0], sub):
        rows = slice(r0, r0 + sub)
        cq, sq = cq_ref[rows, :], sq_ref[rows, :]
        q2 = [_rope(q_ref[rows, 0:hd], cq, sq), _rope(q_ref[rows, hd:2 * hd], cq, sq)]
        o = _diff_attend_head(q2, [kc, kl], [vc_scr[...], vl_scr[...]], lam, g_ref[...],
                              1.0 - lam_init, hd)
        o_ref[rows, :] = o.astype(BF16)


def _attn_lat(q, k, v, cache_k, cache_v, at_lam, g_sub, *, seq_len, lam_init):
    t, d = q.shape
    hd = d // N_HEADS // 2
    vd = 2 * hd
    nb = t // seq_len
    past = cache_k.shape[1]
    qb = min(1024, seq_len)
    nq = seq_len // qb
    cos, sin = _rope_tables(seq_len, hd)
    return pl.pallas_call(
        functools.partial(_attn_lat_kernel, lam_init=lam_init, hd=hd, sub=min(256, qb)),
        grid=(nb, N_HEADS, nq),
        in_specs=[
            pl.BlockSpec((qb, vd), lambda b, h, i: (b * nq + i, h)),
            pl.BlockSpec((seq_len, vd), lambda b, h, i: (b, h)),
            pl.BlockSpec((seq_len, vd), lambda b, h, i: (b, h)),
            pl.BlockSpec((None, past, vd), lambda b, h, i: (b, 0, h)),
            pl.BlockSpec((None, past, vd), lambda b, h, i: (b, 0, h)),
            pl.BlockSpec((qb, hd), lambda b, h, i: (i, 0)),
            pl.BlockSpec((qb, hd), lambda b, h, i: (i, 0)),
            pl.BlockSpec((seq_len, hd), lambda b, h, i: (0, 0)),
            pl.BlockSpec((seq_len, hd), lambda b, h, i: (0, 0)),
            pl.BlockSpec(at_lam.shape, lambda b, h, i: (0, 0)),
            pl.BlockSpec((1, vd), lambda b, h, i: (0, 0)),
        ],
        out_specs=pl.BlockSpec((qb, vd), lambda b, h, i: (b * nq + i, h)),
        out_shape=jax.ShapeDtypeStruct((t, d), BF16),
        scratch_shapes=[pltpu.VMEM((seq_len, vd), BF16), pltpu.VMEM((past, vd), BF16),
                        pltpu.VMEM((seq_len, vd), BF16), pltpu.VMEM((past, vd), BF16)],
        compiler_params=_cparams(("parallel", "parallel", "arbitrary")),
        name="attn_lat",
    )(q, k, v, cache_k, cache_v, cos, sin, cos, sin, at_lam, g_sub.reshape(1, vd))


def _s5_prep_kernel(lre_r, lim_r, ldt_r, btre_ref, btim_ref, ctre_ref, ctim_ref,
                    wst_ref, t_ref, wout_ref, lam_ref):
    Q = S5_CHUNK
    H = S5_GROUP
    P2 = lre_r.shape[-1]
    P = P2 // 2
    N = Q * H

    def cexp(n, re_dt, im_dt):
        mag = jnp.exp(n * re_dt)
        return mag * jnp.cos(n * im_dt), mag * jnp.sin(n * im_dt)

    re = jnp.minimum(lre_r[...], -1e-4)
    im = lim_r[...]
    dt = jnp.exp(ldt_r[...])
    re_dt, im_dt = re * dt, im * dt
    lb_re, lb_im = cexp(1.0, re_dt, im_dt)
    den = re * re + im * im
    q_re = ((lb_re - 1.0) * re + lb_im * im) / den
    q_im = (lb_im * re - (lb_re - 1.0) * im) / den
    bt_re, bt_im = btre_ref[...], btim_ref[...]
    bb_re = q_re * bt_re - q_im * bt_im
    bb_im = q_re * bt_im + q_im * bt_re
    fwd_lane = lax.broadcasted_iota(jnp.int32, (Q, P2), 1) < P
    srow = lax.broadcasted_iota(jnp.int32, (Q, P2), 0)
    n = jnp.where(fwd_lane, (Q - 1) - srow, srow).astype(F32)
    pw_re, pw_im = cexp(n, re_dt, im_dt)
    for s in range(Q):
        pr, pi = pw_re[s:s + 1, :], pw_im[s:s + 1, :]
        wst_ref[s * H:(s + 1) * H, 0:P2] = (bb_re * pr - bb_im * pi).astype(BF16)
        wst_ref[s * H:(s + 1) * H, P2:2 * P2] = (bb_re * pi + bb_im * pr).astype(BF16)
    lq_re, lq_im = cexp(float(Q), re_dt, im_dt)
    lam_ref[0:1, :] = lq_re
    lam_ref[1:2, :] = lq_im

    l1_re = jnp.transpose(jnp.broadcast_to(lb_re, (8, P2)))[:, 0:1]
    l1_im = jnp.transpose(jnp.broadcast_to(lb_im, (8, P2)))[:, 0:1]
    tlane = lax.broadcasted_iota(jnp.int32, (P2, N), 1) // H
    fwd_row = lax.broadcasted_iota(jnp.int32, (P2, N), 0) < P
    nt = jnp.where(fwd_row, tlane, (Q - 1) - tlane)
    g_re = jnp.ones((P2, N), F32)
    g_im = jnp.zeros((P2, N), F32)
    b_re, b_im = l1_re, l1_im
    bit = 1
    while bit < Q:
        use = (nt & bit) != 0
        f_re = jnp.where(use, b_re, 1.0)
        f_im = jnp.where(use, b_im, 0.0)
        g_re, g_im = g_re * f_re - g_im * f_im, g_re * f_im + g_im * f_re
        b_re, b_im = b_re * b_re - b_im * b_im, 2.0 * b_re * b_im
        bit *= 2
    c_re, c_im = ctre_ref[...], ctim_ref[...]
    gx_re = c_re * g_re - c_im * g_im
    gx_im = c_re * g_im + c_im * g_re
    wout_ref[0:P2, :] = (gx_re * l1_re - gx_im * l1_im).astype(BF16)
    wout_ref[P2:2 * P2, :] = (-(gx_re * l1_im + gx_im * l1_re)).astype(BF16)

    gx = jnp.concatenate([gx_re, gx_im], axis=0)
    is_f = lax.broadcasted_iota(jnp.int32, (H, P2), 1) < P
    zero = jnp.zeros((H, P2), F32)
    lhs_f = jnp.concatenate([jnp.where(is_f, bb_re, zero), jnp.where(is_f, -bb_im, zero)], axis=1)
    lhs_b = jnp.concatenate([jnp.where(is_f, zero, bb_re), jnp.where(is_f, zero, -bb_im)], axis=1)
    m_f = jnp.dot(lhs_f, gx, precision=HIGHEST, preferred_element_type=F32)
    m_b = jnp.dot(lhs_b, gx, precision=HIGHEST, preferred_element_type=F32)
    lane_n = lax.broadcasted_iota(jnp.int32, (H, N), 1)
    for s in range(Q):
        tf = m_f if s == 0 else pltpu.roll(m_f, s * H, 1)
        tb = m_b if s == Q - 1 else pltpu.roll(m_b, (s + 1) * H, 1)
        slab = jnp.where(lane_n >= s * H, tf, 0.0) + jnp.where(lane_n < (s + 1) * H, tb, 0.0)
        t_ref[s * H:(s + 1) * H, :] = slab.astype(BF16)


def _s5_prep(lam_re, lam_im, log_dt, b_re, b_im, c_re, c_im):
    _, G, P = lam_re.shape
    H = S5_GROUP
    N = S5_CHUNK * H
    P2 = 2 * P
    fb_lanes = lambda a: jnp.concatenate([a[0], a[1]], axis=-1)
    ldt = jnp.broadcast_to(log_dt[..., None], (2, G, P))
    rows = [fb_lanes(a).reshape(G, 1, P2) for a in (lam_re, lam_im, ldt)]
    bt = [fb_lanes(jnp.swapaxes(a, -1, -2)) for a in (b_re, b_im)]
    ct = [jnp.tile(jnp.concatenate([jnp.swapaxes(a[0], -1, -2), jnp.swapaxes(a[1], -1, -2)], axis=1),
                   (1, 1, S5_CHUNK)) for a in (c_re, c_im)]
    ins = rows + bt + ct
    gspec = lambda a: pl.BlockSpec((None,) + a.shape[1:], lambda g: (g, 0, 0))
    mspec = pl.BlockSpec((None, N, N), lambda g: (g, 0, 0))
    msh = jax.ShapeDtypeStruct((G, N, N), BF16)
    return pl.pallas_call(
        _s5_prep_kernel,
        grid=(G,),
        in_specs=[gspec(a) for a in ins],
        out_specs=[mspec, mspec, mspec, pl.BlockSpec((None, 2, P2), lambda g: (g, 0, 0))],
        out_shape=[msh, msh, msh, jax.ShapeDtypeStruct((G, 2, P2), F32)],
        compiler_params=_cparams(("parallel",)),
        name="s5_prep",
    )(*ins)


def _s5_core_kernel(h_ref, wst_ref, t_ref, wout_ref, lam_ref, s0_ref, y_ref, fin_ref,
                    u_scr, loc, sa, sb, *, nb, nc, gl):
    Q, H, GB = S5_CHUNK, S5_GROUP, S5_GROUPS_PER_STEP
    R = nb * nc
    P2 = lam_ref.shape[-1]
    lane_blk = lax.broadcasted_iota(jnp.int32, (R, LANES), 1) // H

    def gather_blocks(pieces, src_blk):
        acc = None
        for b, piece in enumerate(pieces):
            shift = ((b - src_blk) % GB) * H
            r = piece if shift == 0 else pltpu.roll(piece, shift, 1)
            acc = r if acc is None else jnp.where(lane_blk == b, r, acc)
        return acc

    hi_mask = jnp.int32(-65536)

    def pack2(a, b):
        abits = lax.bitcast_convert_type(a.astype(BF16).astype(F32), jnp.int32)
        bbits = lax.bitcast_convert_type(b.astype(BF16).astype(F32), jnp.int32)
        return (abits & hi_mask) | lax.shift_right_logical(bbits, 16)

    def unpack2(p):
        return [lax.bitcast_convert_type(p & hi_mask, F32),
                lax.bitcast_convert_type(lax.shift_left(p, 16), F32)]

    packed = [pack2(h_ref[:, s, :, :].reshape(R, LANES), h_ref[:, s + GB, :, :].reshape(R, LANES))
              for s in range(GB)]
    for g in range(GB):
        u = jnp.concatenate(unpack2(gather_blocks(packed, g)), axis=-1).astype(BF16)
        u_scr[g] = u
        loc[g] = jnp.dot(u, wst_ref[g], preferred_element_type=F32)

    fwd_half = lax.broadcasted_iota(jnp.int32, (nb, P2), 1) < P2 // 2
    for g0 in range(0, GB, gl):
        def body(k, carry, g0=g0):
            rf = pl.multiple_of(k * nb, nb)
            rb = pl.multiple_of((nc - 1 - k) * nb, nb)
            out = []
            for gi in range(gl):
                g = g0 + gi
                xr, xi = carry[2 * gi], carry[2 * gi + 1]
                sa[g, pl.ds(rf, nb), 0:P2] = xr
                sa[g, pl.ds(rf, nb), P2:2 * P2] = xi
                sb[g, pl.ds(rb, nb), 0:P2] = xr
                sb[g, pl.ds(rb, nb), P2:2 * P2] = xi
                lr = jnp.where(fwd_half, loc[g, pl.ds(rf, nb), 0:P2], loc[g, pl.ds(rb, nb), 0:P2])
                li = jnp.where(fwd_half, loc[g, pl.ds(rf, nb), P2:2 * P2],
                               loc[g, pl.ds(rb, nb), P2:2 * P2])
                ar, ai = lam_ref[g, 0:1, :], lam_ref[g, 1:2, :]
                out += [ar * xr - ai * xi + lr, ar * xi + ai * xr + li]
            return tuple(out)

        init = []
        for gi in range(gl):
            init += [s0_ref[g0 + gi, :, 0:P2], s0_ref[g0 + gi, :, P2:2 * P2]]
        fin = lax.fori_loop(0, nc, body, tuple(init))
        for gi in range(gl):
            fin_ref[g0 + gi, :, 0:P2] = fin[2 * gi]
            fin_ref[g0 + gi, :, P2:2 * P2] = fin[2 * gi + 1]

    fsel = (lax.broadcasted_iota(jnp.int32, (R, 2 * P2), 1) % P2) < P2 // 2
    for g in range(GB):
        s_in = jnp.where(fsel, sa[g], sb[g]).astype(BF16)
        loc[g] = (jnp.dot(u_scr[g], t_ref[g], preferred_element_type=F32)
                  + jnp.dot(s_in, wout_ref[g], preferred_element_type=F32))
    ys = [pack2(loc[g, :, 0:LANES], loc[g, :, LANES:2 * LANES]) for g in range(GB)]
    for tl in range(GB):
        lo, hi = unpack2(gather_blocks(ys, tl))
        y_ref[:, tl, :, :] = lo.reshape(nc, nb, LANES)
        y_ref[:, tl + GB, :, :] = hi.reshape(nc, nb, LANES)


def _s5_core(h_tm, s0, wst, tmat, wout, lam, *, nb, seq_len):
    d = h_tm.shape[1] // nb
    H, Q, GB = S5_GROUP, S5_CHUNK, S5_GROUPS_PER_STEP
    G = d // H
    nc = seq_len // Q
    R = nb * nc
    N = Q * H
    P4 = s0.shape[-1]
    gl = max(1, min(GB, (8 * GB) // nb))
    h4 = h_tm.reshape(nc, Q, nb, d)
    hspec = pl.BlockSpec((nc, Q, nb, LANES), lambda gb: (0, 0, 0, gb))
    mspec = pl.BlockSpec((GB, N, N), lambda gb: (gb, 0, 0))
    sspec = pl.BlockSpec((GB, nb, P4), lambda gb: (gb, 0, 0))
    y, fin = pl.pallas_call(
        functools.partial(_s5_core_kernel, nb=nb, nc=nc, gl=gl),
        grid=(G // GB,),
        in_specs=[hspec, mspec, mspec, mspec,
                  pl.BlockSpec((GB, 2, P4 // 2), lambda gb: (gb, 0, 0)), sspec],
        out_specs=[hspec, sspec],
        out_shape=[jax.ShapeDtypeStruct((nc, Q, nb, d), F32), jax.ShapeDtypeStruct((G, nb, P4), F32)],
        scratch_shapes=[pltpu.VMEM((GB, R, N), BF16), pltpu.VMEM((GB, R, N), F32),
                        pltpu.VMEM((GB, R, P4), F32), pltpu.VMEM((GB, R, P4), F32)],
        compiler_params=_cparams(("parallel",)),
        name="s5_core",
    )(h4, wst, tmat, wout, lam, s0)
    return y.reshape(seq_len, nb * d), fin


def _glu_postadd_kernel(*refs, nsub):
    h_refs, y_refs = refs[:nsub], refs[nsub:2 * nsub]
    (d_ref, wa_ref, wg_ref, ba_ref, bg_ref, x_ref, gate_ref, g_ref, o_ref, acc_a, acc_g, u_scr) = refs[2 * nsub:]
    k = pl.program_id(1)

    @pl.when(k == 0)
    def _():
        acc_a[...] = jnp.zeros_like(acc_a)
        acc_g[...] = jnp.zeros_like(acc_g)

    rows = h_refs[0].shape[0]
    for s in range(nsub):
        u_scr[s * rows:(s + 1) * rows, :] = jax.nn.gelu(
            d_ref[...] * h_refs[s][...] + y_refs[s][...]).astype(BF16)
    u = u_scr[...]
    acc_a[...] += jnp.dot(u, wa_ref[...], preferred_element_type=F32)
    acc_g[...] += jnp.dot(u, wg_ref[...], preferred_element_type=F32)

    @pl.when(k == pl.num_programs(1) - 1)
    def _():
        ba, bg = ba_ref[...], bg_ref[...]
        _post_add_rows(x_ref, lambda rows: (acc_a[rows, :] + ba) * jax.nn.sigmoid(acc_g[rows, :] + bg),
                       g_ref, gate_ref, o_ref)


def _glu_postadd(h_tm, y_tm, dskip, w_glu, layer, b_glu, x, m_l, g1, *, seq_len, latent):
    t, d = x.shape
    tm = min(512, t)
    nsub = max(1, tm // seq_len)
    per_seq = max(1, seq_len // tm)
    sub_rows = tm // nsub
    tk = 1024
    nk = d // tk
    rowfn = _row_fn(tm, seq_len, latent)
    b_glu = b_glu.reshape(1, 2 * d)
    tspecs = [pl.BlockSpec((sub_rows, tk), lambda i, k, s=s: (i % per_seq, ((i // per_seq) * nsub + s) * nk + k))
              for s in range(nsub)]
    return pl.pallas_call(
        functools.partial(_glu_postadd_kernel, nsub=nsub),
        grid=(t // tm, nk),
        in_specs=tspecs + tspecs + [
            pl.BlockSpec((1, tk), lambda i, k: (0, k)),
            pl.BlockSpec((None, None, tk, d), lambda i, k: (layer, 0, k, 0)),
            pl.BlockSpec((None, None, tk, d), lambda i, k: (layer, 1, k, 0)),
            pl.BlockSpec((1, d), lambda i, k: (0, 0)),
            pl.BlockSpec((1, d), lambda i, k: (0, 1)),
            pl.BlockSpec((tm, d), lambda i, k: (i, 0)),
            _mod_spec(d, 2, rowfn),
            pl.BlockSpec((1, d), lambda i, k: (0, 0)),
        ],
        out_specs=pl.BlockSpec((tm, d), lambda i, k: (i, 0)),
        out_shape=jax.ShapeDtypeStruct((t, d), F32),
        scratch_shapes=[pltpu.VMEM((tm, d), F32), pltpu.VMEM((tm, d), F32), pltpu.VMEM((tm, tk), BF16)],
        compiler_params=_cparams(("parallel", "arbitrary")),
        name="glu_postadd",
    )(*([h_tm] * nsub), *([y_tm] * nsub), dskip.reshape(1, d), w_glu, w_glu, b_glu, b_glu, x, m_l,
      g1.reshape(1, d))


def kernel(x_prompt, x_sample, cache_attn_k, cache_attn_v, state_s5_re, state_s5_im, c, c_ctx, w_mod, b_mod, g_norm, w_mlp_in, w_mlp_out, hy_w_in, hy_b_in, hy_w_short, hy_b_short, hy_f_w1, hy_f_b1, hy_f_freq1, hy_f_w2, hy_f_b2, hy_f_freq2, hy_f_w3, hy_log_alpha, hy_skip, hy_w_out, hy_b_out, at_w_qkv, at_lam, at_g_sub, at_w_o, s5_lam_re, s5_lam_im, s5_log_dt, s5_b_re, s5_b_im, s5_c_re, s5_c_im, s5_d, s5_w_glu, s5_b_glu):
    bc, lc, d = x_prompt.shape
    bl, ll, _ = x_sample.shape
    depth = w_mod.shape[0]
    assert 1 + bl <= MOD_ROWS
    assert cache_attn_k.shape[1] == 1 and state_s5_re.shape[1] == 1, "one attention and one S5 layer"
    hd = d // N_HEADS // 2
    G = d // S5_GROUP
    P = s5_lam_re.shape[-1]

    cond = jnp.concatenate([c_ctx[None], c, jnp.zeros((MOD_ROWS - 1 - bl, d), F32)], axis=0)
    mod = _modulation(cond, w_mod, b_mod).reshape(depth, MOD_ROWS, N_MOD, 1, d)

    streams = [dict(seq_len=lc, latent=False), dict(seq_len=ll, latent=True)]
    xs = [x_prompt.reshape(bc * lc, d), x_sample.reshape(bl * ll, d)]

    tables = {}
    for L in {lc, ll}:
        cm, sm, sp, ci, sip = _dft_tables(L)
        tables[L] = (_split_bf16(cm) + _split_bf16(sm), tuple(a.astype(BF16) for a in (cm, sp, ci, sip)))

    w_mlp_in_b = _cast_tiles(w_mlp_in, MLP_TF)
    w_mlp_out_b = _cast_tiles(w_mlp_out.reshape(-1, MLP_TF, d), d).reshape(w_mlp_out.shape)
    hy_w_in_b = _cast_tiles(hy_w_in, PROJ_TN)
    at_w_qkv_b = _cast_tiles(at_w_qkv, PROJ_TN)
    hy_w_out_b = _cast_tiles(hy_w_out, d).reshape(hy_w_out.shape)
    at_w_o_b = _cast_tiles(at_w_o, d).reshape(at_w_o.shape)
    s5_w_glu_b = _cast_tiles(s5_w_glu, d)

    new_k = new_v = None
    fin_ctx = None
    for i in range(depth):
        kind, j = i % 3, i // 3
        m_l = mod[i]
        g = g_norm[i]
        if kind == 0:
            filt = {}
            for L in {lc, ll}:
                filt[L] = _hy_filter(L, hy_f_w1[j], hy_f_b1[j], hy_f_freq1[j], hy_f_w2[j], hy_f_b2[j],
                                     hy_f_freq2[j], hy_f_w3[j], hy_log_alpha[j], tables[L][0])
            for si, st in enumerate(streams):
                L = st["seq_len"]
                x0, vg = _hy_inproj(xs[si], m_l, g[0], hy_w_in_b, j, hy_b_in[j], hy_w_short[j],
                                    hy_b_short[j], **st)
                a = _hy_conv(x0, vg, filt[L][0], filt[L][1], hy_skip[j], tables[L][1], seq_len=L)
                xs[si] = _mm_postadd(a, hy_w_out_b, j, hy_b_out[j], xs[si], m_l, g[1], **st)
        elif kind == 1:
            lam_init = 0.8 - 0.6 * math.exp(-0.3 * i)
            for si, st in enumerate(streams):
                L = st["seq_len"]
                q, k, v = _premod_mm3(xs[si], m_l, g[0], at_w_qkv_b, j, **st)
                if not st["latent"]:
                    new_k = k.reshape(bc, 1, lc, N_HEADS, 2, hd)
                    new_v = v.reshape(bc, 1, lc, N_HEADS, 2 * hd)
                    a = _attn_ctx(q, k, v, at_lam[j], at_g_sub[j], seq_len=L, lam_init=lam_init)
                else:
                    ck = cache_attn_k[:, j].reshape(bl, -1, d)
                    cv = cache_attn_v[:, j].reshape(bl, -1, d)
                    a = _attn_lat(q, k, v, ck, cv, at_lam[j], at_g_sub[j], seq_len=L, lam_init=lam_init)
                xs[si] = _mm_postadd(a, at_w_o_b, j, jnp.zeros((d,), F32), xs[si], m_l, g[1], **st)
        else:
            wst, tmat, wout, lam_q = _s5_prep(s5_lam_re[j], s5_lam_im[j], s5_log_dt[j], s5_b_re[j],
                                              s5_b_im[j], s5_c_re[j], s5_c_im[j])
            for si, st in enumerate(streams):
                L = st["seq_len"]
                nb = xs[si].shape[0] // L
                if st["latent"]:
                    sre, sim = state_s5_re[:, j], state_s5_im[:, j]
                    s0 = jnp.concatenate([sre[:, 0], sre[:, 1], sim[:, 0], sim[:, 1]], axis=-1)
                    s0 = s0.transpose(1, 0, 2)
                else:
                    s0 = jnp.zeros((G, nb, 4 * P), F32)
                h_tm = _premod_time_major(xs[si], m_l, g[0], **st)
                y_tm, fin = _s5_core(h_tm, s0, wst, tmat, wout, lam_q, nb=nb, seq_len=L)
                if not st["latent"]:
                    fin_ctx = fin.reshape(G, nb, 2, 2, P).transpose(1, 2, 3, 0, 4)
                xs[si] = _glu_postadd(h_tm, y_tm, s5_d[j], s5_w_glu_b, j, s5_b_glu[j], xs[si], m_l, g[1], **st)
        for si, st in enumerate(streams):
            xs[si] = _mlp(xs[si], m_l, g[2], g[3], w_mlp_in_b, w_mlp_out_b, i, **st)

    new_s_re = fin_ctx[:, 0][:, None]
    new_s_im = fin_ctx[:, 1][:, None]
    return (xs[0].reshape(bc, lc, d), xs[1].reshape(bl, ll, d), new_k, new_v, new_s_re, new_s_im)
```

```python
import functools
import math

import numpy as np
import jax
import jax.numpy as jnp
from jax import lax
from jax.experimental import pallas as pl
from jax.experimental.pallas import tpu as pltpu

F32 = jnp.float32
BF16 = jnp.bfloat16
HIGHEST = lax.Precision.HIGHEST

NORM_EPS = 1e-6
N_MOD = 6
N_HEADS = 8
GRID_W = 64
ROPE_BASE = 10000.0
HY_PE_BANDS = 16
HY_PE_MIN_PERIOD = 2.0
HY_PE_MAX_PERIOD = 4096.0
S5_GROUP = 16
S5_CHUNK = 16
LANES = 128
S5_GROUPS_PER_STEP = LANES // S5_GROUP
MOD_ROWS = 16
MLP_TF = 1024
PROJ_TN = 512

VMEM_LIMIT = 56 * 1024 * 1024


def _cparams(sem):
    return pltpu.CompilerParams(dimension_semantics=sem, vmem_limit_bytes=VMEM_LIMIT)


def _rms(x):
    return lax.rsqrt(jnp.mean(x * x, axis=-1, keepdims=True) + NORM_EPS)


def _split_bf16(a):
    hi = a.astype(BF16)
    return hi, (a - hi.astype(F32)).astype(BF16)


def _cast_kernel(w_ref, o_ref):
    o_ref[...] = w_ref[...].astype(BF16)


def _cast_tiles(w, tc):
    n, k, m = w.shape
    rk = min(k, (2 * 1024 * 1024) // tc)
    return pl.pallas_call(
        _cast_kernel,
        grid=(n, m // tc, k // rk),
        in_specs=[pl.BlockSpec((None, rk, tc), lambda l, j, r: (l, r, j))],
        out_specs=pl.BlockSpec((None, None, rk, tc), lambda l, j, r: (l, j, r, 0)),
        out_shape=jax.ShapeDtypeStruct((n, m // tc, k, tc), BF16),
        compiler_params=_cparams(("parallel", "parallel", "parallel")),
        name="cast_bf16",
    )(w)


def _mod_kernel(c_ref, w_ref, b_ref, o_ref):
    c = c_ref[...]
    s_hi, s_lo = _split_bf16(c * jax.nn.sigmoid(c))
    both = jnp.dot(jnp.concatenate([s_hi, s_lo], axis=0), w_ref[...].astype(BF16),
                   preferred_element_type=F32)
    o_ref[...] = both[:MOD_ROWS] + both[MOD_ROWS:] + b_ref[...]


def _modulation(cond, w_mod, b_mod):
    depth, d, n = w_mod.shape
    tn = 1024
    return pl.pallas_call(
        _mod_kernel,
        grid=(depth, n // tn),
        in_specs=[
            pl.BlockSpec((MOD_ROWS, d), lambda l, j: (0, 0)),
            pl.BlockSpec((None, d, tn), lambda l, j: (l, 0, j)),
            pl.BlockSpec((None, 1, tn), lambda l, j: (l, 0, j)),
        ],
        out_specs=pl.BlockSpec((None, MOD_ROWS, tn), lambda l, j: (l, 0, j)),
        out_shape=jax.ShapeDtypeStruct((depth, MOD_ROWS, n), F32),
        compiler_params=_cparams(("parallel", "parallel")),
        name="adaln_mod",
    )(cond, w_mod, b_mod.reshape(depth, 1, n))


def _mod_spec(d, which, rowfn):
    return pl.BlockSpec((None, None, 1, d), lambda i, *_: (rowfn(i), which, 0, 0))


def _lookahead_rows_spec(tm, d, nt):
    return pl.BlockSpec((tm, d), lambda i, j: (jnp.where(j >= 1, jnp.minimum(i + 1, nt - 1), i), 0))


def _row_fn(tm, seq_len, latent):
    if latent:
        assert seq_len % tm == 0, "a latent row tile must sit inside one sequence"
        return lambda i: 1 + (i * tm) // seq_len
    return lambda i: 0


ROW_CHUNK = 16


def _for_row_chunks(n_rows, fn):
    for c in range(n_rows // ROW_CHUNK):
        fn(pl.ds(c * ROW_CHUNK, ROW_CHUNK))


def _premod_rows(x_ref, g_ref, sh_ref, sc_ref, out_ref):
    gs, sh = g_ref[...] * (1.0 + sc_ref[...]), sh_ref[...]

    def rows_fn(rows):
        x = x_ref[rows, :]
        out_ref[rows, :] = ((x * _rms(x)) * gs + sh).astype(out_ref.dtype)

    _for_row_chunks(x_ref.shape[0], rows_fn)


def _post_add_rows(x_ref, o_fn, g_ref, gate_ref, out_ref):
    gg = gate_ref[...] * g_ref[...]

    def rows_fn(rows):
        o = o_fn(rows)
        out_ref[rows, :] = x_ref[rows, :] + (o * _rms(o)) * gg

    _for_row_chunks(x_ref.shape[0], rows_fn)


def _premod_mm3_kernel(x_ref, g_ref, sh_ref, sc_ref, wa_ref, wb_ref, wc_ref, oa_ref, ob_ref, oc_ref, h_scr):
    @pl.when(pl.program_id(1) == 0)
    def _():
        _premod_rows(x_ref, g_ref, sh_ref, sc_ref, h_scr)

    h = h_scr[...]
    for w_ref, o_ref in ((wa_ref, oa_ref), (wb_ref, ob_ref), (wc_ref, oc_ref)):
        o_ref[...] = jnp.dot(h, w_ref[...], preferred_element_type=F32)


def _premod_mm3(x, m_l, g, w, layer, *, seq_len, latent):
    t, d = x.shape
    tm = min(1024, seq_len if latent else t)
    tn = PROJ_TN
    nj = d // tn
    rowfn = _row_fn(tm, seq_len, latent)
    wspec = lambda o: pl.BlockSpec((None, None, d, tn), lambda i, j: (layer, o * nj + j, 0, 0))
    ospec = pl.BlockSpec((tm, tn), lambda i, j: (i, j))
    osh = jax.ShapeDtypeStruct((t, d), F32)
    return pl.pallas_call(
        _premod_mm3_kernel,
        grid=(t // tm, nj),
        in_specs=[
            _lookahead_rows_spec(tm, d, t // tm),
            pl.BlockSpec((1, d), lambda i, j: (0, 0)),
            _mod_spec(d, 0, rowfn),
            _mod_spec(d, 1, rowfn),
            wspec(0), wspec(1), wspec(2),
        ],
        out_specs=[ospec, ospec, ospec],
        out_shape=[osh, osh, osh],
        scratch_shapes=[pltpu.VMEM((tm, d), BF16)],
        compiler_params=_cparams(("parallel", "arbitrary")),
        name="premod_mm3",
    )(x, g.reshape(1, d), m_l, m_l, w, w, w)


def _premod_kernel(x_ref, g_ref, sh_ref, sc_ref, o_ref):
    _premod_rows(x_ref, g_ref, sh_ref, sc_ref, o_ref)


def _premod_time_major(x, m_l, g, *, seq_len, latent):
    t, d = x.shape
    nb = t // seq_len
    tm = min(512, seq_len)
    per_seq = seq_len // tm
    rowfn = _row_fn(tm, seq_len, latent)
    return pl.pallas_call(
        _premod_kernel,
        grid=(t // tm,),
        in_specs=[
            pl.BlockSpec((tm, d), lambda i: (i, 0)),
            pl.BlockSpec((1, d), lambda i: (0, 0)),
            _mod_spec(d, 0, rowfn),
            _mod_spec(d, 1, rowfn),
        ],
        out_specs=pl.BlockSpec((tm, d), lambda i: (i % per_seq, i // per_seq)),
        out_shape=jax.ShapeDtypeStruct((seq_len, nb * d), F32),
        compiler_params=_cparams(("parallel",)),
        name="premod",
    )(x, g.reshape(1, d), m_l, m_l)


def _mlp_kernel(x_ref, g2_ref, sh_ref, sc_ref, gate_ref, g3_ref, w1_ref, w2_ref, o_ref, h_scr, acc):
    f = pl.program_id(1)

    @pl.when(f == 0)
    def _():
        _premod_rows(x_ref, g2_ref, sh_ref, sc_ref, h_scr)
        acc[...] = jnp.zeros_like(acc)

    def hidden_tile():
        a = jnp.dot(h_scr[...], w1_ref[...], preferred_element_type=F32)
        a = jnp.square(jnp.maximum(a, 0.0)).astype(BF16)
        acc[...] += jnp.dot(a, w2_ref[...], preferred_element_type=F32)

    last = pl.num_programs(1) - 1

    @pl.when(f < last)
    def _():
        hidden_tile()

    @pl.when(f == last)
    def _():
        hidden_tile()
        _post_add_rows(x_ref, lambda rows: acc[rows, :], g3_ref, gate_ref, o_ref)


def _mlp(x, m_l, g2, g3, w1, w2, layer, *, seq_len, latent):
    t, d = x.shape
    tf = MLP_TF
    dff = w1.shape[1] * tf
    tm = min(512, t)
    rowfn = _row_fn(tm, seq_len, latent)
    return pl.pallas_call(
        _mlp_kernel,
        grid=(t // tm, dff // tf),
        in_specs=[
            pl.BlockSpec((tm, d), lambda i, f: (i, 0)),
            pl.BlockSpec((1, d), lambda i, f: (0, 0)),
            _mod_spec(d, 3, rowfn),
            _mod_spec(d, 4, rowfn),
            _mod_spec(d, 5, rowfn),
            pl.BlockSpec((1, d), lambda i, f: (0, 0)),
            pl.BlockSpec((None, None, d, tf), lambda i, f: (layer, f, 0, 0)),
            pl.BlockSpec((None, tf, d), lambda i, f: (layer, f, 0)),
        ],
        out_specs=pl.BlockSpec((tm, d), lambda i, f: (i, 0)),
        out_shape=jax.ShapeDtypeStruct((t, d), F32),
        scratch_shapes=[pltpu.VMEM((tm, d), BF16), pltpu.VMEM((tm, d), F32)],
        compiler_params=_cparams(("parallel", "arbitrary")),
        name="mlp",
    )(x, g2.reshape(1, d), m_l, m_l, m_l, g3.reshape(1, d), w1, w2)


def _mm_postadd_kernel(a_ref, w_ref, b_ref, x_ref, gate_ref, g_ref, o_ref, acc):
    acc[...] = jnp.dot(a_ref[...], w_ref[...], preferred_element_type=F32)
    b = b_ref[...]
    _post_add_rows(x_ref, lambda rows: acc[rows, :] + b, g_ref, gate_ref, o_ref)


def _mm_postadd(a, w, layer, b, x, m_l, g1, *, seq_len, latent):
    t, d = x.shape
    tm = min(512, t)
    rowfn = _row_fn(tm, seq_len, latent)
    return pl.pallas_call(
        _mm_postadd_kernel,
        grid=(t // tm,),
        in_specs=[
            pl.BlockSpec((tm, d), lambda i: (i, 0)),
            pl.BlockSpec((None, d, d), lambda i: (layer, 0, 0)),
            pl.BlockSpec((1, d), lambda i: (0, 0)),
            pl.BlockSpec((tm, d), lambda i: (i, 0)),
            _mod_spec(d, 2, rowfn),
            pl.BlockSpec((1, d), lambda i: (0, 0)),
        ],
        out_specs=pl.BlockSpec((tm, d), lambda i: (i, 0)),
        out_shape=jax.ShapeDtypeStruct((t, d), F32),
        scratch_shapes=[pltpu.VMEM((tm, d), F32)],
        compiler_params=_cparams(("parallel",)),
        name="mm_postadd",
    )(a, w, b.reshape(1, d), x, m_l, g1.reshape(1, d))


def _dft_tables(L):
    idx = np.arange(L, dtype=np.int64)
    ang = np.pi * ((idx[:, None] * idx[None, :]) % (2 * L)).astype(np.float64) / L
    c = np.cos(ang)
    s = np.sin(ang)
    alt = np.where(idx % 2 == 0, 1.0, -1.0)
    sp = s.copy()
    sp[0, :] = alt
    wf = np.full((L,), 2.0)
    wf[0] = 1.0
    ci = c * wf[None, :] / (2 * L)
    sip = -s * 2.0 / (2 * L)
    sip[:, 0] = alt / (2 * L)
    f32 = lambda a: jnp.asarray(a.astype(np.float32))
    return f32(c), f32(s), f32(sp), f32(ci), f32(sip)


def _hy_pe(L):
    t = np.arange(L, dtype=np.float64)
    periods = HY_PE_MIN_PERIOD * (HY_PE_MAX_PERIOD / HY_PE_MIN_PERIOD) ** (
        np.arange(HY_PE_BANDS, dtype=np.float64) / (HY_PE_BANDS - 1))
    ang = t[:, None] * (2.0 * math.pi / periods)[None]
    return jnp.asarray(np.concatenate([np.sin(ang), np.cos(ang)], axis=-1).astype(np.float32))


def _dot_bf16x3(a_hi, a_lo, b):
    b_hi, b_lo = _split_bf16(b)
    n = b.shape[1]
    both = jnp.dot(a_hi, jnp.concatenate([b_hi, b_lo], axis=1), preferred_element_type=F32)
    return both[:, :n] + both[:, n:] + jnp.dot(a_lo, b_hi, preferred_element_type=F32)


def _hy_filter_kernel(pe_ref, w1_ref, b1_ref, fr1_ref, w2_ref, b2_ref, fr2_ref, w3f_ref, w3b_ref,
                      la_ref, chi_ref, clo_ref, shi_ref, slo_ref, kre_ref, kim_ref, h_scr):
    L, dc = kre_ref.shape

    @pl.when(pl.program_id(0) == 0)
    def _():
        h1 = jnp.sin(fr1_ref[...] * (jnp.dot(pe_ref[...], w1_ref[...], precision=HIGHEST,
                                             preferred_element_type=F32) + b1_ref[...]))
        h_scr[...] = jnp.sin(fr2_ref[...] * (jnp.dot(h1, w2_ref[...], precision=HIGHEST,
                                                     preferred_element_type=F32) + b2_ref[...]))

    h = h_scr[...]
    row = lax.broadcasted_iota(jnp.int32, (L, dc), 0)
    dec = jnp.exp(-jnp.exp(la_ref[...]) * row.astype(F32))
    kf = jnp.dot(h, w3f_ref[...], precision=HIGHEST, preferred_element_type=F32) * dec
    kb = jnp.dot(h, w3b_ref[...], precision=HIGHEST, preferred_element_type=F32) * dec
    kb = jnp.where(row == 0, 0.0, kb)
    norm = jnp.sum(jnp.abs(kf) + jnp.abs(kb), axis=0, keepdims=True) + 1e-6
    inv = 1.0 / norm
    ks = (kf + kb) * inv
    kd = (kb - kf) * inv
    kre = _dot_bf16x3(chi_ref[...], clo_ref[...], ks)
    kim = _dot_bf16x3(shi_ref[...], slo_ref[...], kd)
    alt = jnp.where(row % 2 == 0, 1.0, -1.0)
    nyq = jnp.sum(alt * ks, axis=0, keepdims=True)
    kre_ref[...] = kre
    kim_ref[...] = jnp.where(row == 0, nyq, kim)


def _hy_filter(L, w1, b1, fr1, w2, b2, fr2, w3, log_alpha, cs_split):
    d = log_alpha.shape[-1]
    fw = w1.shape[1]
    dc = 512
    nd = d // dc
    full = lambda a: pl.BlockSpec(a.shape, lambda j: (0,) * a.ndim)
    pe = _hy_pe(L)
    b1, fr1, b2, fr2 = (a.reshape(1, fw) for a in (b1, fr1, b2, fr2))
    osh = jax.ShapeDtypeStruct((L, d), F32)
    return pl.pallas_call(
        _hy_filter_kernel,
        grid=(nd,),
        in_specs=[full(pe), full(w1), full(b1), full(fr1), full(w2), full(b2), full(fr2),
                  pl.BlockSpec((fw, dc), lambda j: (0, j)),
                  pl.BlockSpec((fw, dc), lambda j: (0, nd + j)),
                  pl.BlockSpec((1, dc), lambda j: (0, j))] + [full(a) for a in cs_split],
        out_specs=[pl.BlockSpec((L, dc), lambda j: (0, j))] * 2,
        out_shape=[osh, osh],
        scratch_shapes=[pltpu.VMEM((L, fw), F32)],
        compiler_params=_cparams(("arbitrary",)),
        name="hyena_filter",
    )(pe, w1, b1, fr1, w2, b2, fr2, w3, w3, log_alpha.reshape(1, d), *cs_split)


def _hy_inproj_kernel(x_ref, g_ref, sh_ref, sc_ref, w0_ref, w1_ref, wv_ref, b0_ref, b1_ref, bv_ref,
                      s0_ref, s1_ref, sv_ref, c0_ref, c1_ref, cv_ref, ox_ref, ov_ref, h_scr, *, seq_len):
    @pl.when(pl.program_id(1) == 0)
    def _():
        _premod_rows(x_ref, g_ref, sh_ref, sc_ref, h_scr)

    h = h_scr[...]
    tm, tn = ox_ref.shape
    pos = lax.broadcasted_iota(jnp.int32, (tm, tn), 0) % seq_len
    first, last = pos == 0, pos == seq_len - 1

    def section(w_ref, b_ref, s_ref, c_ref):
        z = jnp.dot(h, w_ref[...], preferred_element_type=F32) + b_ref[...]
        zm = jnp.where(first, 0.0, pltpu.roll(z, 1, 0))
        zp = jnp.where(last, 0.0, pltpu.roll(z, tm - 1, 0))
        return zm * s_ref[0:1, :] + z * s_ref[1:2, :] + zp * s_ref[2:3, :] + c_ref[...]

    ox_ref[...] = section(w0_ref, b0_ref, s0_ref, c0_ref).astype(ox_ref.dtype)
    x1 = section(w1_ref, b1_ref, s1_ref, c1_ref)
    ov_ref[...] = (section(wv_ref, bv_ref, sv_ref, cv_ref) * x1).astype(ov_ref.dtype)


def _hy_inproj(x, m_l, g, w, layer, b, w_sh, b_sh, *, seq_len, latent):
    t, d = x.shape
    tm = min(1024, seq_len if latent else t)
    assert tm % seq_len == 0, "row tiles hold whole sequences, so the short conv needs no halo"
    tn = PROJ_TN
    nj = d // tn
    rowfn = _row_fn(tm, seq_len, latent)
    wspec = lambda o: pl.BlockSpec((None, None, d, tn), lambda i, j: (layer, o * nj + j, 0, 0))
    vspec = lambda rows: (lambda o: pl.BlockSpec((rows, tn), lambda i, j: (0, o * nj + j)))
    bspec, sspec = vspec(1), vspec(3)
    ospec = pl.BlockSpec((tm, tn), lambda i, j: (i, j))
    osh = jax.ShapeDtypeStruct((t, d), BF16)
    b = b.reshape(1, 3 * d)
    b_sh = b_sh.reshape(1, 3 * d)
    return pl.pallas_call(
        functools.partial(_hy_inproj_kernel, seq_len=seq_len),
        grid=(t // tm, nj),
        in_specs=[
            _lookahead_rows_spec(tm, d, t // tm),
            pl.BlockSpec((1, d), lambda i, j: (0, 0)),
            _mod_spec(d, 0, rowfn),
            _mod_spec(d, 1, rowfn),
            wspec(0), wspec(1), wspec(2), bspec(0), bspec(1), bspec(2),
            sspec(0), sspec(1), sspec(2), bspec(0), bspec(1), bspec(2),
        ],
        out_specs=[ospec, ospec],
        out_shape=[osh, osh],
        scratch_shapes=[pltpu.VMEM((tm, d), BF16)],
        compiler_params=_cparams(("parallel", "arbitrary")),
        name="hyena_inproj",
    )(x, g.reshape(1, d), m_l, m_l, w, w, w, b, b, b, w_sh, w_sh, w_sh, b_sh, b_sh, b_sh)


def _hy_conv_kernel(x0_ref, v_ref, kre_ref, kim_ref, skip_ref, c_ref, sp_ref, ci_ref, sip_ref, o_ref):
    sb, L, dc = o_ref.shape
    kre = kre_ref[...]
    kim = kim_ref[...]
    first = lax.broadcasted_iota(jnp.int32, (L, dc), 0) == 0
    for s in range(sb):
        vb = v_ref[s]
        a = jnp.dot(c_ref[...], vb, preferred_element_type=F32)
        bm = jnp.dot(sp_ref[...], vb, preferred_element_type=F32)
        bk = bm * kim
        yre = a * kre + jnp.where(first, 0.0, bk)
        yim = jnp.where(first, bk, a * kim - bm * kre)
        y = (jnp.dot(ci_ref[...], yre.astype(BF16), preferred_element_type=F32)
             + jnp.dot(sip_ref[...], yim.astype(BF16), preferred_element_type=F32))
        o_ref[s] = (x0_ref[s].astype(F32) * (y + vb.astype(F32) * skip_ref[...])).astype(BF16)


def _hy_conv(x0, v, kre, kim, skip, tables_bf16, *, seq_len):
    t, d = x0.shape
    L = seq_len
    nb = t // L
    sb = max(1, min(nb, 1024 // L))
    dc = 512
    nd = d // dc
    once = pl.Buffered(1)
    zspec = pl.BlockSpec((sb, L, dc), lambda j, b: (b, 0, j))
    kspec = pl.BlockSpec((L, dc), lambda j, b: (0, j), pipeline_mode=once)
    mspec = pl.BlockSpec((L, L), lambda j, b: (0, 0), pipeline_mode=once)
    return pl.pallas_call(
        _hy_conv_kernel,
        grid=(nd, nb // sb),
        in_specs=[zspec, zspec, kspec, kspec, pl.BlockSpec((1, dc), lambda j, b: (0, j)),
                  mspec, mspec, mspec, mspec],
        out_specs=zspec,
        out_shape=jax.ShapeDtypeStruct((nb, L, d), BF16),
        compiler_params=_cparams(("parallel", "arbitrary")),
        name="hyena_conv",
    )(x0.reshape(nb, L, d), v.reshape(nb, L, d), kre, kim, skip.reshape(1, d), *tables_bf16).reshape(t, d)


def _rope_tables(L, head_dim):
    rows = L // GRID_W
    row = np.repeat(np.arange(rows), GRID_W).astype(np.float64)
    col = np.tile(np.arange(GRID_W), rows).astype(np.float64)
    half = head_dim // 2
    inv = ROPE_BASE ** (-np.arange(0, half, 2, dtype=np.float64) / half)
    ar = row[:, None] * inv
    ac = col[:, None] * inv
    ang = np.concatenate([ar, ar, ac, ac], axis=-1)
    return (jnp.asarray(np.cos(ang).astype(np.float32)), jnp.asarray(np.sin(ang).astype(np.float32)))


def _rope(x, cos, sin):
    hd = x.shape[-1]
    q = hd // 4
    lane = lax.broadcasted_iota(jnp.int32, x.shape, 1)
    rot = jnp.where((lane % (2 * q)) < q, -pltpu.roll(x, hd - q, 1), pltpu.roll(x, q, 1))
    return x * cos + rot * sin


def _diff_lambda(lam_ref, lam_init):
    lp = lam_ref[...]
    s01 = jnp.sum(lp[0:1, :] * lp[1:2, :], axis=-1, keepdims=True)
    s23 = jnp.sum(lp[2:3, :] * lp[3:4, :], axis=-1, keepdims=True)
    return jnp.exp(s01) - jnp.exp(s23) + lam_init


def _diff_attend_head(q2, k_segs, v_segs, lam, g_sub, out_scale, hd):
    scale = hd ** -0.5
    outs = []
    for comp in range(2):
        qc = (q2[comp] * scale).astype(BF16)
        ss = [lax.dot_general(qc, ks[comp], (((1,), (1,)), ((), ())), preferred_element_type=F32)
              for ks in k_segs]
        m = ss[0].max(axis=-1, keepdims=True)
        for s in ss[1:]:
            m = jnp.maximum(m, s.max(axis=-1, keepdims=True))
        den = None
        pv = None
        for s, v in zip(ss, v_segs):
            e = jnp.exp(s - m)
            dsum = e.sum(axis=-1, keepdims=True)
            den = dsum if den is None else den + dsum
            part = jnp.dot(e.astype(BF16), v, preferred_element_type=F32)
            pv = part if pv is None else pv + part
        outs.append(pv * (1.0 / den))
    o = outs[0] - lam * outs[1]
    return o * _rms(o) * (g_sub * out_scale)


def _attn_ctx_kernel(q_ref, k_ref, v_ref, lam_ref, g_ref, o_ref, *, lam_init, hd):
    lam = _diff_lambda(lam_ref, lam_init)
    vd = 2 * hd
    for h in range(N_HEADS):
        c0 = h * vd
        q2 = [q_ref[:, c0:c0 + hd], q_ref[:, c0 + hd:c0 + vd]]
        k2 = (k_ref[:, c0:c0 + hd].astype(BF16), k_ref[:, c0 + hd:c0 + vd].astype(BF16))
        v = v_ref[:, c0:c0 + vd].astype(BF16)
        o = _diff_attend_head(q2, [k2], [v], lam, g_ref[...], 1.0 - lam_init, hd)
        o_ref[:, c0:c0 + vd] = o.astype(BF16)


def _attn_ctx(q, k, v, at_lam, g_sub, *, seq_len, lam_init):
    t, d = q.shape
    hd = d // N_HEADS // 2
    spec = pl.BlockSpec((seq_len, d), lambda b: (b, 0))
    return pl.pallas_call(
        functools.partial(_attn_ctx_kernel, lam_init=lam_init, hd=hd),
        grid=(t // seq_len,),
        in_specs=[spec, spec, spec,
                  pl.BlockSpec(at_lam.shape, lambda b: (0, 0)),
                  pl.BlockSpec((1, 2 * hd), lambda b: (0, 0))],
        out_specs=spec,
        out_shape=jax.ShapeDtypeStruct((t, d), BF16),
        compiler_params=_cparams(("parallel",)),
        name="attn_ctx",
    )(q, k, v, at_lam, g_sub.reshape(1, 2 * hd))


def _attn_lat_kernel(q_ref, k_ref, v_ref, ck_ref, cv_ref, cq_ref, sq_ref, ckk_ref, skk_ref,
                     lam_ref, g_ref, o_ref, kl_scr, kc_scr, vl_scr, vc_scr, *, lam_init, hd, sub):
    @pl.when(pl.program_id(2) == 0)
    def _():
        ckk, skk = ckk_ref[...], skk_ref[...]
        kl_scr[:, 0:hd] = _rope(k_ref[:, 0:hd], ckk, skk).astype(BF16)
        kl_scr[:, hd:2 * hd] = _rope(k_ref[:, hd:2 * hd], ckk, skk).astype(BF16)
        kc_scr[...] = ck_ref[...].astype(BF16)
        vl_scr[...] = v_ref[...].astype(BF16)
        vc_scr[...] = cv_ref[...].astype(BF16)

    lam = _diff_lambda(lam_ref, lam_init)
    kl = (kl_scr[:, 0:hd], kl_scr[:, hd:2 * hd])
    kc = (kc_scr[:, 0:hd], kc_scr[:, hd:2 * hd])
    for r0 in range(0, q_ref.shape[0], sub):
        rows = slice(r0, r0 + sub)
        cq, sq = cq_ref[rows, :], sq_ref[rows, :]
        q2 = [_rope(q_ref[rows, 0:hd], cq, sq), _rope(q_ref[rows, hd:2 * hd], cq, sq)]
        o = _diff_attend_head(q2, [kc, kl], [vc_scr[...], vl_scr[...]], lam, g_ref[...],
                              1.0 - lam_init, hd)
        o_ref[rows, :] = o.astype(BF16)


def _attn_lat(q, k, v, cache_k, cache_v, at_lam, g_sub, *, seq_len, lam_init):
    t, d = q.shape
    hd = d // N_HEADS // 2
    vd = 2 * hd
    nb = t // seq_len
    past = cache_k.shape[1]
    qb = min(1024, seq_len)
    nq = seq_len // qb
    cos, sin = _rope_tables(seq_len, hd)
    return pl.pallas_call(
        functools.partial(_attn_lat_kernel, lam_init=lam_init, hd=hd, sub=min(256, qb)),
        grid=(nb, N_HEADS, nq),
        in_specs=[
            pl.BlockSpec((qb, vd), lambda b, h, i: (b * nq + i, h)),
            pl.BlockSpec((seq_len, vd), lambda b, h, i: (b, h)),
            pl.BlockSpec((seq_len, vd), lambda b, h, i: (b, h)),
            pl.BlockSpec((None, past, vd), lambda b, h, i: (b, 0, h)),
            pl.BlockSpec((None, past, vd), lambda b, h, i: (b, 0, h)),
            pl.BlockSpec((qb, hd), lambda b, h, i: (i, 0)),
            pl.BlockSpec((qb, hd), lambda b, h, i: (i, 0)),
            pl.BlockSpec((seq_len, hd), lambda b, h, i: (0, 0)),
            pl.BlockSpec((seq_len, hd), lambda b, h, i: (0, 0)),
            pl.BlockSpec(at_lam.shape, lambda b, h, i: (0, 0)),
            pl.BlockSpec((1, vd), lambda b, h, i: (0, 0)),
        ],
        out_specs=pl.BlockSpec((qb, vd), lambda b, h, i: (b * nq + i, h)),
        out_shape=jax.ShapeDtypeStruct((t, d), BF16),
        scratch_shapes=[pltpu.VMEM((seq_len, vd), BF16), pltpu.VMEM((past, vd), BF16),
                        pltpu.VMEM((seq_len, vd), BF16), pltpu.VMEM((past, vd), BF16)],
        compiler_params=_cparams(("parallel", "parallel", "arbitrary")),
        name="attn_lat",
    )(q, k, v, cache_k, cache_v, cos, sin, cos, sin, at_lam, g_sub.reshape(1, vd))


def _s5_prep_kernel(lre_r, lim_r, ldt_r, btre_ref, btim_ref, ctre_ref, ctim_ref,
                    wst_ref, t_ref, wout_ref, lam_ref):
    Q = S5_CHUNK
    H = S5_GROUP
    P2 = lre_r.shape[-1]
    P = P2 // 2
    N = Q * H

    def cexp(n, re_dt, im_dt):
        mag = jnp.exp(n * re_dt)
        return mag * jnp.cos(n * im_dt), mag * jnp.sin(n * im_dt)

    re = jnp.minimum(lre_r[...], -1e-4)
    im = lim_r[...]
    dt = jnp.exp(ldt_r[...])
    re_dt, im_dt = re * dt, im * dt
    lb_re, lb_im = cexp(1.0, re_dt, im_dt)
    den = re * re + im * im
    q_re = ((lb_re - 1.0) * re + lb_im * im) / den
    q_im = (lb_im * re - (lb_re - 1.0) * im) / den
    bt_re, bt_im = btre_ref[...], btim_ref[...]
    bb_re = q_re * bt_re - q_im * bt_im
    bb_im = q_re * bt_im + q_im * bt_re
    fwd_lane = lax.broadcasted_iota(jnp.int32, (Q, P2), 1) < P
    srow = lax.broadcasted_iota(jnp.int32, (Q, P2), 0)
    n = jnp.where(fwd_lane, (Q - 1) - srow, srow).astype(F32)
    pw_re, pw_im = cexp(n, re_dt, im_dt)
    for s in range(Q):
        pr, pi = pw_re[s:s + 1, :], pw_im[s:s + 1, :]
        wst_ref[s * H:(s + 1) * H, 0:P2] = (bb_re * pr - bb_im * pi).astype(BF16)
        wst_ref[s * H:(s + 1) * H, P2:2 * P2] = (bb_re * pi + bb_im * pr).astype(BF16)
    lq_re, lq_im = cexp(float(Q), re_dt, im_dt)
    lam_ref[0:1, :] = lq_re
    lam_ref[1:2, :] = lq_im

    l1_re = jnp.transpose(jnp.broadcast_to(lb_re, (8, P2)))[:, 0:1]
    l1_im = jnp.transpose(jnp.broadcast_to(lb_im, (8, P2)))[:, 0:1]
    tlane = lax.broadcasted_iota(jnp.int32, (P2, N), 1) // H
    fwd_row = lax.broadcasted_iota(jnp.int32, (P2, N), 0) < P
    nt = jnp.where(fwd_row, tlane, (Q - 1) - tlane)
    g_re = jnp.ones((P2, N), F32)
    g_im = jnp.zeros((P2, N), F32)
    b_re, b_im = l1_re, l1_im
    bit = 1
    while bit < Q:
        use = (nt & bit) != 0
        f_re = jnp.where(use, b_re, 1.0)
        f_im = jnp.where(use, b_im, 0.0)
        g_re, g_im = g_re * f_re - g_im * f_im, g_re * f_im + g_im * f_re
        b_re, b_im = b_re * b_re - b_im * b_im, 2.0 * b_re * b_im
        bit *= 2
    c_re, c_im = ctre_ref[...], ctim_ref[...]
    gx_re = c_re * g_re - c_im * g_im
    gx_im = c_re * g_im + c_im * g_re
    wout_ref[0:P2, :] = (gx_re * l1_re - gx_im * l1_im).astype(BF16)
    wout_ref[P2:2 * P2, :] = (-(gx_re * l1_im + gx_im * l1_re)).astype(BF16)

    gx = jnp.concatenate([gx_re, gx_im], axis=0)
    is_f = lax.broadcasted_iota(jnp.int32, (H, P2), 1) < P
    zero = jnp.zeros((H, P2), F32)
    lhs_f = jnp.concatenate([jnp.where(is_f, bb_re, zero), jnp.where(is_f, -bb_im, zero)], axis=1)
    lhs_b = jnp.concatenate([jnp.where(is_f, zero, bb_re), jnp.where(is_f, zero, -bb_im)], axis=1)
    m_f = jnp.dot(lhs_f, gx, precision=HIGHEST, preferred_element_type=F32)
    m_b = jnp.dot(lhs_b, gx, precision=HIGHEST, preferred_element_type=F32)
    lane_n = lax.broadcasted_iota(jnp.int32, (H, N), 1)
    for s in range(Q):
        tf = m_f if s == 0 else pltpu.roll(m_f, s * H, 1)
        tb = m_b if s == Q - 1 else pltpu.roll(m_b, (s + 1) * H, 1)
        slab = jnp.where(lane_n >= s * H, tf, 0.0) + jnp.where(lane_n < (s + 1) * H, tb, 0.0)
        t_ref[s * H:(s + 1) * H, :] = slab.astype(BF16)


def _s5_prep(lam_re, lam_im, log_dt, b_re, b_im, c_re, c_im):
    _, G, P = lam_re.shape
    H = S5_GROUP
    N = S5_CHUNK * H
    P2 = 2 * P
    fb_lanes = lambda a: jnp.concatenate([a[0], a[1]], axis=-1)
    ldt = jnp.broadcast_to(log_dt[..., None], (2, G, P))
    rows = [fb_lanes(a).reshape(G, 1, P2) for a in (lam_re, lam_im, ldt)]
    bt = [fb_lanes(jnp.swapaxes(a, -1, -2)) for a in (b_re, b_im)]
    ct = [jnp.tile(jnp.concatenate([jnp.swapaxes(a[0], -1, -2), jnp.swapaxes(a[1], -1, -2)], axis=1),
                   (1, 1, S5_CHUNK)) for a in (c_re, c_im)]
    ins = rows + bt + ct
    gspec = lambda a: pl.BlockSpec((None,) + a.shape[1:], lambda g: (g, 0, 0))
    mspec = pl.BlockSpec((None, N, N), lambda g: (g, 0, 0))
    msh = jax.ShapeDtypeStruct((G, N, N), BF16)
    return pl.pallas_call(
        _s5_prep_kernel,
        grid=(G,),
        in_specs=[gspec(a) for a in ins],
        out_specs=[mspec, mspec, mspec, pl.BlockSpec((None, 2, P2), lambda g: (g, 0, 0))],
        out_shape=[msh, msh, msh, jax.ShapeDtypeStruct((G, 2, P2), F32)],
        compiler_params=_cparams(("parallel",)),
        name="s5_prep",
    )(*ins)


def _s5_core_kernel(h_ref, wst_ref, t_ref, wout_ref, lam_ref, s0_ref, y_ref, fin_ref,
                    u_scr, loc, sa, sb, *, nb, nc, gl):
    Q, H, GB = S5_CHUNK, S5_GROUP, S5_GROUPS_PER_STEP
    R = nb * nc
    P2 = lam_ref.shape[-1]
    lane_blk = lax.broadcasted_iota(jnp.int32, (R, LANES), 1) // H

    def gather_blocks(pieces, src_blk):
        acc = None
        for b, piece in enumerate(pieces):
            shift = ((b - src_blk) % GB) * H
            r = piece if shift == 0 else pltpu.roll(piece, shift, 1)
            acc = r if acc is None else jnp.where(lane_blk == b, r, acc)
        return acc

    hi_mask = jnp.int32(-65536)

    def pack2(a, b):
        abits = lax.bitcast_convert_type(a.astype(BF16).astype(F32), jnp.int32)
        bbits = lax.bitcast_convert_type(b.astype(BF16).astype(F32), jnp.int32)
        return (abits & hi_mask) | lax.shift_right_logical(bbits, 16)

    def unpack2(p):
        return [lax.bitcast_convert_type(p & hi_mask, F32),
                lax.bitcast_convert_type(lax.shift_left(p, 16), F32)]

    packed = [pack2(h_ref[:, s, :, :].reshape(R, LANES), h_ref[:, s + GB, :, :].reshape(R, LANES))
              for s in range(GB)]
    for g in range(GB):
        u = jnp.concatenate(unpack2(gather_blocks(packed, g)), axis=-1).astype(BF16)
        u_scr[g] = u
        loc[g] = jnp.dot(u, wst_ref[g], preferred_element_type=F32)

    fwd_half = lax.broadcasted_iota(jnp.int32, (nb, P2), 1) < P2 // 2
    for g0 in range(0, GB, gl):
        def body(k, carry, g0=g0):
            rf = pl.multiple_of(k * nb, nb)
            rb = pl.multiple_of((nc - 1 - k) * nb, nb)
            out = []
            for gi in range(gl):
                g = g0 + gi
                xr, xi = carry[2 * gi], carry[2 * gi + 1]
                sa[g, pl.ds(rf, nb), 0:P2] = xr
                sa[g, pl.ds(rf, nb), P2:2 * P2] = xi
                sb[g, pl.ds(rb, nb), 0:P2] = xr
                sb[g, pl.ds(rb, nb), P2:2 * P2] = xi
                lr = jnp.where(fwd_half, loc[g, pl.ds(rf, nb), 0:P2], loc[g, pl.ds(rb, nb), 0:P2])
                li = jnp.where(fwd_half, loc[g, pl.ds(rf, nb), P2:2 * P2],
                               loc[g, pl.ds(rb, nb), P2:2 * P2])
                ar, ai = lam_ref[g, 0:1, :], lam_ref[g, 1:2, :]
                out += [ar * xr - ai * xi + lr, ar * xi + ai * xr + li]
            return tuple(out)

        init = []
        for gi in range(gl):
            init += [s0_ref[g0 + gi, :, 0:P2], s0_ref[g0 + gi, :, P2:2 * P2]]
        fin = lax.fori_loop(0, nc, body, tuple(init))
        for gi in range(gl):
            fin_ref[g0 + gi, :, 0:P2] = fin[2 * gi]
            fin_ref[g0 + gi, :, P2:2 * P2] = fin[2 * gi + 1]

    fsel = (lax.broadcasted_iota(jnp.int32, (R, 2 * P2), 1) % P2) < P2 // 2
    for g in range(GB):
        s_in = jnp.where(fsel, sa[g], sb[g]).astype(BF16)
        loc[g] = (jnp.dot(u_scr[g], t_ref[g], preferred_element_type=F32)
                  + jnp.dot(s_in, wout_ref[g], preferred_element_type=F32))
    ys = [pack2(loc[g, :, 0:LANES], loc[g, :, LANES:2 * LANES]) for g in range(GB)]
    for tl in range(GB):
        lo, hi = unpack2(gather_blocks(ys, tl))
        y_ref[:, tl, :, :] = lo.reshape(nc, nb, LANES)
        y_ref[:, tl + GB, :, :] = hi.reshape(nc, nb, LANES)


def _s5_core(h_tm, s0, wst, tmat, wout, lam, *, nb, seq_len):
    d = h_tm.shape[1] // nb
    H, Q, GB = S5_GROUP, S5_CHUNK, S5_GROUPS_PER_STEP
    G = d // H
    nc = seq_len // Q
    R = nb * nc
    N = Q * H
    P4 = s0.shape[-1]
    gl = max(1, min(GB, (8 * GB) // nb))
    h4 = h_tm.reshape(nc, Q, nb, d)
    hspec = pl.BlockSpec((nc, Q, nb, LANES), lambda gb: (0, 0, 0, gb))
    mspec = pl.BlockSpec((GB, N, N), lambda gb: (gb, 0, 0))
    sspec = pl.BlockSpec((GB, nb, P4), lambda gb: (gb, 0, 0))
    y, fin = pl.pallas_call(
        functools.partial(_s5_core_kernel, nb=nb, nc=nc, gl=gl),
        grid=(G // GB,),
        in_specs=[hspec, mspec, mspec, mspec,
                  pl.BlockSpec((GB, 2, P4 // 2), lambda gb: (gb, 0, 0)), sspec],
        out_specs=[hspec, sspec],
        out_shape=[jax.ShapeDtypeStruct((nc, Q, nb, d), F32), jax.ShapeDtypeStruct((G, nb, P4), F32)],
        scratch_shapes=[pltpu.VMEM((GB, R, N), BF16), pltpu.VMEM((GB, R, N), F32),
                        pltpu.VMEM((GB, R, P4), F32), pltpu.VMEM((GB, R, P4), F32)],
        compiler_params=_cparams(("parallel",)),
        name="s5_core",
    )(h4, wst, tmat, wout, lam, s0)
    return y.reshape(seq_len, nb * d), fin


def _glu_postadd_kernel(*refs, nsub):
    h_refs, y_refs = refs[:nsub], refs[nsub:2 * nsub]
    (d_ref, wa_ref, wg_ref, ba_ref, bg_ref, x_ref, gate_ref, g_ref, o_ref, acc_a, acc_g, u_scr) = refs[2 * nsub:]
    k = pl.program_id(1)

    @pl.when(k == 0)
    def _():
        acc_a[...] = jnp.zeros_like(acc_a)
        acc_g[...] = jnp.zeros_like(acc_g)

    def contraction_tile():
        rows = h_refs[0].shape[0]
        for s in range(nsub):
            u_scr[s * rows:(s + 1) * rows, :] = jax.nn.gelu(
                d_ref[...] * h_refs[s][...] + y_refs[s][...]).astype(BF16)
        u = u_scr[...]
        acc_a[...] += jnp.dot(u, wa_ref[...], preferred_element_type=F32)
        acc_g[...] += jnp.dot(u, wg_ref[...], preferred_element_type=F32)

    last = pl.num_programs(1) - 1

    @pl.when(k < last)
    def _():
        contraction_tile()

    @pl.when(k == last)
    def _():
        contraction_tile()
        ba, bg = ba_ref[...], bg_ref[...]
        _post_add_rows(x_ref, lambda rows: (acc_a[rows, :] + ba) * jax.nn.sigmoid(acc_g[rows, :] + bg),
                       g_ref, gate_ref, o_ref)


def _glu_postadd(h_tm, y_tm, dskip, w_glu, layer, b_glu, x, m_l, g1, *, seq_len, latent):
    t, d = x.shape
    tm = min(512, t)
    nsub = max(1, tm // seq_len)
    per_seq = max(1, seq_len // tm)
    sub_rows = tm // nsub
    tk = 1024
    nk = d // tk
    rowfn = _row_fn(tm, seq_len, latent)
    b_glu = b_glu.reshape(1, 2 * d)
    tspecs = [pl.BlockSpec((sub_rows, tk), lambda i, k, s=s: (i % per_seq, ((i // per_seq) * nsub + s) * nk + k))
              for s in range(nsub)]
    return pl.pallas_call(
        functools.partial(_glu_postadd_kernel, nsub=nsub),
        grid=(t // tm, nk),
        in_specs=tspecs + tspecs + [
            pl.BlockSpec((1, tk), lambda i, k: (0, k)),
            pl.BlockSpec((None, None, tk, d), lambda i, k: (layer, 0, k, 0)),
            pl.BlockSpec((None, None, tk, d), lambda i, k: (layer, 1, k, 0)),
            pl.BlockSpec((1, d), lambda i, k: (0, 0)),
            pl.BlockSpec((1, d), lambda i, k: (0, 1)),
            pl.BlockSpec((tm, d), lambda i, k: (i, 0)),
            _mod_spec(d, 2, rowfn),
            pl.BlockSpec((1, d), lambda i, k: (0, 0)),
        ],
        out_specs=pl.BlockSpec((tm, d), lambda i, k: (i, 0)),
        out_shape=jax.ShapeDtypeStruct((t, d), F32),
        scratch_shapes=[pltpu.VMEM((tm, d), F32), pltpu.VMEM((tm, d), F32), pltpu.VMEM((tm, tk), BF16)],
        compiler_params=_cparams(("parallel", "arbitrary")),
        name="glu_postadd",
    )(*([h_tm] * nsub), *([y_tm] * nsub), dskip.reshape(1, d), w_glu, w_glu, b_glu, b_glu, x, m_l,
      g1.reshape(1, d))


def kernel(x_prompt, x_sample, cache_attn_k, cache_attn_v, state_s5_re, state_s5_im, c, c_ctx, w_mod, b_mod, g_norm, w_mlp_in, w_mlp_out, hy_w_in, hy_b_in, hy_w_short, hy_b_short, hy_f_w1, hy_f_b1, hy_f_freq1, hy_f_w2, hy_f_b2, hy_f_freq2, hy_f_w3, hy_log_alpha, hy_skip, hy_w_out, hy_b_out, at_w_qkv, at_lam, at_g_sub, at_w_o, s5_lam_re, s5_lam_im, s5_log_dt, s5_b_re, s5_b_im, s5_c_re, s5_c_im, s5_d, s5_w_glu, s5_b_glu):
    bc, lc, d = x_prompt.shape
    bl, ll, _ = x_sample.shape
    depth = w_mod.shape[0]
    assert 1 + bl <= MOD_ROWS
    assert cache_attn_k.shape[1] == 1 and state_s5_re.shape[1] == 1, "one attention and one S5 layer"
    hd = d // N_HEADS // 2
    G = d // S5_GROUP
    P = s5_lam_re.shape[-1]

    cond = jnp.concatenate([c_ctx[None], c, jnp.zeros((MOD_ROWS - 1 - bl, d), F32)], axis=0)
    mod = _modulation(cond, w_mod, b_mod).reshape(depth, MOD_ROWS, N_MOD, 1, d)

    streams = [dict(seq_len=lc, latent=False), dict(seq_len=ll, latent=True)]
    xs = [x_prompt.reshape(bc * lc, d), x_sample.reshape(bl * ll, d)]

    tables = {}
    for L in {lc, ll}:
        cm, sm, sp, ci, sip = _dft_tables(L)
        tables[L] = (_split_bf16(cm) + _split_bf16(sm), tuple(a.astype(BF16) for a in (cm, sp, ci, sip)))

    w_mlp_in_b = _cast_tiles(w_mlp_in, MLP_TF)
    w_mlp_out_b = _cast_tiles(w_mlp_out.reshape(-1, MLP_TF, d), d).reshape(w_mlp_out.shape)
    hy_w_in_b = _cast_tiles(hy_w_in, PROJ_TN)
    at_w_qkv_b = _cast_tiles(at_w_qkv, PROJ_TN)
    hy_w_out_b = _cast_tiles(hy_w_out, d).reshape(hy_w_out.shape)
    at_w_o_b = _cast_tiles(at_w_o, d).reshape(at_w_o.shape)
    s5_w_glu_b = _cast_tiles(s5_w_glu, d)

    new_k = new_v = None
    fin_ctx = None
    for i in range(depth):
        kind, j = i % 3, i // 3
        m_l = mod[i]
        g = g_norm[i]
        if kind == 0:
            filt = {}
            for L in {lc, ll}:
                filt[L] = _hy_filter(L, hy_f_w1[j], hy_f_b1[j], hy_f_freq1[j], hy_f_w2[j], hy_f_b2[j],
                                     hy_f_freq2[j], hy_f_w3[j], hy_log_alpha[j], tables[L][0])
            for si, st in enumerate(streams):
                L = st["seq_len"]
                x0, vg = _hy_inproj(xs[si], m_l, g[0], hy_w_in_b, j, hy_b_in[j], hy_w_short[j],
                                    hy_b_short[j], **st)
                a = _hy_conv(x0, vg, filt[L][0], filt[L][1], hy_skip[j], tables[L][1], seq_len=L)
                xs[si] = _mm_postadd(a, hy_w_out_b, j, hy_b_out[j], xs[si], m_l, g[1], **st)
        elif kind == 1:
            lam_init = 0.8 - 0.6 * math.exp(-0.3 * i)
            for si, st in enumerate(streams):
                L = st["seq_len"]
                q, k, v = _premod_mm3(xs[si], m_l, g[0], at_w_qkv_b, j, **st)
                if not st["latent"]:
                    new_k = k.reshape(bc, 1, lc, N_HEADS, 2, hd)
                    new_v = v.reshape(bc, 1, lc, N_HEADS, 2 * hd)
                    a = _attn_ctx(q, k, v, at_lam[j], at_g_sub[j], seq_len=L, lam_init=lam_init)
                else:
                    ck = cache_attn_k[:, j].reshape(bl, -1, d)
                    cv = cache_attn_v[:, j].reshape(bl, -1, d)
                    a = _attn_lat(q, k, v, ck, cv, at_lam[j], at_g_sub[j], seq_len=L, lam_init=lam_init)
                xs[si] = _mm_postadd(a, at_w_o_b, j, jnp.zeros((d,), F32), xs[si], m_l, g[1], **st)
        else:
            wst, tmat, wout, lam_q = _s5_prep(s5_lam_re[j], s5_lam_im[j], s5_log_dt[j], s5_b_re[j],
                                              s5_b_im[j], s5_c_re[j], s5_c_im[j])
            for si, st in enumerate(streams):
                L = st["seq_len"]
                nb = xs[si].shape[0] // L
                if st["latent"]:
                    sre, sim = state_s5_re[:, j], state_s5_im[:, j]
                    s0 = jnp.concatenate([sre[:, 0], sre[:, 1], sim[:, 0], sim[:, 1]], axis=-1)
                    s0 = s0.transpose(1, 0, 2)
                else:
                    s0 = jnp.zeros((G, nb, 4 * P), F32)
                h_tm = _premod_time_major(xs[si], m_l, g[0], **st)
                y_tm, fin = _s5_core(h_tm, s0, wst, tmat, wout, lam_q, nb=nb, seq_len=L)
                if not st["latent"]:
                    fin_ctx = fin.reshape(G, nb, 2, 2, P).transpose(1, 2, 3, 0, 4)
                xs[si] = _glu_postadd(h_tm, y_tm, s5_d[j], s5_w_glu_b, j, s5_b_glu[j], xs[si], m_l, g[1], **st)
        for si, st in enumerate(streams):
            xs[si] = _mlp(xs[si], m_l, g[2], g[3], w_mlp_in_b, w_mlp_out_b, i, **st)

    new_s_re = fin_ctx[:, 0][:, None]
    new_s_im = fin_ctx[:, 1][:, None]
    return (xs[0].reshape(bc, lc, d), xs[1].reshape(bl, ll, d), new_k, new_v, new_s_re, new_s_im)
```

```python
import functools
import math

import numpy as np
import jax
import jax.numpy as jnp
from jax import lax
from jax.experimental import pallas as pl
from jax.experimental.pallas import tpu as pltpu

F32 = jnp.float32
BF16 = jnp.bfloat16
HIGHEST = lax.Precision.HIGHEST

NORM_EPS = 1e-6
N_MOD = 6
N_HEADS = 8
GRID_W = 64
ROPE_BASE = 10000.0
HY_PE_BANDS = 16
HY_PE_MIN_PERIOD = 2.0
HY_PE_MAX_PERIOD = 4096.0
S5_GROUP = 16
S5_CHUNK = 16
LANES = 128
S5_GROUPS_PER_STEP = LANES // S5_GROUP
MOD_ROWS = 16
MLP_TF = 1024
PROJ_TN = 512

VMEM_LIMIT = 56 * 1024 * 1024


def _cparams(sem):
    return pltpu.CompilerParams(dimension_semantics=sem, vmem_limit_bytes=VMEM_LIMIT)


def _rms(x):
    return lax.rsqrt(jnp.mean(x * x, axis=-1, keepdims=True) + NORM_EPS)


def _split_bf16(a):
    hi = a.astype(BF16)
    return hi, (a - hi.astype(F32)).astype(BF16)


def _cast_kernel(w_ref, o_ref):
    o_ref[...] = w_ref[...].astype(BF16)


def _cast_tiles(w, tc):
    n, k, m = w.shape
    rk = min(k, (2 * 1024 * 1024) // tc)
    return pl.pallas_call(
        _cast_kernel,
        grid=(n, m // tc, k // rk),
        in_specs=[pl.BlockSpec((None, rk, tc), lambda l, j, r: (l, r, j))],
        out_specs=pl.BlockSpec((None, None, rk, tc), lambda l, j, r: (l, j, r, 0)),
        out_shape=jax.ShapeDtypeStruct((n, m // tc, k, tc), BF16),
        compiler_params=_cparams(("parallel", "parallel", "parallel")),
        name="cast_bf16",
    )(w)


def _mod_kernel(c_ref, w_ref, b_ref, o_ref):
    c = c_ref[...]
    s_hi, s_lo = _split_bf16(c * jax.nn.sigmoid(c))
    both = jnp.dot(jnp.concatenate([s_hi, s_lo], axis=0), w_ref[...].astype(BF16),
                   preferred_element_type=F32)
    o_ref[...] = both[:MOD_ROWS] + both[MOD_ROWS:] + b_ref[...]


def _modulation(cond, w_mod, b_mod):
    depth, d, n = w_mod.shape
    tn = 1024
    return pl.pallas_call(
        _mod_kernel,
        grid=(depth, n // tn),
        in_specs=[
            pl.BlockSpec((MOD_ROWS, d), lambda l, j: (0, 0)),
            pl.BlockSpec((None, d, tn), lambda l, j: (l, 0, j)),
            pl.BlockSpec((None, 1, tn), lambda l, j: (l, 0, j)),
        ],
        out_specs=pl.BlockSpec((None, MOD_ROWS, tn), lambda l, j: (l, 0, j)),
        out_shape=jax.ShapeDtypeStruct((depth, MOD_ROWS, n), F32),
        compiler_params=_cparams(("parallel", "parallel")),
        name="adaln_mod",
    )(cond, w_mod, b_mod.reshape(depth, 1, n))


def _mod_spec(d, which, rowfn):
    return pl.BlockSpec((None, None, 1, d), lambda i, *_: (rowfn(i), which, 0, 0))


def _lookahead_rows_spec(tm, d, nt):
    return pl.BlockSpec((tm, d), lambda i, j: (jnp.where(j >= 1, jnp.minimum(i + 1, nt - 1), i), 0))


def _row_fn(tm, seq_len, latent):
    if latent:
        assert seq_len % tm == 0, "a latent row tile must sit inside one sequence"
        return lambda i: 1 + (i * tm) // seq_len
    return lambda i: 0


ROW_CHUNK = 16


def _for_row_chunks(n_rows, fn):
    for c in range(n_rows // ROW_CHUNK):
        fn(pl.ds(c * ROW_CHUNK, ROW_CHUNK))


def _premod_rows(x_ref, g_ref, sh_ref, sc_ref, out_ref):
    gs, sh = g_ref[...] * (1.0 + sc_ref[...]), sh_ref[...]

    def rows_fn(rows):
        x = x_ref[rows, :]
        out_ref[rows, :] = ((x * _rms(x)) * gs + sh).astype(out_ref.dtype)

    _for_row_chunks(x_ref.shape[0], rows_fn)


def _post_add_rows(x_ref, o_fn, g_ref, gate_ref, out_ref):
    gg = gate_ref[...] * g_ref[...]

    def rows_fn(rows):
        o = o_fn(rows)
        out_ref[rows, :] = x_ref[rows, :] + (o * _rms(o)) * gg

    _for_row_chunks(x_ref.shape[0], rows_fn)


def _premod_mm3_kernel(x_ref, g_ref, sh_ref, sc_ref, wa_ref, wb_ref, wc_ref, oa_ref, ob_ref, oc_ref, h_scr):
    @pl.when(pl.program_id(1) == 0)
    def _():
        _premod_rows(x_ref, g_ref, sh_ref, sc_ref, h_scr)

    h = h_scr[...]
    for w_ref, o_ref in ((wa_ref, oa_ref), (wb_ref, ob_ref), (wc_ref, oc_ref)):
        o_ref[...] = jnp.dot(h, w_ref[...], preferred_element_type=F32)


def _premod_mm3(x, m_l, g, w, layer, *, seq_len, latent):
    t, d = x.shape
    tm = min(1024, seq_len if latent else t)
    tn = PROJ_TN
    nj = d // tn
    rowfn = _row_fn(tm, seq_len, latent)
    wspec = lambda o: pl.BlockSpec((None, None, d, tn), lambda i, j: (layer, o * nj + j, 0, 0))
    ospec = pl.BlockSpec((tm, tn), lambda i, j: (i, j))
    osh = jax.ShapeDtypeStruct((t, d), F32)
    return pl.pallas_call(
        _premod_mm3_kernel,
        grid=(t // tm, nj),
        in_specs=[
            _lookahead_rows_spec(tm, d, t // tm),
            pl.BlockSpec((1, d), lambda i, j: (0, 0)),
            _mod_spec(d, 0, rowfn),
            _mod_spec(d, 1, rowfn),
            wspec(0), wspec(1), wspec(2),
        ],
        out_specs=[ospec, ospec, ospec],
        out_shape=[osh, osh, osh],
        scratch_shapes=[pltpu.VMEM((tm, d), BF16)],
        compiler_params=_cparams(("parallel", "arbitrary")),
        name="premod_mm3",
    )(x, g.reshape(1, d), m_l, m_l, w, w, w)


def _premod_kernel(x_ref, g_ref, sh_ref, sc_ref, o_ref):
    _premod_rows(x_ref, g_ref, sh_ref, sc_ref, o_ref)


def _premod_time_major(x, m_l, g, *, seq_len, latent):
    t, d = x.shape
    nb = t // seq_len
    tm = min(512, seq_len)
    per_seq = seq_len // tm
    rowfn = _row_fn(tm, seq_len, latent)
    return pl.pallas_call(
        _premod_kernel,
        grid=(t // tm,),
        in_specs=[
            pl.BlockSpec((tm, d), lambda i: (i, 0)),
            pl.BlockSpec((1, d), lambda i: (0, 0)),
            _mod_spec(d, 0, rowfn),
            _mod_spec(d, 1, rowfn),
        ],
        out_specs=pl.BlockSpec((tm, d), lambda i: (i % per_seq, i // per_seq)),
        out_shape=jax.ShapeDtypeStruct((seq_len, nb * d), F32),
        compiler_params=_cparams(("parallel",)),
        name="premod",
    )(x, g.reshape(1, d), m_l, m_l)


def _mlp_kernel(x_ref, g2_ref, sh_ref, sc_ref, gate_ref, g3_ref, w1_ref, w2_ref, o_ref, h_scr, acc):
    f = pl.program_id(1)

    @pl.when(f == 0)
    def _():
        _premod_rows(x_ref, g2_ref, sh_ref, sc_ref, h_scr)
        acc[...] = jnp.zeros_like(acc)

    def hidden_tile():
        a = jnp.dot(h_scr[...], w1_ref[...], preferred_element_type=F32)
        a = jnp.square(jnp.maximum(a, 0.0)).astype(BF16)
        acc[...] += jnp.dot(a, w2_ref[...], preferred_element_type=F32)

    last = pl.num_programs(1) - 1

    @pl.when(f < last)
    def _():
        hidden_tile()

    @pl.when(f == last)
    def _():
        hidden_tile()
        _post_add_rows(x_ref, lambda rows: acc[rows, :], g3_ref, gate_ref, o_ref)


def _mlp(x, m_l, g2, g3, w1, w2, layer, *, seq_len, latent):
    t, d = x.shape
    tf = MLP_TF
    dff = w1.shape[1] * tf
    tm = min(512, t)
    rowfn = _row_fn(tm, seq_len, latent)
    return pl.pallas_call(
        _mlp_kernel,
        grid=(t // tm, dff // tf),
        in_specs=[
            pl.BlockSpec((tm, d), lambda i, f: (i, 0)),
            pl.BlockSpec((1, d), lambda i, f: (0, 0)),
            _mod_spec(d, 3, rowfn),
            _mod_spec(d, 4, rowfn),
            _mod_spec(d, 5, rowfn),
            pl.BlockSpec((1, d), lambda i, f: (0, 0)),
            pl.BlockSpec((None, None, d, tf), lambda i, f: (layer, f, 0, 0)),
            pl.BlockSpec((None, tf, d), lambda i, f: (layer, f, 0)),
        ],
        out_specs=pl.BlockSpec((tm, d), lambda i, f: (i, 0)),
        out_shape=jax.ShapeDtypeStruct((t, d), F32),
        scratch_shapes=[pltpu.VMEM((tm, d), BF16), pltpu.VMEM((tm, d), F32)],
        compiler_params=_cparams(("parallel", "arbitrary")),
        name="mlp",
    )(x, g2.reshape(1, d), m_l, m_l, m_l, g3.reshape(1, d), w1, w2)


def _mm_postadd_kernel(a_ref, w_ref, b_ref, x_ref, gate_ref, g_ref, o_ref, acc):
    acc[...] = jnp.dot(a_ref[...], w_ref[...], preferred_element_type=F32)
    b = b_ref[...]
    _post_add_rows(x_ref, lambda rows: acc[rows, :] + b, g_ref, gate_ref, o_ref)


def _mm_postadd(a, w, layer, b, x, m_l, g1, *, seq_len, latent):
    t, d = x.shape
    tm = min(512, t)
    rowfn = _row_fn(tm, seq_len, latent)
    return pl.pallas_call(
        _mm_postadd_kernel,
        grid=(t // tm,),
        in_specs=[
            pl.BlockSpec((tm, d), lambda i: (i, 0)),
            pl.BlockSpec((None, d, d), lambda i: (layer, 0, 0)),
            pl.BlockSpec((1, d), lambda i: (0, 0)),
            pl.BlockSpec((tm, d), lambda i: (i, 0)),
            _mod_spec(d, 2, rowfn),
            pl.BlockSpec((1, d), lambda i: (0, 0)),
        ],
        out_specs=pl.BlockSpec((tm, d), lambda i: (i, 0)),
        out_shape=jax.ShapeDtypeStruct((t, d), F32),
        scratch_shapes=[pltpu.VMEM((tm, d), F32)],
        compiler_params=_cparams(("parallel",)),
        name="mm_postadd",
    )(a, w, b.reshape(1, d), x, m_l, g1.reshape(1, d))


def _dft_tables(L):
    idx = np.arange(L, dtype=np.int64)
    ang = np.pi * ((idx[:, None] * idx[None, :]) % (2 * L)).astype(np.float64) / L
    c = np.cos(ang)
    s = np.sin(ang)
    alt = np.where(idx % 2 == 0, 1.0, -1.0)
    sp = s.copy()
    sp[0, :] = alt
    wf = np.full((L,), 2.0)
    wf[0] = 1.0
    ci = c * wf[None, :] / (2 * L)
    sip = -s * 2.0 / (2 * L)
    sip[:, 0] = alt / (2 * L)
    f32 = lambda a: jnp.asarray(a.astype(np.float32))
    return f32(c), f32(s), f32(sp), f32(ci), f32(sip)


def _hy_pe(L):
    t = np.arange(L, dtype=np.float64)
    periods = HY_PE_MIN_PERIOD * (HY_PE_MAX_PERIOD / HY_PE_MIN_PERIOD) ** (
        np.arange(HY_PE_BANDS, dtype=np.float64) / (HY_PE_BANDS - 1))
    ang = t[:, None] * (2.0 * math.pi / periods)[None]
    return jnp.asarray(np.concatenate([np.sin(ang), np.cos(ang)], axis=-1).astype(np.float32))


def _dot_bf16x3(a_hi, a_lo, b):
    b_hi, b_lo = _split_bf16(b)
    n = b.shape[1]
    both = jnp.dot(a_hi, jnp.concatenate([b_hi, b_lo], axis=1), preferred_element_type=F32)
    return both[:, :n] + both[:, n:] + jnp.dot(a_lo, b_hi, preferred_element_type=F32)


def _hy_filter_kernel(pe_ref, w1_ref, b1_ref, fr1_ref, w2_ref, b2_ref, fr2_ref, w3f_ref, w3b_ref,
                      la_ref, chi_ref, clo_ref, shi_ref, slo_ref, kre_ref, kim_ref, h_scr):
    L, dc = kre_ref.shape

    @pl.when(pl.program_id(0) == 0)
    def _():
        h1 = jnp.sin(fr1_ref[...] * (jnp.dot(pe_ref[...], w1_ref[...], precision=HIGHEST,
                                             preferred_element_type=F32) + b1_ref[...]))
        h_scr[...] = jnp.sin(fr2_ref[...] * (jnp.dot(h1, w2_ref[...], precision=HIGHEST,
                                                     preferred_element_type=F32) + b2_ref[...]))

    h = h_scr[...]
    row = lax.broadcasted_iota(jnp.int32, (L, dc), 0)
    dec = jnp.exp(-jnp.exp(la_ref[...]) * row.astype(F32))
    kf = jnp.dot(h, w3f_ref[...], precision=HIGHEST, preferred_element_type=F32) * dec
    kb = jnp.dot(h, w3b_ref[...], precision=HIGHEST, preferred_element_type=F32) * dec
    kb = jnp.where(row == 0, 0.0, kb)
    norm = jnp.sum(jnp.abs(kf) + jnp.abs(kb), axis=0, keepdims=True) + 1e-6
    inv = 1.0 / norm
    ks = (kf + kb) * inv
    kd = (kb - kf) * inv
    kre = _dot_bf16x3(chi_ref[...], clo_ref[...], ks)
    kim = _dot_bf16x3(shi_ref[...], slo_ref[...], kd)
    alt = jnp.where(row % 2 == 0, 1.0, -1.0)
    nyq = jnp.sum(alt * ks, axis=0, keepdims=True)
    kre_ref[...] = kre
    kim_ref[...] = jnp.where(row == 0, nyq, kim)


def _hy_filter(L, w1, b1, fr1, w2, b2, fr2, w3, log_alpha, cs_split):
    d = log_alpha.shape[-1]
    fw = w1.shape[1]
    dc = 512
    nd = d // dc
    full = lambda a: pl.BlockSpec(a.shape, lambda j: (0,) * a.ndim)
    pe = _hy_pe(L)
    b1, fr1, b2, fr2 = (a.reshape(1, fw) for a in (b1, fr1, b2, fr2))
    osh = jax.ShapeDtypeStruct((L, d), F32)
    return pl.pallas_call(
        _hy_filter_kernel,
        grid=(nd,),
        in_specs=[full(pe), full(w1), full(b1), full(fr1), full(w2), full(b2), full(fr2),
                  pl.BlockSpec((fw, dc), lambda j: (0, j)),
                  pl.BlockSpec((fw, dc), lambda j: (0, nd + j)),
                  pl.BlockSpec((1, dc), lambda j: (0, j))] + [full(a) for a in cs_split],
        out_specs=[pl.BlockSpec((L, dc), lambda j: (0, j))] * 2,
        out_shape=[osh, osh],
        scratch_shapes=[pltpu.VMEM((L, fw), F32)],
        compiler_params=_cparams(("arbitrary",)),
        name="hyena_filter",
    )(pe, w1, b1, fr1, w2, b2, fr2, w3, w3, log_alpha.reshape(1, d), *cs_split)


def _hy_inproj_kernel(x_ref, g_ref, sh_ref, sc_ref, w0_ref, w1_ref, wv_ref, b0_ref, b1_ref, bv_ref,
                      s0_ref, s1_ref, sv_ref, c0_ref, c1_ref, cv_ref, ox_ref, ov_ref, h_scr, *, seq_len):
    @pl.when(pl.program_id(1) == 0)
    def _():
        _premod_rows(x_ref, g_ref, sh_ref, sc_ref, h_scr)

    h = h_scr[...]
    tm, tn = ox_ref.shape
    pos = lax.broadcasted_iota(jnp.int32, (tm, tn), 0) % seq_len
    first, last = pos == 0, pos == seq_len - 1

    def section(w_ref, b_ref, s_ref, c_ref):
        z = jnp.dot(h, w_ref[...], preferred_element_type=F32) + b_ref[...]
        zm = jnp.where(first, 0.0, pltpu.roll(z, 1, 0))
        zp = jnp.where(last, 0.0, pltpu.roll(z, tm - 1, 0))
        return zm * s_ref[0:1, :] + z * s_ref[1:2, :] + zp * s_ref[2:3, :] + c_ref[...]

    ox_ref[...] = section(w0_ref, b0_ref, s0_ref, c0_ref).astype(ox_ref.dtype)
    x1 = section(w1_ref, b1_ref, s1_ref, c1_ref)
    ov_ref[...] = (section(wv_ref, bv_ref, sv_ref, cv_ref) * x1).astype(ov_ref.dtype)


def _hy_inproj(x, m_l, g, w, layer, b, w_sh, b_sh, *, seq_len, latent):
    t, d = x.shape
    tm = min(1024, seq_len if latent else t)
    assert tm % seq_len == 0, "row tiles hold whole sequences, so the short conv needs no halo"
    tn = PROJ_TN
    nj = d // tn
    rowfn = _row_fn(tm, seq_len, latent)
    wspec = lambda o: pl.BlockSpec((None, None, d, tn), lambda i, j: (layer, o * nj + j, 0, 0))
    vspec = lambda rows: (lambda o: pl.BlockSpec((rows, tn), lambda i, j: (0, o * nj + j)))
    bspec, sspec = vspec(1), vspec(3)
    ospec = pl.BlockSpec((tm, tn), lambda i, j: (i, j))
    osh = jax.ShapeDtypeStruct((t, d), BF16)
    b = b.reshape(1, 3 * d)
    b_sh = b_sh.reshape(1, 3 * d)
    return pl.pallas_call(
        functools.partial(_hy_inproj_kernel, seq_len=seq_len),
        grid=(t // tm, nj),
        in_specs=[
            _lookahead_rows_spec(tm, d, t // tm),
            pl.BlockSpec((1, d), lambda i, j: (0, 0)),
            _mod_spec(d, 0, rowfn),
            _mod_spec(d, 1, rowfn),
            wspec(0), wspec(1), wspec(2), bspec(0), bspec(1), bspec(2),
            sspec(0), sspec(1), sspec(2), bspec(0), bspec(1), bspec(2),
        ],
        out_specs=[ospec, ospec],
        out_shape=[osh, osh],
        scratch_shapes=[pltpu.VMEM((tm, d), BF16)],
        compiler_params=_cparams(("parallel", "arbitrary")),
        name="hyena_inproj",
    )(x, g.reshape(1, d), m_l, m_l, w, w, w, b, b, b, w_sh, w_sh, w_sh, b_sh, b_sh, b_sh)


def _hy_conv_kernel(x0_ref, v_ref, kre_ref, kim_ref, skip_ref, c_ref, sp_ref, ci_ref, sip_ref, o_ref):
    sb, L, dc = o_ref.shape
    kre = kre_ref[...]
    kim = kim_ref[...]
    first = lax.broadcasted_iota(jnp.int32, (L, dc), 0) == 0
    for s in range(sb):
        vb = v_ref[s]
        a = jnp.dot(c_ref[...], vb, preferred_element_type=F32)
        bm = jnp.dot(sp_ref[...], vb, preferred_element_type=F32)
        bk = bm * kim
        yre = a * kre + jnp.where(first, 0.0, bk)
        yim = jnp.where(first, bk, a * kim - bm * kre)
        y = (jnp.dot(ci_ref[...], yre.astype(BF16), preferred_element_type=F32)
             + jnp.dot(sip_ref[...], yim.astype(BF16), preferred_element_type=F32))
        o_ref[s] = (x0_ref[s].astype(F32) * (y + vb.astype(F32) * skip_ref[...])).astype(BF16)


def _hy_conv(x0, v, kre, kim, skip, tables_bf16, *, seq_len):
    t, d = x0.shape
    L = seq_len
    nb = t // L
    sb = max(1, min(nb, 1024 // L))
    dc = 512
    nd = d // dc
    once = pl.Buffered(1)
    zspec = pl.BlockSpec((sb, L, dc), lambda j, b: (b, 0, j))
    kspec = pl.BlockSpec((L, dc), lambda j, b: (0, j), pipeline_mode=once)
    mspec = pl.BlockSpec((L, L), lambda j, b: (0, 0), pipeline_mode=once)
    return pl.pallas_call(
        _hy_conv_kernel,
        grid=(nd, nb // sb),
        in_specs=[zspec, zspec, kspec, kspec, pl.BlockSpec((1, dc), lambda j, b: (0, j)),
                  mspec, mspec, mspec, mspec],
        out_specs=zspec,
        out_shape=jax.ShapeDtypeStruct((nb, L, d), BF16),
        compiler_params=_cparams(("parallel", "arbitrary")),
        name="hyena_conv",
    )(x0.reshape(nb, L, d), v.reshape(nb, L, d), kre, kim, skip.reshape(1, d), *tables_bf16).reshape(t, d)


def _rope_tables(L, head_dim):
    rows = L // GRID_W
    row = np.repeat(np.arange(rows), GRID_W).astype(np.float64)
    col = np.tile(np.arange(GRID_W), rows).astype(np.float64)
    half = head_dim // 2
    inv = ROPE_BASE ** (-np.arange(0, half, 2, dtype=np.float64) / half)
    ar = row[:, None] * inv
    ac = col[:, None] * inv
    ang = np.concatenate([ar, ar, ac, ac], axis=-1)
    return (jnp.asarray(np.cos(ang).astype(np.float32)), jnp.asarray(np.sin(ang).astype(np.float32)))


def _rope(x, cos, sin):
    hd = x.shape[-1]
    q = hd // 4
    lane = lax.broadcasted_iota(jnp.int32, x.shape, 1)
    rot = jnp.where((lane % (2 * q)) < q, -pltpu.roll(x, hd - q, 1), pltpu.roll(x, q, 1))
    return x * cos + rot * sin


def _diff_lambda(lam_ref, lam_init):
    lp = lam_ref[...]
    s01 = jnp.sum(lp[0:1, :] * lp[1:2, :], axis=-1, keepdims=True)
    s23 = jnp.sum(lp[2:3, :] * lp[3:4, :], axis=-1, keepdims=True)
    return jnp.exp(s01) - jnp.exp(s23) + lam_init


def _diff_attend_head(q2, k_segs, v_segs, lam, g_sub, out_scale, hd):
    scale = hd ** -0.5
    outs = []
    for comp in range(2):
        qc = (q2[comp] * scale).astype(BF16)
        ss = [lax.dot_general(qc, ks[comp], (((1,), (1,)), ((), ())), preferred_element_type=F32)
              for ks in k_segs]
        m = ss[0].max(axis=-1, keepdims=True)
        for s in ss[1:]:
            m = jnp.maximum(m, s.max(axis=-1, keepdims=True))
        den = None
        pv = None
        for s, v in zip(ss, v_segs):
            e = jnp.exp(s - m)
            dsum = e.sum(axis=-1, keepdims=True)
            den = dsum if den is None else den + dsum
            part = jnp.dot(e.astype(BF16), v, preferred_element_type=F32)
            pv = part if pv is None else pv + part
        outs.append(pv * (1.0 / den))
    o = outs[0] - lam * outs[1]
    return o * _rms(o) * (g_sub * out_scale)


def _attn_ctx_kernel(q_ref, k_ref, v_ref, lam_ref, g_ref, o_ref, *, lam_init, hd):
    lam = _diff_lambda(lam_ref, lam_init)
    vd = 2 * hd
    for h in range(N_HEADS):
        c0 = h * vd
        q2 = [q_ref[:, c0:c0 + hd], q_ref[:, c0 + hd:c0 + vd]]
        k2 = (k_ref[:, c0:c0 + hd].astype(BF16), k_ref[:, c0 + hd:c0 + vd].astype(BF16))
        v = v_ref[:, c0:c0 + vd].astype(BF16)
        o = _diff_attend_head(q2, [k2], [v], lam, g_ref[...], 1.0 - lam_init, hd)
        o_ref[:, c0:c0 + vd] = o.astype(BF16)


def _attn_ctx(q, k, v, at_lam, g_sub, *, seq_len, lam_init):
    t, d = q.shape
    hd = d // N_HEADS // 2
    spec = pl.BlockSpec((seq_len, d), lambda b: (b, 0))
    return pl.pallas_call(
        functools.partial(_attn_ctx_kernel, lam_init=lam_init, hd=hd),
        grid=(t // seq_len,),
        in_specs=[spec, spec, spec,
                  pl.BlockSpec(at_lam.shape, lambda b: (0, 0)),
                  pl.BlockSpec((1, 2 * hd), lambda b: (0, 0))],
        out_specs=spec,
        out_shape=jax.ShapeDtypeStruct((t, d), BF16),
        compiler_params=_cparams(("parallel",)),
        name="attn_ctx",
    )(q, k, v, at_lam, g_sub.reshape(1, 2 * hd))


def _attn_lat_kernel(q_ref, k_ref, v_ref, ck_ref, cv_ref, cq_ref, sq_ref, ckk_ref, skk_ref,
                     lam_ref, g_ref, o_ref, kl_scr, kc_scr, vl_scr, vc_scr, *, lam_init, hd, sub):
    @pl.when(pl.program_id(2) == 0)
    def _():
        ckk, skk = ckk_ref[...], skk_ref[...]
        kl_scr[:, 0:hd] = _rope(k_ref[:, 0:hd], ckk, skk).astype(BF16)
        kl_scr[:, hd:2 * hd] = _rope(k_ref[:, hd:2 * hd], ckk, skk).astype(BF16)
        kc_scr[...] = ck_ref[...].astype(BF16)
        vl_scr[...] = v_ref[...].astype(BF16)
        vc_scr[...] = cv_ref[...].astype(BF16)

    lam = _diff_lambda(lam_ref, lam_init)
    kl = (kl_scr[:, 0:hd], kl_scr[:, hd:2 * hd])
    kc = (kc_scr[:, 0:hd], kc_scr[:, hd:2 * hd])
    for r0 in range(0, q_ref.shape[0], sub):
        rows = slice(r0, r0 + sub)
        cq, sq = cq_ref[rows, :], sq_ref[rows, :]
        q2 = [_rope(q_ref[rows, 0:hd], cq, sq), _rope(q_ref[rows, hd:2 * hd], cq, sq)]
        o = _diff_attend_head(q2, [kc, kl], [vc_scr[...], vl_scr[...]], lam, g_ref[...],
                              1.0 - lam_init, hd)
        o_ref[rows, :] = o.astype(BF16)


def _attn_lat(q, k, v, cache_k, cache_v, at_lam, g_sub, *, seq_len, lam_init):
    t, d = q.shape
    hd = d // N_HEADS // 2
    vd = 2 * hd
    nb = t // seq_len
    past = cache_k.shape[1]
    qb = min(1024, seq_len)
    nq = seq_len // qb
    cos, sin = _rope_tables(seq_len, hd)
    return pl.pallas_call(
        functools.partial(_attn_lat_kernel, lam_init=lam_init, hd=hd, sub=min(256, qb)),
        grid=(nb, N_HEADS, nq),
        in_specs=[
            pl.BlockSpec((qb, vd), lambda b, h, i: (b * nq + i, h)),
            pl.BlockSpec((seq_len, vd), lambda b, h, i: (b, h)),
            pl.BlockSpec((seq_len, vd), lambda b, h, i: (b, h)),
            pl.BlockSpec((None, past, vd), lambda b, h, i: (b, 0, h)),
            pl.BlockSpec((None, past, vd), lambda b, h, i: (b, 0, h)),
            pl.BlockSpec((qb, hd), lambda b, h, i: (i, 0)),
            pl.BlockSpec((qb, hd), lambda b, h, i: (i, 0)),
            pl.BlockSpec((seq_len, hd), lambda b, h, i: (0, 0)),
            pl.BlockSpec((seq_len, hd), lambda b, h, i: (0, 0)),
            pl.BlockSpec(at_lam.shape, lambda b, h, i: (0, 0)),
            pl.BlockSpec((1, vd), lambda b, h, i: (0, 0)),
        ],
        out_specs=pl.BlockSpec((qb, vd), lambda b, h, i: (b * nq + i, h)),
        out_shape=jax.ShapeDtypeStruct((t, d), BF16),
        scratch_shapes=[pltpu.VMEM((seq_len, vd), BF16), pltpu.VMEM((past, vd), BF16),
                        pltpu.VMEM((seq_len, vd), BF16), pltpu.VMEM((past, vd), BF16)],
        compiler_params=_cparams(("parallel", "parallel", "arbitrary")),
        name="attn_lat",
    )(q, k, v, cache_k, cache_v, cos, sin, cos, sin, at_lam, g_sub.reshape(1, vd))


def _s5_prep_kernel(lre_r, lim_r, ldt_r, btre_ref, btim_ref, ctre_ref, ctim_ref,
                    wst_ref, t_ref, wout_ref, lam_ref):
    Q = S5_CHUNK
    H = S5_GROUP
    P2 = lre_r.shape[-1]
    P = P2 // 2
    N = Q * H

    def cexp(n, re_dt, im_dt):
        mag = jnp.exp(n * re_dt)
        return mag * jnp.cos(n * im_dt), mag * jnp.sin(n * im_dt)

    re = jnp.minimum(lre_r[...], -1e-4)
    im = lim_r[...]
    dt = jnp.exp(ldt_r[...])
    re_dt, im_dt = re * dt, im * dt
    lb_re, lb_im = cexp(1.0, re_dt, im_dt)
    den = re * re + im * im
    q_re = ((lb_re - 1.0) * re + lb_im * im) / den
    q_im = (lb_im * re - (lb_re - 1.0) * im) / den
    bt_re, bt_im = btre_ref[...], btim_ref[...]
    bb_re = q_re * bt_re - q_im * bt_im
    bb_im = q_re * bt_im + q_im * bt_re
    fwd_lane = lax.broadcasted_iota(jnp.int32, (Q, P2), 1) < P
    srow = lax.broadcasted_iota(jnp.int32, (Q, P2), 0)
    n = jnp.where(fwd_lane, (Q - 1) - srow, srow).astype(F32)
    pw_re, pw_im = cexp(n, re_dt, im_dt)
    for s in range(Q):
        pr, pi = pw_re[s:s + 1, :], pw_im[s:s + 1, :]
        wst_ref[s * H:(s + 1) * H, 0:P2] = (bb_re * pr - bb_im * pi).astype(BF16)
        wst_ref[s * H:(s + 1) * H, P2:2 * P2] = (bb_re * pi + bb_im * pr).astype(BF16)
    lq_re, lq_im = cexp(float(Q), re_dt, im_dt)
    lam_ref[0:1, :] = lq_re
    lam_ref[1:2, :] = lq_im

    l1_re = jnp.transpose(jnp.broadcast_to(lb_re, (8, P2)))[:, 0:1]
    l1_im = jnp.transpose(jnp.broadcast_to(lb_im, (8, P2)))[:, 0:1]
    tlane = lax.broadcasted_iota(jnp.int32, (P2, N), 1) // H
    fwd_row = lax.broadcasted_iota(jnp.int32, (P2, N), 0) < P
    nt = jnp.where(fwd_row, tlane, (Q - 1) - tlane)
    g_re = jnp.ones((P2, N), F32)
    g_im = jnp.zeros((P2, N), F32)
    b_re, b_im = l1_re, l1_im
    bit = 1
    while bit < Q:
        use = (nt & bit) != 0
        f_re = jnp.where(use, b_re, 1.0)
        f_im = jnp.where(use, b_im, 0.0)
        g_re, g_im = g_re * f_re - g_im * f_im, g_re * f_im + g_im * f_re
        b_re, b_im = b_re * b_re - b_im * b_im, 2.0 * b_re * b_im
        bit *= 2
    c_re, c_im = ctre_ref[...], ctim_ref[...]
    gx_re = c_re * g_re - c_im * g_im
    gx_im = c_re * g_im + c_im * g_re
    wout_ref[0:P2, :] = (gx_re * l1_re - gx_im * l1_im).astype(BF16)
    wout_ref[P2:2 * P2, :] = (-(gx_re * l1_im + gx_im * l1_re)).astype(BF16)

    gx = jnp.concatenate([gx_re, gx_im], axis=0)
    is_f = lax.broadcasted_iota(jnp.int32, (H, P2), 1) < P
    zero = jnp.zeros((H, P2), F32)
    lhs_f = jnp.concatenate([jnp.where(is_f, bb_re, zero), jnp.where(is_f, -bb_im, zero)], axis=1)
    lhs_b = jnp.concatenate([jnp.where(is_f, zero, bb_re), jnp.where(is_f, zero, -bb_im)], axis=1)
    m_f = jnp.dot(lhs_f, gx, precision=HIGHEST, preferred_element_type=F32)
    m_b = jnp.dot(lhs_b, gx, precision=HIGHEST, preferred_element_type=F32)
    lane_n = lax.broadcasted_iota(jnp.int32, (H, N), 1)
    for s in range(Q):
        tf = m_f if s == 0 else pltpu.roll(m_f, s * H, 1)
        tb = m_b if s == Q - 1 else pltpu.roll(m_b, (s + 1) * H, 1)
        slab = jnp.where(lane_n >= s * H, tf, 0.0) + jnp.where(lane_n < (s + 1) * H, tb, 0.0)
        t_ref[s * H:(s + 1) * H, :] = slab.astype(BF16)


def _s5_prep(lam_re, lam_im, log_dt, b_re, b_im, c_re, c_im):
    _, G, P = lam_re.shape
    H = S5_GROUP
    N = S5_CHUNK * H
    P2 = 2 * P
    fb_lanes = lambda a: jnp.concatenate([a[0], a[1]], axis=-1)
    ldt = jnp.broadcast_to(log_dt[..., None], (2, G, P))
    rows = [fb_lanes(a).reshape(G, 1, P2) for a in (lam_re, lam_im, ldt)]
    bt = [fb_lanes(jnp.swapaxes(a, -1, -2)) for a in (b_re, b_im)]
    ct = [jnp.tile(jnp.concatenate([jnp.swapaxes(a[0], -1, -2), jnp.swapaxes(a[1], -1, -2)], axis=1),
                   (1, 1, S5_CHUNK)) for a in (c_re, c_im)]
    ins = rows + bt + ct
    gspec = lambda a: pl.BlockSpec((None,) + a.shape[1:], lambda g: (g, 0, 0))
    mspec = pl.BlockSpec((None, N, N), lambda g: (g, 0, 0))
    msh = jax.ShapeDtypeStruct((G, N, N), BF16)
    return pl.pallas_call(
        _s5_prep_kernel,
        grid=(G,),
        in_specs=[gspec(a) for a in ins],
        out_specs=[mspec, mspec, mspec, pl.BlockSpec((None, 2, P2), lambda g: (g, 0, 0))],
        out_shape=[msh, msh, msh, jax.ShapeDtypeStruct((G, 2, P2), F32)],
        compiler_params=_cparams(("parallel",)),
        name="s5_prep",
    )(*ins)


def _s5_core_kernel(h_ref, wst_ref, t_ref, wout_ref, lam_ref, s0_ref, y_ref, fin_ref,
                    u_scr, loc, sa, sb, *, nb, nc, gl):
    Q, H, GB = S5_CHUNK, S5_GROUP, S5_GROUPS_PER_STEP
    R = nb * nc
    P2 = lam_ref.shape[-1]
    lane_blk = lax.broadcasted_iota(jnp.int32, (R, LANES), 1) // H

    def block_transpose(arrs):
        arrs = list(arrs)
        dist = GB // 2
        while dist >= 1:
            upper = (lane_blk & dist) != 0
            nxt = list(arrs)
            for a in range(GB):
                if a & dist:
                    continue
                lo, hi = arrs[a], arrs[a + dist]
                nxt[a] = jnp.where(upper, pltpu.roll(hi, dist * H, 1), lo)
                nxt[a + dist] = jnp.where(upper, hi, pltpu.roll(lo, LANES - dist * H, 1))
            arrs = nxt
            dist //= 2
        return arrs

    hi_mask = jnp.int32(-65536)

    def pack2(a, b):
        abits = lax.bitcast_convert_type(a.astype(BF16).astype(F32), jnp.int32)
        bbits = lax.bitcast_convert_type(b.astype(BF16).astype(F32), jnp.int32)
        return (abits & hi_mask) | lax.shift_right_logical(bbits, 16)

    def unpack2(p):
        return [lax.bitcast_convert_type(p & hi_mask, F32),
                lax.bitcast_convert_type(lax.shift_left(p, 16), F32)]

    packed = [pack2(h_ref[:, s, :, :].reshape(R, LANES), h_ref[:, s + GB, :, :].reshape(R, LANES))
              for s in range(GB)]
    for g, up in enumerate(block_transpose(packed)):
        u = jnp.concatenate(unpack2(up), axis=-1).astype(BF16)
        u_scr[g] = u
        loc[g] = jnp.dot(u, wst_ref[g], preferred_element_type=F32)

    fwd_half = lax.broadcasted_iota(jnp.int32, (nb, P2), 1) < P2 // 2
    for g0 in range(0, GB, gl):
        def body(k, carry, g0=g0):
            rf = pl.multiple_of(k * nb, nb)
            rb = pl.multiple_of((nc - 1 - k) * nb, nb)
            out = []
            for gi in range(gl):
                g = g0 + gi
                xr, xi = carry[2 * gi], carry[2 * gi + 1]
                sa[g, pl.ds(rf, nb), 0:P2] = xr
                sa[g, pl.ds(rf, nb), P2:2 * P2] = xi
                sb[g, pl.ds(rb, nb), 0:P2] = xr
                sb[g, pl.ds(rb, nb), P2:2 * P2] = xi
                lr = jnp.where(fwd_half, loc[g, pl.ds(rf, nb), 0:P2], loc[g, pl.ds(rb, nb), 0:P2])
                li = jnp.where(fwd_half, loc[g, pl.ds(rf, nb), P2:2 * P2],
                               loc[g, pl.ds(rb, nb), P2:2 * P2])
                ar, ai = lam_ref[g, 0:1, :], lam_ref[g, 1:2, :]
                out += [ar * xr - ai * xi + lr, ar * xi + ai * xr + li]
            return tuple(out)

        init = []
        for gi in range(gl):
            init += [s0_ref[g0 + gi, :, 0:P2], s0_ref[g0 + gi, :, P2:2 * P2]]
        fin = lax.fori_loop(0, nc, body, tuple(init))
        for gi in range(gl):
            fin_ref[g0 + gi, :, 0:P2] = fin[2 * gi]
            fin_ref[g0 + gi, :, P2:2 * P2] = fin[2 * gi + 1]

    fsel = (lax.broadcasted_iota(jnp.int32, (R, 2 * P2), 1) % P2) < P2 // 2
    for g in range(GB):
        s_in = jnp.where(fsel, sa[g], sb[g]).astype(BF16)
        loc[g] = (jnp.dot(u_scr[g], t_ref[g], preferred_element_type=F32)
                  + jnp.dot(s_in, wout_ref[g], preferred_element_type=F32))
    ys = [pack2(loc[g, :, 0:LANES], loc[g, :, LANES:2 * LANES]) for g in range(GB)]
    for tl, slab in enumerate(block_transpose(ys)):
        lo, hi = unpack2(slab)
        y_ref[:, tl, :, :] = lo.reshape(nc, nb, LANES)
        y_ref[:, tl + GB, :, :] = hi.reshape(nc, nb, LANES)


def _s5_core(h_tm, s0, wst, tmat, wout, lam, *, nb, seq_len):
    d = h_tm.shape[1] // nb
    H, Q, GB = S5_GROUP, S5_CHUNK, S5_GROUPS_PER_STEP
    G = d // H
    nc = seq_len // Q
    R = nb * nc
    N = Q * H
    P4 = s0.shape[-1]
    gl = max(1, min(GB, (8 * GB) // nb))
    h4 = h_tm.reshape(nc, Q, nb, d)
    hspec = pl.BlockSpec((nc, Q, nb, LANES), lambda gb: (0, 0, 0, gb))
    mspec = pl.BlockSpec((GB, N, N), lambda gb: (gb, 0, 0))
    sspec = pl.BlockSpec((GB, nb, P4), lambda gb: (gb, 0, 0))
    y, fin = pl.pallas_call(
        functools.partial(_s5_core_kernel, nb=nb, nc=nc, gl=gl),
        grid=(G // GB,),
        in_specs=[hspec, mspec, mspec, mspec,
                  pl.BlockSpec((GB, 2, P4 // 2), lambda gb: (gb, 0, 0)), sspec],
        out_specs=[hspec, sspec],
        out_shape=[jax.ShapeDtypeStruct((nc, Q, nb, d), F32), jax.ShapeDtypeStruct((G, nb, P4), F32)],
        scratch_shapes=[pltpu.VMEM((GB, R, N), BF16), pltpu.VMEM((GB, R, N), F32),
                        pltpu.VMEM((GB, R, P4), F32), pltpu.VMEM((GB, R, P4), F32)],
        compiler_params=_cparams(("parallel",)),
        name="s5_core",
    )(h4, wst, tmat, wout, lam, s0)
    return y.reshape(seq_len, nb * d), fin


def _glu_postadd_kernel(*refs, nsub):
    h_refs, y_refs = refs[:nsub], refs[nsub:2 * nsub]
    (d_ref, wa_ref, wg_ref, ba_ref, bg_ref, x_ref, gate_ref, g_ref, o_ref, acc_a, acc_g, u_scr) = refs[2 * nsub:]
    k = pl.program_id(1)

    @pl.when(k == 0)
    def _():
        acc_a[...] = jnp.zeros_like(acc_a)
        acc_g[...] = jnp.zeros_like(acc_g)

    def contraction_tile():
        rows = h_refs[0].shape[0]
        for s in range(nsub):
            u_scr[s * rows:(s + 1) * rows, :] = jax.nn.gelu(
                d_ref[...] * h_refs[s][...] + y_refs[s][...]).astype(BF16)
        u = u_scr[...]
        acc_a[...] += jnp.dot(u, wa_ref[...], preferred_element_type=F32)
        acc_g[...] += jnp.dot(u, wg_ref[...], preferred_element_type=F32)

    last = pl.num_programs(1) - 1

    @pl.when(k < last)
    def _():
        contraction_tile()

    @pl.when(k == last)
    def _():
        contraction_tile()
        ba, bg = ba_ref[...], bg_ref[...]
        _post_add_rows(x_ref, lambda rows: (acc_a[rows, :] + ba) * jax.nn.sigmoid(acc_g[rows, :] + bg),
                       g_ref, gate_ref, o_ref)


def _glu_postadd(h_tm, y_tm, dskip, w_glu, layer, b_glu, x, m_l, g1, *, seq_len, latent):
    t, d = x.shape
    tm = min(512, t)
    nsub = max(1, tm // seq_len)
    per_seq = max(1, seq_len // tm)
    sub_rows = tm // nsub
    tk = 1024
    nk = d // tk
    rowfn = _row_fn(tm, seq_len, latent)
    b_glu = b_glu.reshape(1, 2 * d)
    tspecs = [pl.BlockSpec((sub_rows, tk), lambda i, k, s=s: (i % per_seq, ((i // per_seq) * nsub + s) * nk + k))
              for s in range(nsub)]
    return pl.pallas_call(
        functools.partial(_glu_postadd_kernel, nsub=nsub),
        grid=(t // tm, nk),
        in_specs=tspecs + tspecs + [
            pl.BlockSpec((1, tk), lambda i, k: (0, k)),
            pl.BlockSpec((None, None, tk, d), lambda i, k: (layer, 0, k, 0)),
            pl.BlockSpec((None, None, tk, d), lambda i, k: (layer, 1, k, 0)),
            pl.BlockSpec((1, d), lambda i, k: (0, 0)),
            pl.BlockSpec((1, d), lambda i, k: (0, 1)),
            pl.BlockSpec((tm, d), lambda i, k: (i, 0)),
            _mod_spec(d, 2, rowfn),
            pl.BlockSpec((1, d), lambda i, k: (0, 0)),
        ],
        out_specs=pl.BlockSpec((tm, d), lambda i, k: (i, 0)),
        out_shape=jax.ShapeDtypeStruct((t, d), F32),
        scratch_shapes=[pltpu.VMEM((tm, d), F32), pltpu.VMEM((tm, d), F32), pltpu.VMEM((tm, tk), BF16)],
        compiler_params=_cparams(("parallel", "arbitrary")),
        name="glu_postadd",
    )(*([h_tm] * nsub), *([y_tm] * nsub), dskip.reshape(1, d), w_glu, w_glu, b_glu, b_glu, x, m_l,
      g1.reshape(1, d))


def kernel(x_prompt, x_sample, cache_attn_k, cache_attn_v, state_s5_re, state_s5_im, c, c_ctx, w_mod, b_mod, g_norm, w_mlp_in, w_mlp_out, hy_w_in, hy_b_in, hy_w_short, hy_b_short, hy_f_w1, hy_f_b1, hy_f_freq1, hy_f_w2, hy_f_b2, hy_f_freq2, hy_f_w3, hy_log_alpha, hy_skip, hy_w_out, hy_b_out, at_w_qkv, at_lam, at_g_sub, at_w_o, s5_lam_re, s5_lam_im, s5_log_dt, s5_b_re, s5_b_im, s5_c_re, s5_c_im, s5_d, s5_w_glu, s5_b_glu):
    bc, lc, d = x_prompt.shape
    bl, ll, _ = x_sample.shape
    depth = w_mod.shape[0]
    assert 1 + bl <= MOD_ROWS
    assert cache_attn_k.shape[1] == 1 and state_s5_re.shape[1] == 1, "one attention and one S5 layer"
    hd = d // N_HEADS // 2
    G = d // S5_GROUP
    P = s5_lam_re.shape[-1]

    cond = jnp.concatenate([c_ctx[None], c, jnp.zeros((MOD_ROWS - 1 - bl, d), F32)], axis=0)
    mod = _modulation(cond, w_mod, b_mod).reshape(depth, MOD_ROWS, N_MOD, 1, d)

    streams = [dict(seq_len=lc, latent=False), dict(seq_len=ll, latent=True)]
    xs = [x_prompt.reshape(bc * lc, d), x_sample.reshape(bl * ll, d)]

    tables = {}
    for L in {lc, ll}:
        cm, sm, sp, ci, sip = _dft_tables(L)
        tables[L] = (_split_bf16(cm) + _split_bf16(sm), tuple(a.astype(BF16) for a in (cm, sp, ci, sip)))

    w_mlp_in_b = _cast_tiles(w_mlp_in, MLP_TF)
    w_mlp_out_b = _cast_tiles(w_mlp_out.reshape(-1, MLP_TF, d), d).reshape(w_mlp_out.shape)
    hy_w_in_b = _cast_tiles(hy_w_in, PROJ_TN)
    at_w_qkv_b = _cast_tiles(at_w_qkv, PROJ_TN)
    hy_w_out_b = _cast_tiles(hy_w_out, d).reshape(hy_w_out.shape)
    at_w_o_b = _cast_tiles(at_w_o, d).reshape(at_w_o.shape)
    s5_w_glu_b = _cast_tiles(s5_w_glu, d)

    new_k = new_v = None
    fin_ctx = None
    for i in range(depth):
        kind, j = i % 3, i // 3
        m_l = mod[i]
        g = g_norm[i]
        if kind == 0:
            filt = {}
            for L in {lc, ll}:
                filt[L] = _hy_filter(L, hy_f_w1[j], hy_f_b1[j], hy_f_freq1[j], hy_f_w2[j], hy_f_b2[j],
                                     hy_f_freq2[j], hy_f_w3[j], hy_log_alpha[j], tables[L][0])
            for si, st in enumerate(streams):
                L = st["seq_len"]
                x0, vg = _hy_inproj(xs[si], m_l, g[0], hy_w_in_b, j, hy_b_in[j], hy_w_short[j],
                                    hy_b_short[j], **st)
                a = _hy_conv(x0, vg, filt[L][0], filt[L][1], hy_skip[j], tables[L][1], seq_len=L)
                xs[si] = _mm_postadd(a, hy_w_out_b, j, hy_b_out[j], xs[si], m_l, g[1], **st)
        elif kind == 1:
            lam_init = 0.8 - 0.6 * math.exp(-0.3 * i)
            for si, st in enumerate(streams):
                L = st["seq_len"]
                q, k, v = _premod_mm3(xs[si], m_l, g[0], at_w_qkv_b, j, **st)
                if not st["latent"]:
                    new_k = k.reshape(bc, 1, lc, N_HEADS, 2, hd)
                    new_v = v.reshape(bc, 1, lc, N_HEADS, 2 * hd)
                    a = _attn_ctx(q, k, v, at_lam[j], at_g_sub[j], seq_len=L, lam_init=lam_init)
                else:
                    ck = cache_attn_k[:, j].reshape(bl, -1, d)
                    cv = cache_attn_v[:, j].reshape(bl, -1, d)
                    a = _attn_lat(q, k, v, ck, cv, at_lam[j], at_g_sub[j], seq_len=L, lam_init=lam_init)
                xs[si] = _mm_postadd(a, at_w_o_b, j, jnp.zeros((d,), F32), xs[si], m_l, g[1], **st)
        else:
            wst, tmat, wout, lam_q = _s5_prep(s5_lam_re[j], s5_lam_im[j], s5_log_dt[j], s5_b_re[j],
                                              s5_b_im[j], s5_c_re[j], s5_c_im[j])
            for si, st in enumerate(streams):
                L = st["seq_len"]
                nb = xs[si].shape[0] // L
                if st["latent"]:
                    sre, sim = state_s5_re[:, j], state_s5_im[:, j]
                    s0 = jnp.concatenate([sre[:, 0], sre[:, 1], sim[:, 0], sim[:, 1]], axis=-1)
                    s0 = s0.transpose(1, 0, 2)
                else:
                    s0 = jnp.zeros((G, nb, 4 * P), F32)
                h_tm = _premod_time_major(xs[si], m_l, g[0], **st)
                y_tm, fin = _s5_core(h_tm, s0, wst, tmat, wout, lam_q, nb=nb, seq_len=L)
                if not st["latent"]:
                    fin_ctx = fin.reshape(G, nb, 2, 2, P).transpose(1, 2, 3, 0, 4)
                xs[si] = _glu_postadd(h_tm, y_tm, s5_d[j], s5_w_glu_b, j, s5_b_glu[j], xs[si], m_l, g[1], **st)
        for si, st in enumerate(streams):
            xs[si] = _mlp(xs[si], m_l, g[2], g[3], w_mlp_in_b, w_mlp_out_b, i, **st)

    new_s_re = fin_ctx[:, 0][:, None]
    new_s_im = fin_ctx[:, 1][:, None]
    return (xs[0].reshape(bc, lc, d), xs[1].reshape(bl, ll, d), new_k, new_v, new_s_re, new_s_im)
```
